```python
import math, functools
import jax, jax.numpy as jnp
from jax import lax
import numpy as np

D_MODEL = 1024
BATCH = 16
SEQ = 2048
DEPTH = 1
DEC_BATCH = 16
DEC_SEQ = 16
PAST_LEN = 2048

CHUNK = 64
N_HEADS = 8
HEAD_DIM = 64
V_DIM = 2 * HEAD_DIM
ATTN_WIDTH = N_HEADS * V_DIM
POOL_WINDOWS = (2, 4, 8, 16)
POOL_GROUPS = len(POOL_WINDOWS)
POOL_WIDTH = D_MODEL // 2
POOL_GROUP_DIM = POOL_WIDTH // POOL_GROUPS
POOL_OUT_DIM = D_MODEL // POOL_GROUPS
POOL_STATE = max(POOL_WINDOWS) - 1
IN_COLS = 3 * ATTN_WIDTH + POOL_WIDTH + 2 * D_MODEL
Q_BLOCK = 128
N_EXPERTS = 32
TOP_K = 4
D_FF = D_MODEL
SWIGLU_LIMIT = 7.0
SWIGLU_ALPHA = 1.702
EXPERT_BLOCK = 256
RMS_EPS = 1e-6
NEG_INF = -1e30

kernel_name = "hybrid_stream_diffattn_pool_moe_step"


def _rms_norm(x, g):
    xf = x.astype(jnp.float32)
    y = xf * lax.rsqrt(jnp.mean(xf * xf, axis=-1, keepdims=True) + RMS_EPS)
    return (y * g.astype(jnp.float32)).astype(x.dtype)


def _diff_attention(q, k, v, q_pos, k_pos, lam, slopes):
    s = jnp.einsum('bqhmd,bkhmd->mbhqk', q, k).astype(jnp.float32) * (HEAD_DIM ** -0.5)
    dist = jnp.abs(q_pos[:, None] - k_pos[None, :]).astype(jnp.float32)
    visible = (k_pos[None, :] // CHUNK) <= (q_pos[:, None] // CHUNK)
    bias = jnp.where(visible[None], -slopes[:, None, None] * dist[None], NEG_INF)
    p = jax.nn.softmax(s + bias[None, None], axis=-1)
    w = p[0] - lam * p[1]
    return jnp.einsum('bhqk,bkhe->bqhe', w, v.astype(jnp.float32))


def _multiscale_pool(u, prefix, start_pos, pool_w, pool_scale):
    B, T, _ = u.shape
    ext = jnp.concatenate([prefix.astype(u.dtype), u], axis=1)
    extf = ext.astype(jnp.float32)
    cs = jnp.cumsum(extf, axis=1)
    cs = jnp.concatenate([jnp.zeros_like(cs[:, :1]), cs], axis=1)
    pos = start_pos + jnp.arange(T, dtype=jnp.int32)
    hi = cs[:, POOL_STATE + 1:POOL_STATE + 1 + T]
    own = extf[:, POOL_STATE:]
    groups = []
    for g, w in enumerate(POOL_WINDOWS):
        sl = slice(g * POOL_GROUP_DIM, (g + 1) * POOL_GROUP_DIM)
        lo = cs[:, POOL_STATE + 1 - w:POOL_STATE + 1 - w + T, sl]
        cnt = jnp.minimum(w, pos + 1).astype(jnp.float32)[None, :, None]
        groups.append((hi[..., sl] - lo) / cnt - own[..., sl])
    z = jnp.stack(groups, axis=2)
    y = jnp.einsum('btgc,gcd->btgd', z, pool_w.astype(jnp.float32)).reshape(B, T, D_MODEL)
    return y * pool_scale.astype(jnp.float32), ext[:, -POOL_STATE:]


def _token_mixer(x, past_k, past_v, past_u, lam, lam_init, norm_g, w_in, q_norm_g, k_norm_g,
                 subln_g, pool_w, pool_scale, w_out):
    B, T, _ = x.shape
    past_len = past_k.shape[1]
    xn = _rms_norm(x, norm_g)
    proj = xn @ w_in
    A = ATTN_WIDTH
    q = _rms_norm(proj[..., :A].reshape(B, T, N_HEADS, 2, HEAD_DIM), q_norm_g)
    k = _rms_norm(proj[..., A:2 * A].reshape(B, T, N_HEADS, 2, HEAD_DIM), k_norm_g)
    v = proj[..., 2 * A:3 * A].reshape(B, T, N_HEADS, V_DIM)
    u = proj[..., 3 * A:3 * A + POOL_WIDTH]
    gate_a = proj[..., 3 * A + POOL_WIDTH:3 * A + POOL_WIDTH + D_MODEL]
    gate_b = proj[..., 3 * A + POOL_WIDTH + D_MODEL:]
    k_rows = k.reshape(B, T, N_HEADS, V_DIM)
    k_all = jnp.concatenate([past_k.astype(k_rows.dtype), k_rows], axis=1)
    k_all = k_all.reshape(B, past_len + T, N_HEADS, 2, HEAD_DIM)
    v_all = jnp.concatenate([past_v.astype(v.dtype), v], axis=1)
    k_pos = jnp.arange(past_len + T, dtype=jnp.int32)
    q_pos = past_len + jnp.arange(T, dtype=jnp.int32)
    slopes = jnp.exp2(-(8.0 / N_HEADS) * jnp.arange(1, N_HEADS + 1, dtype=jnp.float32))
    if T <= Q_BLOCK:
        o = _diff_attention(q, k_all, v_all, q_pos, k_pos, lam, slopes)
    else:
        nb = T // Q_BLOCK
        q_blk = jnp.moveaxis(q.reshape(B, nb, Q_BLOCK, N_HEADS, 2, HEAD_DIM), 1, 0)
        pos_blk = q_pos.reshape(nb, Q_BLOCK)
        o = lax.map(lambda a: _diff_attention(a[0], k_all, v_all, a[1], k_pos, lam, slopes),
                    (q_blk, pos_blk))
        o = jnp.moveaxis(o, 0, 1).reshape(B, T, N_HEADS, V_DIM)
    attn = (_rms_norm(o, subln_g) * (1.0 - lam_init)).reshape(B, T, ATTN_WIDTH)
    pool, u_tail = _multiscale_pool(u, past_u, past_len, pool_w, pool_scale)
    merged = (jax.nn.sigmoid(gate_a.astype(jnp.float32)) * attn
              + jax.nn.sigmoid(gate_b.astype(jnp.float32)) * pool)
    y = merged.astype(x.dtype) @ w_out
    return y, k_rows, v, u_tail


def _moe_ffn(x, router_w, router_b, w_gate, b_gate, w_up, b_up, w_down, b_down):
    B, T, D = x.shape
    n_tok = B * T
    xt = x.reshape(n_tok, D)
    logits = (xt @ router_w).astype(jnp.float32) + router_b.astype(jnp.float32)
    top_val, top_idx = lax.top_k(logits, TOP_K)
    gates = jax.nn.softmax(top_val, axis=-1)
    n_assign = n_tok * TOP_K
    n_blocks = -(-n_assign // EXPERT_BLOCK) + N_EXPERTS
    expert_of = top_idx.reshape(-1).astype(jnp.int32)
    order = jnp.argsort(expert_of)
    expert_sorted = expert_of[order]
    counts = jnp.zeros((N_EXPERTS,), jnp.int32).at[expert_of].add(1)
    padded = (counts + EXPERT_BLOCK - 1) // EXPERT_BLOCK * EXPERT_BLOCK
    pad_end = jnp.cumsum(padded).astype(jnp.int32)
    pad_start = pad_end - padded
    grp_start = jnp.cumsum(counts).astype(jnp.int32) - counts
    rank = jnp.arange(n_assign, dtype=jnp.int32) - grp_start[expert_sorted]
    dest = jnp.zeros((n_assign,), jnp.int32).at[order].set(pad_start[expert_sorted] + rank)
    assign_tok = jnp.arange(n_assign, dtype=jnp.int32) // TOP_K
    row_tok = jnp.zeros((n_blocks * EXPERT_BLOCK,), jnp.int32).at[dest].set(assign_tok)
    block_start = jnp.arange(n_blocks, dtype=jnp.int32) * EXPERT_BLOCK
    block_expert = jnp.minimum(jnp.searchsorted(pad_end, block_start, side='right'),
                               N_EXPERTS - 1).astype(jnp.int32)
    x_rows = xt[row_tok].reshape(n_blocks, EXPERT_BLOCK, D)

    def expert_block(args):
        xb, e = args
        g = xb @ w_gate[e] + b_gate[e]
        up = xb @ w_up[e] + b_up[e]
        g = jnp.minimum(g, SWIGLU_LIMIT)
        up = jnp.clip(up, -SWIGLU_LIMIT, SWIGLU_LIMIT)
        hdn = (up + 1.0) * (g * jax.nn.sigmoid(SWIGLU_ALPHA * g))
        return hdn @ w_down[e] + b_down[e]

    y_rows = lax.map(expert_block, (x_rows, block_expert)).reshape(n_blocks * EXPERT_BLOCK, D)
    picked = y_rows[dest].reshape(n_tok, TOP_K, D).astype(jnp.float32)
    y = jnp.einsum('tkd,tk->td', picked, gates)
    return y.astype(x.dtype).reshape(B, T, D)


def setup_inputs(seed: int = 0) -> dict:
    key = jax.random.key(seed)
    ks = jax.random.split(key, 32)
    f32 = jnp.float32
    L = DEPTH

    def nrm(k, shape, scale):
        return jax.random.normal(k, shape, f32) * scale

    return {
        "x_prompt": nrm(ks[0], (BATCH, SEQ, D_MODEL), 1.0),
        "x_sample": nrm(ks[1], (DEC_BATCH, DEC_SEQ, D_MODEL), 1.0),
        "cache_k": nrm(ks[2], (L, DEC_BATCH, PAST_LEN, N_HEADS, V_DIM), 1.0),
        "cache_v": nrm(ks[3], (L, DEC_BATCH, PAST_LEN, N_HEADS, V_DIM), 1.0),
        "state_pool": nrm(ks[4], (L, DEC_BATCH, POOL_STATE, POOL_WIDTH), 1.0),
        "norm_mix_g": 1.0 + nrm(ks[5], (L, D_MODEL), 0.05),
        "w_in": nrm(ks[6], (L, D_MODEL, IN_COLS), D_MODEL ** -0.5),
        "q_norm_g": 1.0 + nrm(ks[7], (L, HEAD_DIM), 0.05),
        "k_norm_g": 1.0 + nrm(ks[8], (L, HEAD_DIM), 0.05),
        "lambda_q1": nrm(ks[9], (L, HEAD_DIM), 0.1),
        "lambda_k1": nrm(ks[10], (L, HEAD_DIM), 0.1),
        "lambda_q2": nrm(ks[11], (L, HEAD_DIM), 0.1),
        "lambda_k2": nrm(ks[12], (L, HEAD_DIM), 0.1),
        "subln_g": 1.0 + nrm(ks[13], (L, V_DIM), 0.05),
        "pool_w": nrm(ks[14], (L, POOL_GROUPS, POOL_GROUP_DIM, POOL_OUT_DIM), POOL_GROUP_DIM ** -0.5),
        "pool_scale": 1.0 + nrm(ks[15], (L, D_MODEL), 0.1),
        "w_out": nrm(ks[16], (L, D_MODEL, D_MODEL), D_MODEL ** -0.5),
        "norm_ffn_g": 1.0 + nrm(ks[17], (L, D_MODEL), 0.05),
        "router_w": nrm(ks[18], (L, D_MODEL, N_EXPERTS), D_MODEL ** -0.5),
        "router_b": nrm(ks[19], (L, N_EXPERTS), 0.01),
        "w_gate": nrm(ks[20], (L, N_EXPERTS, D_MODEL, D_FF), D_MODEL ** -0.5),
        "b_gate": nrm(ks[21], (L, N_EXPERTS, D_FF), 0.01),
        "w_up": nrm(ks[22], (L, N_EXPERTS, D_MODEL, D_FF), D_MODEL ** -0.5),
        "b_up": nrm(ks[23], (L, N_EXPERTS, D_FF), 0.01),
        "w_down": nrm(ks[24], (L, N_EXPERTS, D_FF, D_MODEL), D_FF ** -0.5),
        "b_down": nrm(ks[25], (L, N_EXPERTS, D_MODEL), 0.01),
    }


def reference(x_prompt, x_sample, cache_k, cache_v, state_pool, norm_mix_g, w_in, q_norm_g,
              k_norm_g, lambda_q1, lambda_k1, lambda_q2, lambda_k2, subln_g, pool_w, pool_scale,
              w_out, norm_ffn_g, router_w, router_b, w_gate, b_gate, w_up, b_up, w_down, b_down):
    f32 = jnp.float32
    hp, hs = x_prompt, x_sample
    bp = x_prompt.shape[0]
    kp_l, vp_l, up_l, ks_l, vs_l, us_l = [], [], [], [], [], []
    for layer in range(DEPTH):
        lam_init = 0.8 - 0.6 * math.exp(-0.3 * layer)
        lam = (jnp.exp(jnp.sum(lambda_q1[layer].astype(f32) * lambda_k1[layer].astype(f32)))
               - jnp.exp(jnp.sum(lambda_q2[layer].astype(f32) * lambda_k2[layer].astype(f32)))
               + lam_init)
        mix = functools.partial(
            _token_mixer, lam=lam, lam_init=lam_init, norm_g=norm_mix_g[layer], w_in=w_in[layer],
            q_norm_g=q_norm_g[layer], k_norm_g=k_norm_g[layer], subln_g=subln_g[layer],
            pool_w=pool_w[layer], pool_scale=pool_scale[layer], w_out=w_out[layer])
        ffn = functools.partial(
            _moe_ffn, router_w=router_w[layer], router_b=router_b[layer], w_gate=w_gate[layer],
            b_gate=b_gate[layer], w_up=w_up[layer], b_up=b_up[layer], w_down=w_down[layer],
            b_down=b_down[layer])
        empty_kv = jnp.zeros((bp, 0, N_HEADS, V_DIM), hp.dtype)
        empty_pool = jnp.zeros((bp, POOL_STATE, POOL_WIDTH), hp.dtype)
        yp, kp, vp, upt = mix(hp, empty_kv, empty_kv, empty_pool)
        ys, ksm, vsm, ust = mix(hs, cache_k[layer], cache_v[layer], state_pool[layer])
        hp = hp + yp
        hs = hs + ys
        hp = hp + ffn(_rms_norm(hp, norm_ffn_g[layer]))
        hs = hs + ffn(_rms_norm(hs, norm_ffn_g[layer]))
        kp_l.append(kp)
        vp_l.append(vp)
        up_l.append(upt)
        ks_l.append(ksm)
        vs_l.append(vsm)
        us_l.append(ust)
    return (hp, hs, jnp.stack(kp_l), jnp.stack(vp_l), jnp.stack(up_l),
            jnp.stack(ks_l), jnp.stack(vs_l), jnp.stack(us_l))
```

```python
import functools
import math

import jax
import jax.numpy as jnp
from jax import lax
from jax.experimental import pallas as pl
from jax.experimental.pallas import tpu as pltpu

F32 = jnp.float32
BF16 = jnp.bfloat16

CHUNK = 64
N_HEADS = 8
HEAD_DIM = 64
V_DIM = 2 * HEAD_DIM
POOL_WINDOWS = (2, 4, 8, 16)
POOL_GROUP_DIM = 128
POOL_OUT_DIM = 256
POOL_HALO = 16
N_EXPERTS = 32
TOP_K = 4
SWIGLU_LIMIT = 7.0
SWIGLU_ALPHA = 1.702
RMS_EPS = 1e-6
NEG_INF = -1e30

LANES = 128
NORM_BLOCK = 256
VMEM_LIMIT = 56 * 1024 * 1024

ROW_TILE = 512
ATTN_TILE = 256
MOE_TILE = 256
TOK_TILE = 256


def _sigmoid(x):
    return 1.0 / (1.0 + jnp.exp(-x))


def _cparams(sem):
    return pltpu.CompilerParams(dimension_semantics=sem, vmem_limit_bytes=VMEM_LIMIT)


def _inproj_body(x_ref, pre_ref, ng_ref, w_ref, qg_ref, kg_ref, bd_ref, pw_ref, ps_ref,
                 q_ref, k_ref, kb_ref, v_ref, vb_ref, sa_ref, gp_ref, ut_ref, ext_ref,
                 *, nseg, seg_len, start_pos, carry, attn_w, pool_w):
    rows = nseg * seg_len
    t = pl.program_id(1)
    x = x_ref[0]
    ms = jnp.mean(x * x, axis=-1, keepdims=True)
    xn = (x * lax.rsqrt(ms + RMS_EPS) * ng_ref[...]).astype(BF16)

    def proj(c0, width):
        return jnp.dot(xn, w_ref[:, c0:c0 + width], preferred_element_type=F32)

    bd = bd_ref[...]

    def group_norm(p, g_ref):
        ss = jnp.dot((p * p).astype(BF16), bd, preferred_element_type=F32)
        return p * lax.rsqrt(ss * (1.0 / HEAD_DIM) + RMS_EPS) * g_ref[...]

    nb = NORM_BLOCK
    for c in range(attn_w // nb):
        cs = slice(c * nb, (c + 1) * nb)
        q_ref[0, :, cs] = group_norm(proj(c * nb, nb), qg_ref).astype(BF16)
    for c in range(attn_w // nb):
        cs = slice(c * nb, (c + 1) * nb)
        kn = group_norm(proj(attn_w + c * nb, nb), kg_ref)
        k_ref[0, :, cs] = kn
        kb_ref[0, :, cs] = kn.astype(BF16)
    for c in range(attn_w // nb):
        cs = slice(c * nb, (c + 1) * nb)
        vv = proj(2 * attn_w + c * nb, nb)
        v_ref[0, :, cs] = vv
        vb_ref[0, :, cs] = vv.astype(BF16)

    if carry:
        @pl.when(t == 0)
        def _():
            ext_ref[:, 0:POOL_HALO, :] = pre_ref[...]
    else:
        ext_ref[:, 0:POOL_HALO, :] = pre_ref[...]
    for c in range(pool_w // nb):
        cs = slice(c * nb, (c + 1) * nb)
        u = proj(3 * attn_w + c * nb, nb)
        ext_ref[:, POOL_HALO:POOL_HALO + seg_len, cs] = u.reshape(nseg, seg_len, nb)

    ga0 = 3 * attn_w + pool_w
    d_model = attn_w
    for c in range(d_model // nb):
        cs = slice(c * nb, (c + 1) * nb)
        sa_ref[0, :, cs] = _sigmoid(proj(ga0 + c * nb, nb)).astype(BF16)

    gb0 = ga0 + d_model
    row = lax.broadcasted_iota(jnp.int32, (1, seg_len, 1), 1)
    pos = row + start_pos
    if carry:
        pos = pos + t * seg_len
    for g, w in enumerate(POOL_WINDOWS):
        cs = slice(g * POOL_GROUP_DIM, (g + 1) * POOL_GROUP_DIM)
        own = ext_ref[:, POOL_HALO:POOL_HALO + seg_len, cs]
        acc = own
        for i in range(1, w):
            acc = acc + ext_ref[:, POOL_HALO - i:POOL_HALO - i + seg_len, cs]
        inv = 1.0 / jnp.minimum(w, pos + 1).astype(F32)
        z = (acc * inv - own).reshape(rows, POOL_GROUP_DIM)
        os_ = slice(g * POOL_OUT_DIM, (g + 1) * POOL_OUT_DIM)
        yp = jnp.dot(z.astype(BF16), pw_ref[g], preferred_element_type=F32) * ps_ref[:, os_]
        gb = proj(gb0 + g * POOL_OUT_DIM, POOL_OUT_DIM)
        gp_ref[0, :, os_] = (_sigmoid(gb) * yp).astype(BF16)

    tail = ext_ref[:, seg_len:seg_len + POOL_HALO, :]
    ut_ref[...] = tail
    if carry:
        ext_ref[:, 0:POOL_HALO, :] = tail


def _inproj(x3, prefix, ng, w_in, qg, kg, bd, pw, ps, *, nseg, seg_len, start_pos, carry):
    groups, t_len, d_model = x3.shape
    rows = nseg * seg_len
    steps = t_len // rows
    in_cols = w_in.shape[1]
    pool_w = prefix.shape[-1]
    attn_w = d_model
    assert in_cols == 3 * attn_w + pool_w + 2 * d_model
    tok = lambda b, t: (b, t, 0)
    fixed2 = lambda b, t: (0, 0)
    act = lambda dt: jax.ShapeDtypeStruct((groups, t_len, d_model), dt)
    body = functools.partial(_inproj_body, nseg=nseg, seg_len=seg_len, start_pos=start_pos,
                             carry=carry, attn_w=attn_w, pool_w=pool_w)
    return pl.pallas_call(
        body,
        grid=(groups, steps),
        in_specs=[
            pl.BlockSpec((1, rows, d_model), tok),
            pl.BlockSpec((nseg, POOL_HALO, pool_w), lambda b, t: (b, 0, 0)),
            pl.BlockSpec((1, d_model), fixed2),
            pl.BlockSpec((d_model, in_cols), fixed2, pipeline_mode=pl.Buffered(1)),
            pl.BlockSpec((1, NORM_BLOCK), fixed2),
            pl.BlockSpec((1, NORM_BLOCK), fixed2),
            pl.BlockSpec((NORM_BLOCK, NORM_BLOCK), fixed2),
            pl.BlockSpec(pw.shape, lambda b, t: (0, 0, 0)),
            pl.BlockSpec((1, d_model), fixed2),
        ],
        out_specs=[pl.BlockSpec((1, rows, d_model), tok)] * 7
        + [pl.BlockSpec((nseg, POOL_HALO, pool_w), lambda b, t: (b, 0, 0))],
        out_shape=[act(BF16), act(F32), act(BF16), act(F32), act(BF16), act(BF16), act(BF16),
                   jax.ShapeDtypeStruct(prefix.shape, F32)],
        scratch_shapes=[pltpu.VMEM((nseg, POOL_HALO + seg_len, pool_w), F32)],
        compiler_params=_cparams(("arbitrary", "arbitrary")),
        name="inproj",
    )(x3, prefix, ng, w_in, qg, kg, bd, pw, ps)


def _half_masks(q):
    lane = lax.broadcasted_iota(jnp.int32, q.shape, 1)
    zero = jnp.zeros_like(q)
    return jnp.where(lane < HEAD_DIM, q, zero), jnp.where(lane >= HEAD_DIM, q, zero)


def _qk(qz, kblk):
    return lax.dot_general(qz, kblk, (((1,), (1,)), ((), ())), preferred_element_type=F32)


def _subln(o, sg_ref):
    ms = jnp.mean(o * o, axis=-1, keepdims=True)
    return o * lax.rsqrt(ms + RMS_EPS) * sg_ref[...]


def _attn_body(slope_ref, lam_ref, q_ref, k_ref, v_ref, sg_ref, o_ref, s_ref, *, tq):
    h = pl.program_id(1)
    qi = pl.program_id(2)
    slope = slope_ref[h]
    lam = lam_ref[0]
    q1z, q2z = _half_masks(q_ref[0])
    row = lax.broadcasted_iota(jnp.int32, (tq, tq), 0)
    col = lax.broadcasted_iota(jnp.int32, (tq, tq), 1)
    rc = row - col
    cd = lax.shift_right_logical(col, 6) - lax.shift_right_logical(row, 6)
    chunks_per_tile = tq // CHUNK
    nkb = qi + 1

    def pass_scores(kb, carry):
        m1, m2 = carry
        k0 = pl.multiple_of(kb * tq, tq)
        kblk = k_ref[0, pl.ds(k0, tq), :]
        off = qi - kb
        dist = jnp.abs(rc + off * tq).astype(F32)
        vis = cd <= off * chunks_per_tile
        bias = jnp.where(vis, -slope * dist, NEG_INF)
        s1 = _qk(q1z, kblk) + bias
        s2 = _qk(q2z, kblk) + bias
        s_ref[0, kb] = s1
        s_ref[1, kb] = s2
        return (jnp.maximum(m1, jnp.max(s1, axis=-1, keepdims=True)),
                jnp.maximum(m2, jnp.max(s2, axis=-1, keepdims=True)))

    minit = jnp.full((tq, 1), -jnp.inf, F32)
    m1, m2 = lax.fori_loop(0, nkb, pass_scores, (minit, minit))

    def pass_exp(kb, carry):
        l1, l2 = carry
        p1 = jnp.exp(s_ref[0, kb] - m1)
        p2 = jnp.exp(s_ref[1, kb] - m2)
        s_ref[0, kb] = p1
        s_ref[1, kb] = p2
        return (l1 + jnp.sum(p1, axis=-1, keepdims=True),
                l2 + jnp.sum(p2, axis=-1, keepdims=True))

    linit = jnp.zeros((tq, 1), F32)
    l1, l2 = lax.fori_loop(0, nkb, pass_exp, (linit, linit))
    c1 = 1.0 / l1
    c2 = lam / l2

    def pass_pv(kb, acc):
        k0 = pl.multiple_of(kb * tq, tq)
        w = (s_ref[0, kb] * c1 - s_ref[1, kb] * c2).astype(BF16)
        return acc + jnp.dot(w, v_ref[0, pl.ds(k0, tq), :], preferred_element_type=F32)

    o = lax.fori_loop(0, nkb, pass_pv, jnp.zeros((tq, V_DIM), F32))
    o_ref[0] = _subln(o, sg_ref).astype(BF16)


def _attn_prompt(q, kb, vb, slopes, lam, sg):
    batch, t_len, width = q.shape
    tq = min(ATTN_TILE, t_len)
    nq = t_len // tq
    smem = pl.BlockSpec(memory_space=pltpu.SMEM)
    return pl.pallas_call(
        functools.partial(_attn_body, tq=tq),
        grid=(batch, N_HEADS, nq),
        in_specs=[
            smem, smem,
            pl.BlockSpec((1, tq, V_DIM), lambda b, h, i: (b, i, h)),
            pl.BlockSpec((1, t_len, V_DIM), lambda b, h, i: (b, 0, h)),
            pl.BlockSpec((1, t_len, V_DIM), lambda b, h, i: (b, 0, h)),
            pl.BlockSpec((1, V_DIM), lambda b, h, i: (0, 0)),
        ],
        out_specs=pl.BlockSpec((1, tq, V_DIM), lambda b, h, i: (b, i, h)),
        out_shape=jax.ShapeDtypeStruct((batch, t_len, width), BF16),
        scratch_shapes=[pltpu.VMEM((2, nq, tq, tq), F32)],
        compiler_params=_cparams(("arbitrary", "arbitrary", "arbitrary")),
        name="attn_prompt",
    )(slopes, lam, q, kb, vb, sg)


def _attn_dec_body(slope_ref, lam_ref, q_ref, kc_ref, vc_ref, kn_ref, vn_ref, sg_ref, o_ref, *, past):
    h = pl.program_id(1)
    slope = slope_ref[h]
    lam = lam_ref[0]
    q1z, q2z = _half_masks(q_ref[0])
    tq = q_ref.shape[1]
    kc = kc_ref[0].astype(BF16)
    kn = kn_ref[0]
    qpos_a = lax.broadcasted_iota(jnp.int32, (tq, past), 0) + past
    kpos_a = lax.broadcasted_iota(jnp.int32, (tq, past), 1)
    qpos_b = lax.broadcasted_iota(jnp.int32, (tq, tq), 0) + past
    kpos_b = lax.broadcasted_iota(jnp.int32, (tq, tq), 1) + past

    def bias_of(qpos, kpos):
        vis = lax.shift_right_logical(kpos, 6) <= lax.shift_right_logical(qpos, 6)
        return jnp.where(vis, -slope * jnp.abs(qpos - kpos).astype(F32), NEG_INF)

    bias_a = bias_of(qpos_a, kpos_a)
    bias_b = bias_of(qpos_b, kpos_b)

    def softmax_parts(qz):
        sa = _qk(qz, kc) + bias_a
        sb = _qk(qz, kn) + bias_b
        m = jnp.maximum(jnp.max(sa, axis=-1, keepdims=True), jnp.max(sb, axis=-1, keepdims=True))
        pa = jnp.exp(sa - m)
        pb = jnp.exp(sb - m)
        l = jnp.sum(pa, axis=-1, keepdims=True) + jnp.sum(pb, axis=-1, keepdims=True)
        return pa, pb, l

    pa1, pb1, l1 = softmax_parts(q1z)
    pa2, pb2, l2 = softmax_parts(q2z)
    c1 = 1.0 / l1
    c2 = lam / l2
    wa = (pa1 * c1 - pa2 * c2).astype(BF16)
    wb = (pb1 * c1 - pb2 * c2).astype(BF16)
    o = (jnp.dot(wa, vc_ref[0].astype(BF16), preferred_element_type=F32)
         + jnp.dot(wb, vn_ref[0], preferred_element_type=F32))
    o_ref[0] = _subln(o, sg_ref).astype(BF16)


def _attn_sample(q, kn, vn, cache_k, cache_v, slopes, lam, sg):
    batch, tq, width = q.shape
    past = cache_k.shape[1]
    smem = pl.BlockSpec(memory_space=pltpu.SMEM)
    new = pl.BlockSpec((1, tq, V_DIM), lambda b, h: (b, 0, h))
    old = pl.BlockSpec((1, past, V_DIM), lambda b, h: (b, 0, h))
    return pl.pallas_call(
        functools.partial(_attn_dec_body, past=past),
        grid=(batch, N_HEADS),
        in_specs=[smem, smem, new, old, old, new, new,
                  pl.BlockSpec((1, V_DIM), lambda b, h: (0, 0))],
        out_specs=new,
        out_shape=jax.ShapeDtypeStruct((batch, tq, width), BF16),
        compiler_params=_cparams(("arbitrary", "arbitrary")),
        name="attn_sample",
    )(slopes, lam, q, cache_k, cache_v, kn, vn, sg)


def _outproj_body(at_ref, sa_ref, gp_ref, x_ref, wo_ref, g_ref, rw_ref, rb_ref,
                  h_ref, ti_ref, tg_ref, cnt_ref):
    i = pl.program_id(0)
    merged = (sa_ref[...].astype(F32) * at_ref[...].astype(F32) + gp_ref[...].astype(F32)).astype(BF16)
    hh = x_ref[...] + jnp.dot(merged, wo_ref[...], preferred_element_type=F32)
    h_ref[...] = hh
    ms = jnp.mean(hh * hh, axis=-1, keepdims=True)
    hn = (hh * lax.rsqrt(ms + RMS_EPS) * g_ref[...]).astype(BF16)
    logits = jnp.dot(hn, rw_ref[...], preferred_element_type=F32) + rb_ref[...]
    lt = jnp.transpose(logits)[0:N_EXPERTS, :]
    tm = lt.shape[1]
    e_iota = lax.broadcasted_iota(jnp.int32, (N_EXPERTS, tm), 0)
    vals, idxs, hots = [], [], []
    cur = lt
    for _ in range(TOP_K):
        m = jnp.max(cur, axis=0, keepdims=True)
        idx = jnp.min(jnp.where(cur == m, e_iota, N_EXPERTS), axis=0, keepdims=True)
        hit = e_iota == idx
        vals.append(m)
        idxs.append(idx)
        hots.append(hit)
        cur = jnp.where(hit, -jnp.inf, cur)
    ex = [jnp.exp(v - vals[0]) for v in vals]
    den = ex[0] + ex[1] + ex[2] + ex[3]
    inv = 1.0 / den
    zi = jnp.zeros((8 - TOP_K, tm), jnp.int32)
    zf = jnp.zeros((8 - TOP_K, tm), F32)
    ti_ref[...] = jnp.concatenate(idxs + [zi], axis=0)
    tg_ref[...] = jnp.concatenate([e * inv for e in ex] + [zf], axis=0)
    hot = jnp.concatenate([hh_.astype(F32) for hh_ in hots], axis=0)
    csum = jnp.sum(hot, axis=1, keepdims=True)

    @pl.when(i == 0)
    def _():
        cnt_ref[...] = jnp.zeros_like(cnt_ref)

    cnt_ref[...] += jnp.broadcast_to(csum, cnt_ref.shape)


def _outproj(attn, sa, gp, x, w_out, g, rw, rb):
    n_tok, d_model = x.shape
    tm = min(ROW_TILE, n_tok)
    row = lambda i: (i, 0)
    fixed = lambda i: (0, 0)
    colb = lambda i: (0, i)
    return pl.pallas_call(
        _outproj_body,
        grid=(n_tok // tm,),
        in_specs=[
            pl.BlockSpec((tm, d_model), row),
            pl.BlockSpec((tm, d_model), row),
            pl.BlockSpec((tm, d_model), row),
            pl.BlockSpec((tm, d_model), row),
            pl.BlockSpec((d_model, d_model), fixed),
            pl.BlockSpec((1, d_model), fixed),
            pl.BlockSpec((d_model, LANES), fixed),
            pl.BlockSpec((1, LANES), fixed),
        ],
        out_specs=[
            pl.BlockSpec((tm, d_model), row),
            pl.BlockSpec((8, tm), colb),
            pl.BlockSpec((8, tm), colb),
            pl.BlockSpec((TOP_K * N_EXPERTS, LANES), fixed),
        ],
        out_shape=[
            jax.ShapeDtypeStruct((n_tok, d_model), F32),
            jax.ShapeDtypeStruct((8, n_tok), jnp.int32),
            jax.ShapeDtypeStruct((8, n_tok), F32),
            jax.ShapeDtypeStruct((TOP_K * N_EXPERTS, LANES), F32),
        ],
        compiler_params=_cparams(("arbitrary",)),
        name="outproj",
    )(attn, sa, gp, x, w_out, g, rw, rb)


def _rank_body(ti_ref, base_ref, tri_ref, dest_ref, carry_ref):
    i = pl.program_id(0)

    @pl.when(i == 0)
    def _():
        carry_ref[...] = jnp.zeros_like(carry_ref)

    tt = ti_ref.shape[1]
    e_iota = lax.broadcasted_iota(jnp.int32, (N_EXPERTS, tt), 0)
    hot = jnp.concatenate([(ti_ref[k:k + 1, :] == e_iota).astype(F32) for k in range(TOP_K)], axis=0)
    incl = jnp.dot(hot.astype(BF16), tri_ref[...], preferred_element_type=F32)
    slot = base_ref[:, 0:1] + carry_ref[:, 0:1] + incl - 1.0
    picked = hot * slot
    rows = [jnp.sum(picked[k * N_EXPERTS:(k + 1) * N_EXPERTS, :], axis=0, keepdims=True)
            for k in range(TOP_K)]
    rows.append(jnp.zeros((8 - TOP_K, tt), F32))
    dest_ref[...] = jnp.concatenate(rows, axis=0).astype(jnp.int32)
    carry_ref[...] += jnp.broadcast_to(jnp.sum(hot, axis=1, keepdims=True), carry_ref.shape)


def _rank(topi, base, tri):
    n_tok = topi.shape[1]
    tt = tri.shape[0]
    return pl.pallas_call(
        _rank_body,
        grid=(n_tok // tt,),
        in_specs=[
            pl.BlockSpec((8, tt), lambda i: (0, i)),
            pl.BlockSpec(base.shape, lambda i: (0, 0)),
            pl.BlockSpec((tt, tt), lambda i: (0, 0)),
        ],
        out_specs=pl.BlockSpec((8, tt), lambda i: (0, i)),
        out_shape=jax.ShapeDtypeStruct((8, n_tok), jnp.int32),
        scratch_shapes=[pltpu.VMEM((TOP_K * N_EXPERTS, LANES), F32)],
        compiler_params=_cparams(("arbitrary",)),
        name="rank",
    )(topi, base, tri)


def _row_copy(src_hbm, dst_buf, sem, src_row, dst_row):
    return pltpu.make_async_copy(src_hbm.at[pl.ds(src_row, 1)], dst_buf.at[pl.ds(dst_row, 1)], sem)


def _moe_body(te_ref, nu_ref, rt_hbm, h_hbm, g_ref, wg_ref, bg_ref, wu_ref, bu_ref, wd_ref, bd_ref,
              y_ref, idx_ref, xbuf, wgb, wub, wdb, isem, gsem, *, tm):
    i = pl.program_id(0)
    n_used = nu_ref[0]
    slot = i % 2
    other = 1 - slot

    def idx_copy(tile, s):
        return pltpu.make_async_copy(rt_hbm.at[tile], idx_ref.at[s], isem.at[s])

    def issue_gather(s):
        def body(r, c):
            _row_copy(h_hbm, xbuf.at[s], gsem.at[s], idx_ref[s, r], r).start()
            return c
        lax.fori_loop(0, tm, body, 0, unroll=8)

    def wait_gather(s):
        pltpu.make_async_copy(h_hbm.at[pl.ds(0, tm)], xbuf.at[s], gsem.at[s]).wait()

    @pl.when(i == 0)
    def _():
        idx_copy(0, 0).start()
        idx_copy(0, 0).wait()
        issue_gather(0)

        @pl.when(n_used > 1)
        def _():
            idx_copy(1, 1).start()

    @pl.when(i >= n_used)
    def _():
        y_ref[...] = jnp.zeros_like(y_ref)

    @pl.when(i < n_used)
    def _():
        @pl.when(i + 1 < n_used)
        def _():
            idx_copy(i + 1, other).wait()
            issue_gather(other)

        @pl.when(i + 2 < n_used)
        def _():
            idx_copy(i + 2, slot).start()

        wait_gather(slot)

        prev = te_ref[jnp.maximum(i - 1, 0)]
        changed = jnp.logical_or(i == 0, te_ref[i] != prev)

        @pl.when(changed)
        def _():
            wgb[...] = wg_ref[0].astype(BF16)
            wub[...] = wu_ref[0].astype(BF16)
            wdb[...] = wd_ref[0].astype(BF16)

        x = xbuf[slot]
        ms = jnp.mean(x * x, axis=-1, keepdims=True)
        xb = (x * lax.rsqrt(ms + RMS_EPS) * g_ref[...]).astype(BF16)
        gt = jnp.dot(xb, wgb[...], preferred_element_type=F32) + bg_ref[0]
        up = jnp.dot(xb, wub[...], preferred_element_type=F32) + bu_ref[0]
        gt = jnp.minimum(gt, SWIGLU_LIMIT)
        up = jnp.clip(up, -SWIGLU_LIMIT, SWIGLU_LIMIT)
        hdn = (up + 1.0) * (gt * _sigmoid(SWIGLU_ALPHA * gt))
        y_ref[...] = jnp.dot(hdn.astype(BF16), wdb[...], preferred_element_type=F32) + bd_ref[0]


def _moe(tile_expert, n_used, row_tok, h_all, g, w_gate, b_gate, w_up, b_up, w_down, b_down):
    n_tiles, tm = row_tok.shape
    d_model = h_all.shape[1]
    d_ff = w_gate.shape[2]
    wspec = lambda shape: pl.BlockSpec((1,) + shape, lambda i, te, nu: (te[i], 0, 0))
    any_spec = pl.BlockSpec(memory_space=pl.ANY)
    grid_spec = pltpu.PrefetchScalarGridSpec(
        num_scalar_prefetch=2,
        grid=(n_tiles,),
        in_specs=[
            any_spec, any_spec,
            pl.BlockSpec((1, d_model), lambda i, te, nu: (0, 0)),
            wspec((d_model, d_ff)), wspec((1, d_ff)),
            wspec((d_model, d_ff)), wspec((1, d_ff)),
            wspec((d_ff, d_model)), wspec((1, d_model)),
        ],
        out_specs=pl.BlockSpec((tm, d_model), lambda i, te, nu: (i, 0)),
        scratch_shapes=[
            pltpu.SMEM((2, tm), jnp.int32),
            pltpu.VMEM((2, tm, d_model), F32),
            pltpu.VMEM((d_model, d_ff), BF16),
            pltpu.VMEM((d_model, d_ff), BF16),
            pltpu.VMEM((d_ff, d_model), BF16),
            pltpu.SemaphoreType.DMA((2,)),
            pltpu.SemaphoreType.DMA((2,)),
        ],
    )
    return pl.pallas_call(
        functools.partial(_moe_body, tm=tm),
        grid_spec=grid_spec,
        out_shape=jax.ShapeDtypeStruct((n_tiles * tm, d_model), F32),
        compiler_params=_cparams(("arbitrary",)),
        name="moe",
    )(tile_expert, n_used, row_tok, h_all, g,
      w_gate, b_gate.reshape(N_EXPERTS, 1, d_ff), w_up, b_up.reshape(N_EXPERTS, 1, d_ff),
      w_down, b_down.reshape(N_EXPERTS, 1, d_model))


def _combine_body(dest_hbm, y_hbm, h_ref, gt_ref, o_ref, idx_ref, ybuf, isem, gsem, *, tt):
    i = pl.program_id(0)
    n = pl.num_programs(0)
    slot = i % 2
    other = 1 - slot

    def idx_copy(tile, s):
        return pltpu.make_async_copy(dest_hbm.at[tile], idx_ref.at[s], isem.at[s])

    def issue_gather(s):
        for k in range(TOP_K):
            def body(r, c, k=k):
                _row_copy(y_hbm, ybuf.at[s, k], gsem.at[s], idx_ref[s, k * tt + r], r).start()
                return c
            lax.fori_loop(0, tt, body, 0, unroll=8)

    def wait_gather(s):
        for k in range(TOP_K):
            pltpu.make_async_copy(y_hbm.at[pl.ds(0, tt)], ybuf.at[s, k], gsem.at[s]).wait()

    @pl.when(i == 0)
    def _():
        idx_copy(0, 0).start()
        idx_copy(0, 0).wait()
        issue_gather(0)

        @pl.when(n > 1)
        def _():
            idx_copy(1, 1).start()

    @pl.when(i + 1 < n)
    def _():
        idx_copy(i + 1, other).wait()
        issue_gather(other)

    @pl.when(i + 2 < n)
    def _():
        idx_copy(i + 2, slot).start()

    wait_gather(slot)
    acc = h_ref[...]
    gts = gt_ref[...]
    for k in range(TOP_K):
        acc = acc + gts[:, k:k + 1] * ybuf[slot, k]
    o_ref[...] = acc


def _combine(dest_tiles, y_rows, h, gates_tok):
    n_tiles, width = dest_tiles.shape
    tt = width // TOP_K
    n_tok, d_model = h.shape
    any_spec = pl.BlockSpec(memory_space=pl.ANY)
    return pl.pallas_call(
        functools.partial(_combine_body, tt=tt),
        grid=(n_tiles,),
        in_specs=[any_spec, any_spec,
                  pl.BlockSpec((tt, d_model), lambda i: (i, 0)),
                  pl.BlockSpec((tt, 8), lambda i: (i, 0))],
        out_specs=pl.BlockSpec((tt, d_model), lambda i: (i, 0)),
        out_shape=jax.ShapeDtypeStruct((n_tok, d_model), F32),
        scratch_shapes=[
            pltpu.SMEM((2, TOP_K * tt), jnp.int32),
            pltpu.VMEM((2, TOP_K, tt, d_model), F32),
            pltpu.SemaphoreType.DMA((2,)),
            pltpu.SemaphoreType.DMA((2,)),
        ],
        compiler_params=_cparams(("arbitrary",)),
        name="combine",
    )(dest_tiles, y_rows, h, gates_tok)


def _block_diag_ones():
    r = lax.broadcasted_iota(jnp.int32, (NORM_BLOCK, NORM_BLOCK), 0) // HEAD_DIM
    c = lax.broadcasted_iota(jnp.int32, (NORM_BLOCK, NORM_BLOCK), 1) // HEAD_DIM
    return (r == c).astype(BF16)


def _dest_tiles(dest, tt):
    n_tok = dest.shape[1]
    return dest[:TOP_K].reshape(TOP_K, n_tok // tt, tt).transpose(1, 0, 2).reshape(n_tok // tt, TOP_K * tt)


def _layer(xp, xs, ck, cv, sp, lam, lam_init, ng, w_in, qng, kng, slg, pw, ps, w_out, fg,
           rw, rb, w_gate, b_gate, w_up, b_up, w_down, b_down):
    bp, t_len, d_model = xp.shape
    bs, ts, _ = xs.shape
    past = ck.shape[1]
    pool_w = sp.shape[-1]
    n_p, n_s = bp * t_len, bs * ts

    w_in_b = w_in.astype(BF16)
    w_out_b = w_out.astype(BF16)
    pw_b = pw.astype(BF16)
    reps = NORM_BLOCK // HEAD_DIM
    qg = (jnp.tile(qng.astype(F32), reps) * (HEAD_DIM ** -0.5)).reshape(1, NORM_BLOCK)
    kg = jnp.tile(kng.astype(F32), reps).reshape(1, NORM_BLOCK)
    bd = _block_diag_ones()
    ng2 = ng.astype(F32).reshape(1, d_model)
    ps2 = ps.astype(F32).reshape(1, d_model)
    sg = (slg.astype(F32) * (1.0 - lam_init)).reshape(1, V_DIM)
    slopes = jnp.exp2(-(8.0 / N_HEADS) * jnp.arange(1, N_HEADS + 1, dtype=F32))
    lam1 = lam.reshape(1).astype(F32)
    fg2 = fg.astype(F32).reshape(1, d_model)
    rw_b = jnp.zeros((d_model, LANES), BF16).at[:, :N_EXPERTS].set(rw.astype(BF16))
    rb2 = jnp.zeros((1, LANES), F32).at[0, :N_EXPERTS].set(rb.astype(F32))

    tm = min(ROW_TILE, t_len)
    zero_pre = jnp.zeros((bp, POOL_HALO, pool_w), F32)
    qp, kp, kpb, vp, vpb, sap, gpp, utp = _inproj(
        xp, zero_pre, ng2, w_in_b, qg, kg, bd, pw_b, ps2, nseg=1, seg_len=tm, start_pos=0, carry=True)
    atp = _attn_prompt(qp, kpb, vpb, slopes, lam1, sg)
    hp, tip, tgp, cntp = _outproj(atp.reshape(n_p, d_model), sap.reshape(n_p, d_model),
                                  gpp.reshape(n_p, d_model), xp.reshape(n_p, d_model),
                                  w_out_b, fg2, rw_b, rb2)

    pre_s = jnp.concatenate([jnp.zeros((bs, POOL_HALO - sp.shape[1], pool_w), F32), sp.astype(F32)], axis=1)
    qs, ks, ksb, vs, vsb, sas, gps, uts = _inproj(
        xs.reshape(1, n_s, d_model), pre_s, ng2, w_in_b, qg, kg, bd, pw_b, ps2,
        nseg=bs, seg_len=ts, start_pos=past, carry=False)
    ats = _attn_sample(qs.reshape(bs, ts, d_model), ksb.reshape(bs, ts, d_model), vsb.reshape(bs, ts, d_model),
                       ck.reshape(bs, past, d_model), cv.reshape(bs, past, d_model), slopes, lam1, sg)
    hs, tis, tgs, cnts = _outproj(ats.reshape(n_s, d_model), sas.reshape(n_s, d_model),
                                  gps.reshape(n_s, d_model), xs.reshape(n_s, d_model),
                                  w_out_b, fg2, rw_b, rb2)

    n_tok = n_p + n_s
    h_all = jnp.concatenate([hp, hs], axis=0)
    topi = jnp.concatenate([tip, tis], axis=1)
    gates = jnp.concatenate([tgp, tgs], axis=1)
    cnt = (cntp[:, 0] + cnts[:, 0]).astype(jnp.int32).reshape(TOP_K, N_EXPERTS)
    per_expert = jnp.sum(cnt, axis=0)
    padded = (per_expert + MOE_TILE - 1) // MOE_TILE * MOE_TILE
    pad_end = jnp.cumsum(padded)
    pad_start = pad_end - padded
    base = pad_start[None, :] + jnp.cumsum(cnt, axis=0) - cnt
    base_f = jnp.broadcast_to(base.reshape(-1, 1).astype(F32), (TOP_K * N_EXPERTS, LANES))
    tt = TOK_TILE
    tri = (lax.broadcasted_iota(jnp.int32, (tt, tt), 0) <= lax.broadcasted_iota(jnp.int32, (tt, tt), 1)).astype(BF16)
    dest = _rank(topi, base_f, tri)

    n_tiles = (n_tok * TOP_K) // MOE_TILE + N_EXPERTS
    tok_ids = jnp.tile(jnp.arange(n_tok, dtype=jnp.int32), TOP_K)
    row_tok = jnp.zeros((n_tiles * MOE_TILE,), jnp.int32).at[dest[:TOP_K].reshape(-1)].set(tok_ids)
    n_used = (pad_end[-1] // MOE_TILE).astype(jnp.int32)
    tile_start = jnp.arange(n_tiles, dtype=jnp.int32) * MOE_TILE
    last_start = jnp.maximum(pad_end[-1] - MOE_TILE, 0)
    tile_expert = jnp.minimum(jnp.searchsorted(pad_end, jnp.minimum(tile_start, last_start), side='right'),
                              N_EXPERTS - 1).astype(jnp.int32)
    y_rows = _moe(tile_expert, n_used.reshape(1), row_tok.reshape(n_tiles, MOE_TILE), h_all, fg2,
                  w_gate, b_gate, w_up, b_up, w_down, b_down)

    gates_tok = jnp.transpose(gates)
    yp = _combine(_dest_tiles(dest[:, :n_p], tt), y_rows, hp, gates_tok[:n_p])
    ys = _combine(_dest_tiles(dest[:, n_p:], tt), y_rows, hs, gates_tok[n_p:])

    heads = (N_HEADS, V_DIM)
    return (yp.reshape(bp, t_len, d_model), ys.reshape(bs, ts, d_model),
            kp.reshape(bp, t_len, *heads), vp.reshape(bp, t_len, *heads), utp[:, 1:],
            ks.reshape(bs, ts, *heads), vs.reshape(bs, ts, *heads), uts[:, 1:])


def kernel(x_prompt, x_sample, cache_k, cache_v, state_pool, norm_mix_g, w_in, q_norm_g, k_norm_g,
           lambda_q1, lambda_k1, lambda_q2, lambda_k2, subln_g, pool_w, pool_scale, w_out, norm_ffn_g,
           router_w, router_b, w_gate, b_gate, w_up, b_up, w_down, b_down):
    depth = w_in.shape[0]
    hp, hs = x_prompt, x_sample
    outs = [[] for _ in range(6)]
    for layer in range(depth):
        lam_init = 0.8 - 0.6 * math.exp(-0.3 * layer)
        lam = (jnp.exp(jnp.sum(lambda_q1[layer].astype(F32) * lambda_k1[layer].astype(F32)))
               - jnp.exp(jnp.sum(lambda_q2[layer].astype(F32) * lambda_k2[layer].astype(F32)))
               + lam_init)
        hp, hs, kp, vp, up, ks, vs, us = _layer(
            hp, hs, cache_k[layer], cache_v[layer], state_pool[layer], lam, lam_init,
            norm_mix_g[layer], w_in[layer], q_norm_g[layer], k_norm_g[layer], subln_g[layer],
            pool_w[layer], pool_scale[layer], w_out[layer], norm_ffn_g[layer],
            router_w[layer], router_b[layer], w_gate[layer], b_gate[layer], w_up[layer], b_up[layer],
            w_down[layer], b_down[layer])
        for lst, val in zip(outs, (kp, vp, up, ks, vs, us)):
            lst.append(val)
    return (hp, hs) + tuple(jnp.stack(o) for o in outs)
```

```python
import functools
import math

import jax
import jax.numpy as jnp
from jax import lax
from jax.experimental import pallas as pl
from jax.experimental.pallas import tpu as pltpu

F32 = jnp.float32
BF16 = jnp.bfloat16

CHUNK = 64
N_HEADS = 8
HEAD_DIM = 64
V_DIM = 2 * HEAD_DIM
POOL_WINDOWS = (2, 4, 8, 16)
POOL_GROUP_DIM = 128
POOL_OUT_DIM = 256
POOL_HALO = 16
N_EXPERTS = 32
TOP_K = 4
SWIGLU_LIMIT = 7.0
SWIGLU_ALPHA = 1.702
RMS_EPS = 1e-6
NEG_INF = -1e30

LANES = 128
NORM_BLOCK = 256
VMEM_LIMIT = 56 * 1024 * 1024

ROW_TILE = 512
ATTN_TILE = 256
MOE_TILE = 256
TOK_TILE = 256


def _sigmoid(x):
    return 1.0 / (1.0 + jnp.exp(-x))


def _cparams(sem):
    return pltpu.CompilerParams(dimension_semantics=sem, vmem_limit_bytes=VMEM_LIMIT)


def _inproj_body(x_ref, pre_ref, ng_ref, w_ref, qg_ref, kg_ref, bd_ref, pw_ref, ps_ref,
                 q_ref, k_ref, kb_ref, v_ref, vb_ref, sa_ref, gp_ref, ut_ref, ext_ref,
                 *, nseg, seg_len, start_pos, carry, attn_w, pool_w):
    rows = nseg * seg_len
    t = pl.program_id(1)
    x = x_ref[0]
    ms = jnp.mean(x * x, axis=-1, keepdims=True)
    xn = (x * lax.rsqrt(ms + RMS_EPS) * ng_ref[...]).astype(BF16)

    def proj(c0, width):
        return jnp.dot(xn, w_ref[:, c0:c0 + width], preferred_element_type=F32)

    bd = bd_ref[...]

    def group_norm(p, g_ref):
        ss = jnp.dot((p * p).astype(BF16), bd, preferred_element_type=F32)
        return p * lax.rsqrt(ss * (1.0 / HEAD_DIM) + RMS_EPS) * g_ref[...]

    nb = NORM_BLOCK
    heads_per_block = nb // V_DIM

    def store_heads(ref, val, c):
        for j in range(heads_per_block):
            head = c * heads_per_block + j
            ref[0, pl.ds(head, rows, stride=N_HEADS), :] = val[:, j * V_DIM:(j + 1) * V_DIM]

    for c in range(attn_w // nb):
        cs = slice(c * nb, (c + 1) * nb)
        q_ref[0, :, cs] = group_norm(proj(c * nb, nb), qg_ref).astype(BF16)
    for c in range(attn_w // nb):
        cs = slice(c * nb, (c + 1) * nb)
        kn = group_norm(proj(attn_w + c * nb, nb), kg_ref)
        store_heads(k_ref, kn, c)
        kb_ref[0, :, cs] = kn.astype(BF16)
    for c in range(attn_w // nb):
        cs = slice(c * nb, (c + 1) * nb)
        vv = proj(2 * attn_w + c * nb, nb)
        store_heads(v_ref, vv, c)
        vb_ref[0, :, cs] = vv.astype(BF16)

    if carry:
        @pl.when(t == 0)
        def _():
            ext_ref[:, 0:POOL_HALO, :] = pre_ref[...]
    else:
        ext_ref[:, 0:POOL_HALO, :] = pre_ref[...]
    for c in range(pool_w // nb):
        cs = slice(c * nb, (c + 1) * nb)
        u = proj(3 * attn_w + c * nb, nb)
        ext_ref[:, POOL_HALO:POOL_HALO + seg_len, cs] = u.reshape(nseg, seg_len, nb)

    ga0 = 3 * attn_w + pool_w
    d_model = attn_w
    for c in range(d_model // nb):
        cs = slice(c * nb, (c + 1) * nb)
        sa_ref[0, :, cs] = _sigmoid(proj(ga0 + c * nb, nb)).astype(BF16)

    gb0 = ga0 + d_model
    row = lax.broadcasted_iota(jnp.int32, (1, seg_len, 1), 1)
    pos = row + start_pos
    if carry:
        pos = pos + t * seg_len
    for g, w in enumerate(POOL_WINDOWS):
        cs = slice(g * POOL_GROUP_DIM, (g + 1) * POOL_GROUP_DIM)
        own = ext_ref[:, POOL_HALO:POOL_HALO + seg_len, cs]
        acc = own
        for i in range(1, w):
            acc = acc + ext_ref[:, POOL_HALO - i:POOL_HALO - i + seg_len, cs]
        inv = 1.0 / jnp.minimum(w, pos + 1).astype(F32)
        z = (acc * inv - own).reshape(rows, POOL_GROUP_DIM)
        os_ = slice(g * POOL_OUT_DIM, (g + 1) * POOL_OUT_DIM)
        yp = jnp.dot(z.astype(BF16), pw_ref[g], preferred_element_type=F32) * ps_ref[:, os_]
        gb = proj(gb0 + g * POOL_OUT_DIM, POOL_OUT_DIM)
        gp_ref[0, :, os_] = (_sigmoid(gb) * yp).astype(BF16)

    tail = ext_ref[:, seg_len:seg_len + POOL_HALO, :]
    ut_ref[...] = tail
    if carry:
        ext_ref[:, 0:POOL_HALO, :] = tail


def _inproj(x3, prefix, ng, w_in, qg, kg, bd, pw, ps, *, nseg, seg_len, start_pos, carry):
    groups, t_len, d_model = x3.shape
    rows = nseg * seg_len
    steps = t_len // rows
    in_cols = w_in.shape[1]
    pool_w = prefix.shape[-1]
    attn_w = d_model
    assert in_cols == 3 * attn_w + pool_w + 2 * d_model
    tok = lambda b, t: (b, t, 0)
    fixed2 = lambda b, t: (0, 0)
    act = lambda dt: jax.ShapeDtypeStruct((groups, t_len, d_model), dt)
    by_head = jax.ShapeDtypeStruct((groups, t_len * N_HEADS, V_DIM), F32)
    tok_spec = pl.BlockSpec((1, rows, d_model), tok)
    head_spec = pl.BlockSpec((1, rows * N_HEADS, V_DIM), tok)
    body = functools.partial(_inproj_body, nseg=nseg, seg_len=seg_len, start_pos=start_pos,
                             carry=carry, attn_w=attn_w, pool_w=pool_w)
    return pl.pallas_call(
        body,
        grid=(groups, steps),
        in_specs=[
            pl.BlockSpec((1, rows, d_model), tok),
            pl.BlockSpec((nseg, POOL_HALO, pool_w), lambda b, t: (b, 0, 0)),
            pl.BlockSpec((1, d_model), fixed2),
            pl.BlockSpec((d_model, in_cols), fixed2, pipeline_mode=pl.Buffered(1)),
            pl.BlockSpec((1, NORM_BLOCK), fixed2),
            pl.BlockSpec((1, NORM_BLOCK), fixed2),
            pl.BlockSpec((NORM_BLOCK, NORM_BLOCK), fixed2),
            pl.BlockSpec(pw.shape, lambda b, t: (0, 0, 0)),
            pl.BlockSpec((1, d_model), fixed2),
        ],
        out_specs=[tok_spec, head_spec, tok_spec, head_spec, tok_spec, tok_spec, tok_spec,
                   pl.BlockSpec((nseg, POOL_HALO, pool_w), lambda b, t: (b, 0, 0))],
        out_shape=[act(BF16), by_head, act(BF16), by_head, act(BF16), act(BF16), act(BF16),
                   jax.ShapeDtypeStruct(prefix.shape, F32)],
        scratch_shapes=[pltpu.VMEM((nseg, POOL_HALO + seg_len, pool_w), F32)],
        compiler_params=_cparams(("arbitrary", "arbitrary")),
        name="inproj",
    )(x3, prefix, ng, w_in, qg, kg, bd, pw, ps)


def _half_masks(q):
    lane = lax.broadcasted_iota(jnp.int32, q.shape, 1)
    zero = jnp.zeros_like(q)
    return jnp.where(lane < HEAD_DIM, q, zero), jnp.where(lane >= HEAD_DIM, q, zero)


def _qk(qz, kblk):
    return lax.dot_general(qz, kblk, (((1,), (1,)), ((), ())), preferred_element_type=F32)


def _subln(o, sg_ref):
    ms = jnp.mean(o * o, axis=-1, keepdims=True)
    return o * lax.rsqrt(ms + RMS_EPS) * sg_ref[...]


def _attn_body(slope_ref, lam_ref, q_ref, k_ref, v_ref, ka_ref, sg_ref, o_ref,
               kf_ref, vf_ref, s_ref, p_ref, *, tq, nq):
    h = pl.program_id(1)
    slope = slope_ref[h]
    lam = lam_ref[0]
    lane = lax.broadcasted_iota(jnp.int32, (tq, LANES), 1)
    kf_ref[:, 0:V_DIM] = k_ref[0]
    kf_ref[:, V_DIM:] = ka_ref[...]
    vf_ref[:, 0:V_DIM] = v_ref[0]
    t_len = vf_ref.shape[0]
    vf_ref[:, V_DIM:] = (lax.broadcasted_iota(jnp.int32, (t_len, LANES), 1) == 0).astype(BF16)

    row = lax.broadcasted_iota(jnp.int32, (tq, tq), 0)
    col = lax.broadcasted_iota(jnp.int32, (tq, tq), 1)
    rc = (row - col).astype(F32)
    vis = lax.shift_right_logical(col, 6) <= lax.shift_right_logical(row, 6)
    corr = jnp.where(vis, jnp.minimum(rc, 0.0) * (2.0 * slope), NEG_INF)

    for qi in range(nq):
        rows = slice(qi * tq, (qi + 1) * tq)
        nk = (qi + 1) * tq
        q1z, q2z = _half_masks(q_ref[0, rows, :])
        t = lax.broadcasted_iota(jnp.int32, (tq, LANES), 0) + qi * tq
        t_hi = lax.shift_left(lax.shift_right_logical(t, 8), 8).astype(F32)
        t_lo = jnp.bitwise_and(t, 255).astype(F32)
        qaug = jnp.where(lane == 0, -slope * t_hi,
                         jnp.where(lane == 1, -slope * t_lo,
                                   jnp.where(lane < 4, slope, 0.0))).astype(BF16)
        outs = []
        for m, qz in enumerate((q1z, q2z)):
            qa = jnp.concatenate([qz, qaug], axis=1)
            s_ref[m, :, 0:nk] = _qk(qa, kf_ref[0:nk, :])
            s_ref[m, :, nk - tq:nk] += corr
            mx = jnp.max(s_ref[m, :, 0:nk], axis=-1, keepdims=True)
            p_ref[m, :, 0:nk] = jnp.exp(s_ref[m, :, 0:nk] - mx).astype(BF16)
            outs.append(jnp.dot(p_ref[m, :, 0:nk], vf_ref[0:nk, :], preferred_element_type=F32))
        o1, o2 = outs
        c1 = 1.0 / o1[:, V_DIM:V_DIM + 1]
        c2 = lam / o2[:, V_DIM:V_DIM + 1]
        o = o1[:, 0:V_DIM] * c1 - o2[:, 0:V_DIM] * c2
        o_ref[0, rows, :] = _subln(o, sg_ref).astype(BF16)


def _attn_prompt(q, kb, vb, slopes, lam, sg):
    batch, t_len, width = q.shape
    tq = min(ATTN_TILE, t_len)
    nq = t_len // tq
    pos = jnp.arange(t_len, dtype=jnp.int32)
    ka = jnp.zeros((t_len, LANES), F32)
    ka = ka.at[:, 0:2].set(1.0).at[:, 2].set(((pos >> 8) << 8).astype(F32)).at[:, 3].set((pos & 255).astype(F32))
    smem = pl.BlockSpec(memory_space=pltpu.SMEM)
    seq = pl.BlockSpec((1, t_len, V_DIM), lambda b, h: (b, 0, h))
    return pl.pallas_call(
        functools.partial(_attn_body, tq=tq, nq=nq),
        grid=(batch, N_HEADS),
        in_specs=[
            smem, smem, seq, seq, seq,
            pl.BlockSpec((t_len, LANES), lambda b, h: (0, 0)),
            pl.BlockSpec((1, V_DIM), lambda b, h: (0, 0)),
        ],
        out_specs=seq,
        out_shape=jax.ShapeDtypeStruct((batch, t_len, width), BF16),
        scratch_shapes=[
            pltpu.VMEM((t_len, V_DIM + LANES), BF16),
            pltpu.VMEM((t_len, V_DIM + LANES), BF16),
            pltpu.VMEM((2, tq, t_len), F32),
            pltpu.VMEM((2, tq, t_len), BF16),
        ],
        compiler_params=_cparams(("arbitrary", "arbitrary")),
        name="attn_prompt",
    )(slopes, lam, q, kb, vb, ka.astype(BF16), sg)


def _attn_dec_body(slope_ref, lam_ref, q_ref, kc_ref, vc_ref, kn_ref, vn_ref, sg_ref, o_ref, *, past):
    lam = lam_ref[0]
    tq = q_ref.shape[1]
    qpos_a = lax.broadcasted_iota(jnp.int32, (tq, past), 0) + past
    kpos_a = lax.broadcasted_iota(jnp.int32, (tq, past), 1)
    qpos_b = lax.broadcasted_iota(jnp.int32, (tq, tq), 0) + past
    kpos_b = lax.broadcasted_iota(jnp.int32, (tq, tq), 1) + past

    def dist_and_vis(qpos, kpos):
        vis = lax.shift_right_logical(kpos, 6) <= lax.shift_right_logical(qpos, 6)
        return jnp.abs(qpos - kpos).astype(F32), vis

    dist_a, vis_a = dist_and_vis(qpos_a, kpos_a)
    dist_b, vis_b = dist_and_vis(qpos_b, kpos_b)

    for h in range(N_HEADS):
        cs = slice(h * V_DIM, (h + 1) * V_DIM)
        slope = slope_ref[h]
        bias_a = jnp.where(vis_a, -slope * dist_a, NEG_INF)
        bias_b = jnp.where(vis_b, -slope * dist_b, NEG_INF)
        q1z, q2z = _half_masks(q_ref[0, :, cs])
        kc = kc_ref[0, pl.ds(h, past, stride=N_HEADS), :].astype(BF16)
        vc = vc_ref[0, pl.ds(h, past, stride=N_HEADS), :].astype(BF16)
        kn = kn_ref[0, :, cs]

        def softmax_parts(qz):
            sa = _qk(qz, kc) + bias_a
            sb = _qk(qz, kn) + bias_b
            m = jnp.maximum(jnp.max(sa, axis=-1, keepdims=True), jnp.max(sb, axis=-1, keepdims=True))
            pa = jnp.exp(sa - m)
            pb = jnp.exp(sb - m)
            l = jnp.sum(pa, axis=-1, keepdims=True) + jnp.sum(pb, axis=-1, keepdims=True)
            return pa, pb, l

        pa1, pb1, l1 = softmax_parts(q1z)
        pa2, pb2, l2 = softmax_parts(q2z)
        c1 = 1.0 / l1
        c2 = lam / l2
        wa = (pa1 * c1 - pa2 * c2).astype(BF16)
        wb = (pb1 * c1 - pb2 * c2).astype(BF16)
        o = (jnp.dot(wa, vc, preferred_element_type=F32)
             + jnp.dot(wb, vn_ref[0, :, cs], preferred_element_type=F32))
        o_ref[0, :, cs] = _subln(o, sg_ref).astype(BF16)


def _attn_sample(q, kn, vn, cache_k, cache_v, slopes, lam, sg):
    batch, tq, width = q.shape
    past = cache_k.shape[1] // N_HEADS
    smem = pl.BlockSpec(memory_space=pltpu.SMEM)
    new = pl.BlockSpec((1, tq, width), lambda b: (b, 0, 0))
    old = pl.BlockSpec((1, past * N_HEADS, V_DIM), lambda b: (b, 0, 0))
    return pl.pallas_call(
        functools.partial(_attn_dec_body, past=past),
        grid=(batch,),
        in_specs=[smem, smem, new, old, old, new, new,
                  pl.BlockSpec((1, V_DIM), lambda b: (0, 0))],
        out_specs=new,
        out_shape=jax.ShapeDtypeStruct((batch, tq, width), BF16),
        compiler_params=_cparams(("arbitrary",)),
        name="attn_sample",
    )(slopes, lam, q, cache_k, cache_v, kn, vn, sg)


def _outproj_body(at_ref, sa_ref, gp_ref, x_ref, wo_ref, g_ref, rw_ref, rb_ref,
                  h_ref, ti_ref, tg_ref, cnt_ref):
    i = pl.program_id(0)
    merged = (sa_ref[...].astype(F32) * at_ref[...].astype(F32) + gp_ref[...].astype(F32)).astype(BF16)
    hh = x_ref[...] + jnp.dot(merged, wo_ref[...], preferred_element_type=F32)
    h_ref[...] = hh
    ms = jnp.mean(hh * hh, axis=-1, keepdims=True)
    hn = (hh * lax.rsqrt(ms + RMS_EPS) * g_ref[...]).astype(BF16)
    logits = jnp.dot(hn, rw_ref[...], preferred_element_type=F32) + rb_ref[...]
    lt = jnp.transpose(logits)[0:N_EXPERTS, :]
    tm = lt.shape[1]
    e_iota = lax.broadcasted_iota(jnp.int32, (N_EXPERTS, tm), 0)
    vals, idxs, hots = [], [], []
    cur = lt
    for _ in range(TOP_K):
        m = jnp.max(cur, axis=0, keepdims=True)
        idx = jnp.min(jnp.where(cur == m, e_iota, N_EXPERTS), axis=0, keepdims=True)
        hit = e_iota == idx
        vals.append(m)
        idxs.append(idx)
        hots.append(hit)
        cur = jnp.where(hit, -jnp.inf, cur)
    ex = [jnp.exp(v - vals[0]) for v in vals]
    den = ex[0] + ex[1] + ex[2] + ex[3]
    inv = 1.0 / den
    zi = jnp.zeros((8 - TOP_K, tm), jnp.int32)
    zf = jnp.zeros((8 - TOP_K, tm), F32)
    ti_ref[...] = jnp.concatenate(idxs + [zi], axis=0)
    tg_ref[...] = jnp.concatenate([e * inv for e in ex] + [zf], axis=0)
    hot = jnp.concatenate([hh_.astype(F32) for hh_ in hots], axis=0)
    csum = jnp.sum(hot, axis=1, keepdims=True)

    @pl.when(i == 0)
    def _():
        cnt_ref[...] = jnp.zeros_like(cnt_ref)

    cnt_ref[...] += jnp.broadcast_to(csum, cnt_ref.shape)


def _outproj(attn, sa, gp, x, w_out, g, rw, rb):
    n_tok, d_model = x.shape
    tm = min(ROW_TILE, n_tok)
    row = lambda i: (i, 0)
    fixed = lambda i: (0, 0)
    colb = lambda i: (0, i)
    return pl.pallas_call(
        _outproj_body,
        grid=(n_tok // tm,),
        in_specs=[
            pl.BlockSpec((tm, d_model), row),
            pl.BlockSpec((tm, d_model), row),
            pl.BlockSpec((tm, d_model), row),
            pl.BlockSpec((tm, d_model), row),
            pl.BlockSpec((d_model, d_model), fixed),
            pl.BlockSpec((1, d_model), fixed),
            pl.BlockSpec((d_model, LANES), fixed),
            pl.BlockSpec((1, LANES), fixed),
        ],
        out_specs=[
            pl.BlockSpec((tm, d_model), row),
            pl.BlockSpec((8, tm), colb),
            pl.BlockSpec((8, tm), colb),
            pl.BlockSpec((TOP_K * N_EXPERTS, LANES), fixed),
        ],
        out_shape=[
            jax.ShapeDtypeStruct((n_tok, d_model), F32),
            jax.ShapeDtypeStruct((8, n_tok), jnp.int32),
            jax.ShapeDtypeStruct((8, n_tok), F32),
            jax.ShapeDtypeStruct((TOP_K * N_EXPERTS, LANES), F32),
        ],
        compiler_params=_cparams(("arbitrary",)),
        name="outproj",
    )(attn, sa, gp, x, w_out, g, rw, rb)


def _rank_body(ti_ref, base_ref, tri_ref, dest_ref, carry_ref):
    i = pl.program_id(0)

    @pl.when(i == 0)
    def _():
        carry_ref[...] = jnp.zeros_like(carry_ref)

    tt = ti_ref.shape[1]
    e_iota = lax.broadcasted_iota(jnp.int32, (N_EXPERTS, tt), 0)
    hot = jnp.concatenate([(ti_ref[k:k + 1, :] == e_iota).astype(F32) for k in range(TOP_K)], axis=0)
    incl = jnp.dot(hot.astype(BF16), tri_ref[...], preferred_element_type=F32)
    slot = base_ref[:, 0:1] + carry_ref[:, 0:1] + incl - 1.0
    picked = hot * slot
    rows = [jnp.sum(picked[k * N_EXPERTS:(k + 1) * N_EXPERTS, :], axis=0, keepdims=True)
            for k in range(TOP_K)]
    rows.append(jnp.zeros((8 - TOP_K, tt), F32))
    dest_ref[...] = jnp.concatenate(rows, axis=0).astype(jnp.int32)
    carry_ref[...] += jnp.broadcast_to(jnp.sum(hot, axis=1, keepdims=True), carry_ref.shape)


def _rank(topi, base, tri):
    n_tok = topi.shape[1]
    tt = tri.shape[0]
    return pl.pallas_call(
        _rank_body,
        grid=(n_tok // tt,),
        in_specs=[
            pl.BlockSpec((8, tt), lambda i: (0, i)),
            pl.BlockSpec(base.shape, lambda i: (0, 0)),
            pl.BlockSpec((tt, tt), lambda i: (0, 0)),
        ],
        out_specs=pl.BlockSpec((8, tt), lambda i: (0, i)),
        out_shape=jax.ShapeDtypeStruct((8, n_tok), jnp.int32),
        scratch_shapes=[pltpu.VMEM((TOP_K * N_EXPERTS, LANES), F32)],
        compiler_params=_cparams(("arbitrary",)),
        name="rank",
    )(topi, base, tri)


def _row_copy(src_hbm, dst_buf, sem, src_row, dst_row):
    return pltpu.make_async_copy(src_hbm.at[pl.ds(src_row, 1)], dst_buf.at[pl.ds(dst_row, 1)], sem)


GATHER_CHUNKS = 4


def _moe_body(te_ref, nu_ref, rt_hbm, h_hbm, g_ref, wg_ref, bg_ref, wu_ref, bu_ref, wd_ref, bd_ref,
              y_ref, idx0, idx1, xb0, xb1, wgb, wub, wdb, isem, gsem, *, tm):
    i = pl.program_id(0)
    n_used = nu_ref[0]
    idx = (idx0, idx1)
    xbuf = (xb0, xb1)

    def idx_copy(tile, s):
        return pltpu.make_async_copy(rt_hbm.at[tile], idx[s], isem.at[s])

    def issue_gather(s, r0, r1):
        for r in range(r0, r1):
            _row_copy(h_hbm, xbuf[s], gsem.at[s], idx[s][r], r).start()

    def wait_gather(s):
        pltpu.make_async_copy(h_hbm.at[pl.ds(0, tm)], xbuf[s], gsem.at[s]).wait()

    @pl.when(i == 0)
    def _():
        idx_copy(0, 0).start()
        idx_copy(0, 0).wait()
        issue_gather(0, 0, tm)
        idx_copy(1, 1).start()

    def step(s):
        o = 1 - s
        wait_gather(s)
        idx_copy(i + 1, o).wait()
        idx_copy(i + 2, s).start()

        prev = te_ref[jnp.maximum(i - 1, 0)]
        changed = jnp.logical_or(i == 0, te_ref[i] != prev)

        @pl.when(changed)
        def _():
            wgb[...] = wg_ref[0].astype(BF16)
            wub[...] = wu_ref[0].astype(BF16)
            wdb[...] = wd_ref[0].astype(BF16)

        chunk = tm // GATHER_CHUNKS
        x = xbuf[s][...]
        ms = jnp.mean(x * x, axis=-1, keepdims=True)
        xb = (x * lax.rsqrt(ms + RMS_EPS) * g_ref[...]).astype(BF16)
        issue_gather(o, 0, chunk)
        gt = jnp.dot(xb, wgb[...], preferred_element_type=F32) + bg_ref[0]
        issue_gather(o, chunk, 2 * chunk)
        up = jnp.dot(xb, wub[...], preferred_element_type=F32) + bu_ref[0]
        gt = jnp.minimum(gt, SWIGLU_LIMIT)
        up = jnp.clip(up, -SWIGLU_LIMIT, SWIGLU_LIMIT)
        hdn = (up + 1.0) * (gt * _sigmoid(SWIGLU_ALPHA * gt))
        issue_gather(o, 2 * chunk, 3 * chunk)
        y_ref[...] = jnp.dot(hdn.astype(BF16), wdb[...], preferred_element_type=F32) + bd_ref[0]
        issue_gather(o, 3 * chunk, tm)

    for s in range(2):
        @pl.when(jnp.logical_and(i < n_used, i % 2 == s))
        def _(s=s):
            step(s)

        @pl.when(jnp.logical_and(i == n_used, i % 2 == s))
        def _(s=s):
            wait_gather(s)
            idx_copy(i + 1, 1 - s).wait()

    @pl.when(i >= n_used)
    def _():
        y_ref[...] = jnp.zeros_like(y_ref)


def _moe(tile_expert, n_used, row_tok, h_all, g, w_gate, b_gate, w_up, b_up, w_down, b_down):
    n_tiles, tm = row_tok.shape[0] - 2, row_tok.shape[1]
    d_model = h_all.shape[1]
    d_ff = w_gate.shape[2]
    wspec = lambda shape: pl.BlockSpec((1,) + shape, lambda i, te, nu: (te[i], 0, 0))
    any_spec = pl.BlockSpec(memory_space=pl.ANY)
    grid_spec = pltpu.PrefetchScalarGridSpec(
        num_scalar_prefetch=2,
        grid=(n_tiles + 1,),
        in_specs=[
            any_spec, any_spec,
            pl.BlockSpec((1, d_model), lambda i, te, nu: (0, 0)),
            wspec((d_model, d_ff)), wspec((1, d_ff)),
            wspec((d_model, d_ff)), wspec((1, d_ff)),
            wspec((d_ff, d_model)), wspec((1, d_model)),
        ],
        out_specs=pl.BlockSpec((tm, d_model), lambda i, te, nu: (i, 0)),
        scratch_shapes=[
            pltpu.SMEM((tm,), jnp.int32),
            pltpu.SMEM((tm,), jnp.int32),
            pltpu.VMEM((tm, d_model), F32),
            pltpu.VMEM((tm, d_model), F32),
            pltpu.VMEM((d_model, d_ff), BF16),
            pltpu.VMEM((d_model, d_ff), BF16),
            pltpu.VMEM((d_ff, d_model), BF16),
            pltpu.SemaphoreType.DMA((2,)),
            pltpu.SemaphoreType.DMA((2,)),
        ],
    )
    return pl.pallas_call(
        functools.partial(_moe_body, tm=tm),
        grid_spec=grid_spec,
        out_shape=jax.ShapeDtypeStruct(((n_tiles + 1) * tm, d_model), F32),
        compiler_params=_cparams(("arbitrary",)),
        name="moe",
    )(tile_expert, n_used, row_tok, h_all, g,
      w_gate, b_gate.reshape(N_EXPERTS, 1, d_ff), w_up, b_up.reshape(N_EXPERTS, 1, d_ff),
      w_down, b_down.reshape(N_EXPERTS, 1, d_model))


def _combine_body(dest_hbm, y_hbm, h_ref, gt_ref, o_ref, idx0, idx1, yb0, yb1, isem, gsem, *, tt):
    i = pl.program_id(0)
    n = pl.num_programs(0)
    idx = (idx0, idx1)
    ybuf = (yb0, yb1)

    def idx_copy(tile, s):
        return pltpu.make_async_copy(dest_hbm.at[tile], idx[s], isem.at[s])

    def issue_gather(s):
        for k in range(TOP_K):
            for r in range(tt):
                _row_copy(y_hbm, ybuf[s].at[k], gsem.at[s], idx[s][k * tt + r], r).start()

    def wait_gather(s):
        for k in range(TOP_K):
            pltpu.make_async_copy(y_hbm.at[pl.ds(0, tt)], ybuf[s].at[k], gsem.at[s]).wait()

    @pl.when(i == 0)
    def _():
        idx_copy(0, 0).start()
        idx_copy(0, 0).wait()
        issue_gather(0)
        idx_copy(1, 1).start()

    for s in range(2):
        @pl.when(i % 2 == s)
        def _(s=s):
            o = 1 - s
            idx_copy(i + 1, o).wait()
            issue_gather(o)
            idx_copy(i + 2, s).start()
            wait_gather(s)
            acc = h_ref[...]
            gts = gt_ref[...]
            for k in range(TOP_K):
                acc = acc + gts[:, k:k + 1] * ybuf[s][k]
            o_ref[...] = acc

            @pl.when(i == n - 1)
            def _():
                wait_gather(o)
                idx_copy(i + 2, s).wait()


def _combine(dest_tiles, y_rows, h, gates_tok):
    n_tiles, width = dest_tiles.shape[0] - 2, dest_tiles.shape[1]
    tt = width // TOP_K
    n_tok, d_model = h.shape
    any_spec = pl.BlockSpec(memory_space=pl.ANY)
    return pl.pallas_call(
        functools.partial(_combine_body, tt=tt),
        grid=(n_tiles,),
        in_specs=[any_spec, any_spec,
                  pl.BlockSpec((tt, d_model), lambda i: (i, 0)),
                  pl.BlockSpec((tt, 8), lambda i: (i, 0))],
        out_specs=pl.BlockSpec((tt, d_model), lambda i: (i, 0)),
        out_shape=jax.ShapeDtypeStruct((n_tok, d_model), F32),
        scratch_shapes=[
            pltpu.SMEM((TOP_K * tt,), jnp.int32),
            pltpu.SMEM((TOP_K * tt,), jnp.int32),
            pltpu.VMEM((TOP_K, tt, d_model), F32),
            pltpu.VMEM((TOP_K, tt, d_model), F32),
            pltpu.SemaphoreType.DMA((2,)),
            pltpu.SemaphoreType.DMA((2,)),
        ],
        compiler_params=_cparams(("arbitrary",)),
        name="combine",
    )(dest_tiles, y_rows, h, gates_tok)


def _block_diag_ones():
    r = lax.broadcasted_iota(jnp.int32, (NORM_BLOCK, NORM_BLOCK), 0) // HEAD_DIM
    c = lax.broadcasted_iota(jnp.int32, (NORM_BLOCK, NORM_BLOCK), 1) // HEAD_DIM
    return (r == c).astype(BF16)


def _dest_tiles(dest, tt):
    n_tok = dest.shape[1]
    tiles = dest[:TOP_K].reshape(TOP_K, n_tok // tt, tt).transpose(1, 0, 2).reshape(n_tok // tt, TOP_K * tt)
    return jnp.pad(tiles, ((0, 2), (0, 0)))


def _layer(xp, xs, ck, cv, sp, lam, lam_init, ng, w_in, qng, kng, slg, pw, ps, w_out, fg,
           rw, rb, w_gate, b_gate, w_up, b_up, w_down, b_down):
    bp, t_len, d_model = xp.shape
    bs, ts, _ = xs.shape
    past = ck.shape[1]
    pool_w = sp.shape[-1]
    n_p, n_s = bp * t_len, bs * ts

    w_in_b = w_in.astype(BF16)
    w_out_b = w_out.astype(BF16)
    pw_b = pw.astype(BF16)
    reps = NORM_BLOCK // HEAD_DIM
    qg = (jnp.tile(qng.astype(F32), reps) * (HEAD_DIM ** -0.5)).reshape(1, NORM_BLOCK)
    kg = jnp.tile(kng.astype(F32), reps).reshape(1, NORM_BLOCK)
    bd = _block_diag_ones()
    ng2 = ng.astype(F32).reshape(1, d_model)
    ps2 = ps.astype(F32).reshape(1, d_model)
    sg = (slg.astype(F32) * (1.0 - lam_init)).reshape(1, V_DIM)
    slopes = jnp.exp2(-(8.0 / N_HEADS) * jnp.arange(1, N_HEADS + 1, dtype=F32))
    lam1 = lam.reshape(1).astype(F32)
    fg2 = fg.astype(F32).reshape(1, d_model)
    rw_b = jnp.zeros((d_model, LANES), BF16).at[:, :N_EXPERTS].set(rw.astype(BF16))
    rb2 = jnp.zeros((1, LANES), F32).at[0, :N_EXPERTS].set(rb.astype(F32))

    tm = min(ROW_TILE, t_len)
    zero_pre = jnp.zeros((bp, POOL_HALO, pool_w), F32)
    qp, kp, kpb, vp, vpb, sap, gpp, utp = _inproj(
        xp, zero_pre, ng2, w_in_b, qg, kg, bd, pw_b, ps2, nseg=1, seg_len=tm, start_pos=0, carry=True)
    atp = _attn_prompt(qp, kpb, vpb, slopes, lam1, sg)
    hp, tip, tgp, cntp = _outproj(atp.reshape(n_p, d_model), sap.reshape(n_p, d_model),
                                  gpp.reshape(n_p, d_model), xp.reshape(n_p, d_model),
                                  w_out_b, fg2, rw_b, rb2)

    pre_s = jnp.concatenate([jnp.zeros((bs, POOL_HALO - sp.shape[1], pool_w), F32), sp.astype(F32)], axis=1)
    qs, ks, ksb, vs, vsb, sas, gps, uts = _inproj(
        xs.reshape(1, n_s, d_model), pre_s, ng2, w_in_b, qg, kg, bd, pw_b, ps2,
        nseg=bs, seg_len=ts, start_pos=past, carry=False)
    ats = _attn_sample(qs.reshape(bs, ts, d_model), ksb.reshape(bs, ts, d_model), vsb.reshape(bs, ts, d_model),
                       ck.reshape(bs, past * N_HEADS, V_DIM), cv.reshape(bs, past * N_HEADS, V_DIM),
                       slopes, lam1, sg)
    hs, tis, tgs, cnts = _outproj(ats.reshape(n_s, d_model), sas.reshape(n_s, d_model),
                                  gps.reshape(n_s, d_model), xs.reshape(n_s, d_model),
                                  w_out_b, fg2, rw_b, rb2)

    n_tok = n_p + n_s
    h_all = jnp.concatenate([hp, hs], axis=0)
    topi = jnp.concatenate([tip, tis], axis=1)
    gates = jnp.concatenate([tgp, tgs], axis=1)
    cnt = (cntp[:, 0] + cnts[:, 0]).astype(jnp.int32).reshape(TOP_K, N_EXPERTS)
    per_expert = jnp.sum(cnt, axis=0)
    padded = (per_expert + MOE_TILE - 1) // MOE_TILE * MOE_TILE
    pad_end = jnp.cumsum(padded)
    pad_start = pad_end - padded
    base = pad_start[None, :] + jnp.cumsum(cnt, axis=0) - cnt
    base_f = jnp.broadcast_to(base.reshape(-1, 1).astype(F32), (TOP_K * N_EXPERTS, LANES))
    tt = TOK_TILE
    tri = (lax.broadcasted_iota(jnp.int32, (tt, tt), 0) <= lax.broadcasted_iota(jnp.int32, (tt, tt), 1)).astype(BF16)
    dest = _rank(topi, base_f, tri)

    n_tiles = (n_tok * TOP_K) // MOE_TILE + N_EXPERTS
    tok_ids = jnp.tile(jnp.arange(n_tok, dtype=jnp.int32), TOP_K)
    row_tok = jnp.zeros(((n_tiles + 2) * MOE_TILE,), jnp.int32).at[dest[:TOP_K].reshape(-1)].set(
        tok_ids, unique_indices=True, mode='promise_in_bounds')
    n_used = (pad_end[-1] // MOE_TILE).astype(jnp.int32)
    tile_start = jnp.arange(n_tiles + 1, dtype=jnp.int32) * MOE_TILE
    last_start = jnp.maximum(pad_end[-1] - MOE_TILE, 0)
    tile_expert = jnp.minimum(
        jnp.sum((jnp.minimum(tile_start, last_start)[:, None] >= pad_end[None, :]).astype(jnp.int32), axis=1),
        N_EXPERTS - 1)
    y_rows = _moe(tile_expert, n_used.reshape(1), row_tok.reshape(n_tiles + 2, MOE_TILE), h_all, fg2,
                  w_gate, b_gate, w_up, b_up, w_down, b_down)

    gates_tok = jnp.transpose(gates)
    yp = _combine(_dest_tiles(dest[:, :n_p], tt), y_rows, hp, gates_tok[:n_p])
    ys = _combine(_dest_tiles(dest[:, n_p:], tt), y_rows, hs, gates_tok[n_p:])

    heads = (N_HEADS, V_DIM)
    return (yp.reshape(bp, t_len, d_model), ys.reshape(bs, ts, d_model),
            kp.reshape(bp, t_len, *heads), vp.reshape(bp, t_len, *heads), utp[:, 1:],
            ks.reshape(bs, ts, *heads), vs.reshape(bs, ts, *heads), uts[:, 1:])


def kernel(x_prompt, x_sample, cache_k, cache_v, state_pool, norm_mix_g, w_in, q_norm_g, k_norm_g,
           lambda_q1, lambda_k1, lambda_q2, lambda_k2, subln_g, pool_w, pool_scale, w_out, norm_ffn_g,
           router_w, router_b, w_gate, b_gate, w_up, b_up, w_down, b_down):
    depth = w_in.shape[0]
    hp, hs = x_prompt, x_sample
    outs = [[] for _ in range(6)]
    for layer in range(depth):
        lam_init = 0.8 - 0.6 * math.exp(-0.3 * layer)
        lam = (jnp.exp(jnp.sum(lambda_q1[layer].astype(F32) * lambda_k1[layer].astype(F32)))
               - jnp.exp(jnp.sum(lambda_q2[layer].astype(F32) * lambda_k2[layer].astype(F32)))
               + lam_init)
        hp, hs, kp, vp, up, ks, vs, us = _layer(
            hp, hs, cache_k[layer], cache_v[layer], state_pool[layer], lam, lam_init,
            norm_mix_g[layer], w_in[layer], q_norm_g[layer], k_norm_g[layer], subln_g[layer],
            pool_w[layer], pool_scale[layer], w_out[layer], norm_ffn_g[layer],
            router_w[layer], router_b[layer], w_gate[layer], b_gate[layer], w_up[layer], b_up[layer],
            w_down[layer], b_down[layer])
        for lst, val in zip(outs, (kp, vp, up, ks, vs, us)):
            lst.append(val)
    return (hp, hs) + tuple(jnp.stack(o) for o in outs)
```

```python
import functools
import math

import jax
import jax.numpy as jnp
from jax import lax
from jax.experimental import pallas as pl
from jax.experimental.pallas import tpu as pltpu

F32 = jnp.float32
BF16 = jnp.bfloat16

CHUNK = 64
N_HEADS = 8
HEAD_DIM = 64
V_DIM = 2 * HEAD_DIM
POOL_WINDOWS = (2, 4, 8, 16)
POOL_GROUP_DIM = 128
POOL_OUT_DIM = 256
POOL_HALO = 16
N_EXPERTS = 32
TOP_K = 4
SWIGLU_LIMIT = 7.0
SWIGLU_ALPHA = 1.702
RMS_EPS = 1e-6
NEG_INF = -1e30

LANES = 128
ROW_SUB = 8
NORM_BLOCK = 256
VMEM_LIMIT = 56 * 1024 * 1024

ROW_TILE = 512
ATTN_TILE = 256
MOE_TILE = 256
TOK_TILE = 256


def _sigmoid(x):
    return 1.0 / (1.0 + jnp.exp(-x))


def _cparams(sem):
    return pltpu.CompilerParams(dimension_semantics=sem, vmem_limit_bytes=VMEM_LIMIT)


def _inproj_body(x_ref, pre_ref, ng_ref, w_ref, qg_ref, kg_ref, bd_ref, pw_ref, ps_ref,
                 q_ref, k_ref, kb_ref, v_ref, vb_ref, sa_ref, gp_ref, ut_ref, ext_ref,
                 *, nseg, seg_len, start_pos, carry, attn_w, pool_w):
    rows = nseg * seg_len
    t = pl.program_id(1)
    x = x_ref[0]
    ms = jnp.mean(x * x, axis=-1, keepdims=True)
    xn = (x * lax.rsqrt(ms + RMS_EPS) * ng_ref[...]).astype(BF16)

    def proj(c0, width):
        return jnp.dot(xn, w_ref[:, c0:c0 + width], preferred_element_type=F32)

    bd = bd_ref[...]

    def group_norm(p, g_ref):
        ss = jnp.dot((p * p).astype(BF16), bd, preferred_element_type=F32)
        return p * lax.rsqrt(ss * (1.0 / HEAD_DIM) + RMS_EPS) * g_ref[...]

    nb = NORM_BLOCK
    heads_per_block = nb // V_DIM

    def store_heads(ref, val, c):
        for j in range(heads_per_block):
            head = c * heads_per_block + j
            ref[0, pl.ds(head, rows, stride=N_HEADS), :] = val[:, j * V_DIM:(j + 1) * V_DIM]

    for c in range(attn_w // nb):
        cs = slice(c * nb, (c + 1) * nb)
        q_ref[0, :, cs] = group_norm(proj(c * nb, nb), qg_ref).astype(BF16)
    for c in range(attn_w // nb):
        cs = slice(c * nb, (c + 1) * nb)
        kn = group_norm(proj(attn_w + c * nb, nb), kg_ref)
        store_heads(k_ref, kn, c)
        kb_ref[0, :, cs] = kn.astype(BF16)
    for c in range(attn_w // nb):
        cs = slice(c * nb, (c + 1) * nb)
        vv = proj(2 * attn_w + c * nb, nb)
        store_heads(v_ref, vv, c)
        vb_ref[0, :, cs] = vv.astype(BF16)

    if carry:
        @pl.when(t == 0)
        def _():
            ext_ref[:, 0:POOL_HALO, :] = pre_ref[...]
    else:
        ext_ref[:, 0:POOL_HALO, :] = pre_ref[...]
    for c in range(pool_w // nb):
        cs = slice(c * nb, (c + 1) * nb)
        u = proj(3 * attn_w + c * nb, nb)
        ext_ref[:, POOL_HALO:POOL_HALO + seg_len, cs] = u.reshape(nseg, seg_len, nb)

    ga0 = 3 * attn_w + pool_w
    d_model = attn_w
    for c in range(d_model // nb):
        cs = slice(c * nb, (c + 1) * nb)
        sa_ref[0, :, cs] = _sigmoid(proj(ga0 + c * nb, nb)).astype(BF16)

    gb0 = ga0 + d_model
    row = lax.broadcasted_iota(jnp.int32, (1, seg_len, 1), 1)
    pos = row + start_pos
    if carry:
        pos = pos + t * seg_len
    for g, w in enumerate(POOL_WINDOWS):
        cs = slice(g * POOL_GROUP_DIM, (g + 1) * POOL_GROUP_DIM)
        own = ext_ref[:, POOL_HALO:POOL_HALO + seg_len, cs]
        acc = own
        for i in range(1, w):
            acc = acc + ext_ref[:, POOL_HALO - i:POOL_HALO - i + seg_len, cs]
        inv = 1.0 / jnp.minimum(w, pos + 1).astype(F32)
        z = (acc * inv - own).reshape(rows, POOL_GROUP_DIM)
        os_ = slice(g * POOL_OUT_DIM, (g + 1) * POOL_OUT_DIM)
        yp = jnp.dot(z.astype(BF16), pw_ref[g], preferred_element_type=F32) * ps_ref[:, os_]
        gb = proj(gb0 + g * POOL_OUT_DIM, POOL_OUT_DIM)
        gp_ref[0, :, os_] = (_sigmoid(gb) * yp).astype(BF16)

    tail = ext_ref[:, seg_len:seg_len + POOL_HALO, :]
    ut_ref[...] = tail
    if carry:
        ext_ref[:, 0:POOL_HALO, :] = tail


def _inproj(x3, prefix, ng, w_in, qg, kg, bd, pw, ps, *, nseg, seg_len, start_pos, carry):
    groups, t_len, d_model = x3.shape
    rows = nseg * seg_len
    steps = t_len // rows
    in_cols = w_in.shape[1]
    pool_w = prefix.shape[-1]
    attn_w = d_model
    assert in_cols == 3 * attn_w + pool_w + 2 * d_model
    tok = lambda b, t: (b, t, 0)
    fixed2 = lambda b, t: (0, 0)
    act = lambda dt: jax.ShapeDtypeStruct((groups, t_len, d_model), dt)
    by_head = jax.ShapeDtypeStruct((groups, t_len * N_HEADS, V_DIM), F32)
    tok_spec = pl.BlockSpec((1, rows, d_model), tok)
    head_spec = pl.BlockSpec((1, rows * N_HEADS, V_DIM), tok)
    body = functools.partial(_inproj_body, nseg=nseg, seg_len=seg_len, start_pos=start_pos,
                             carry=carry, attn_w=attn_w, pool_w=pool_w)
    return pl.pallas_call(
        body,
        grid=(groups, steps),
        in_specs=[
            pl.BlockSpec((1, rows, d_model), tok),
            pl.BlockSpec((nseg, POOL_HALO, pool_w), lambda b, t: (b, 0, 0)),
            pl.BlockSpec((1, d_model), fixed2),
            pl.BlockSpec((d_model, in_cols), fixed2, pipeline_mode=pl.Buffered(1)),
            pl.BlockSpec((1, NORM_BLOCK), fixed2),
            pl.BlockSpec((1, NORM_BLOCK), fixed2),
            pl.BlockSpec((NORM_BLOCK, NORM_BLOCK), fixed2),
            pl.BlockSpec(pw.shape, lambda b, t: (0, 0, 0)),
            pl.BlockSpec((1, d_model), fixed2),
        ],
        out_specs=[tok_spec, head_spec, tok_spec, head_spec, tok_spec, tok_spec, tok_spec,
                   pl.BlockSpec((nseg, POOL_HALO, pool_w), lambda b, t: (b, 0, 0))],
        out_shape=[act(BF16), by_head, act(BF16), by_head, act(BF16), act(BF16), act(BF16),
                   jax.ShapeDtypeStruct(prefix.shape, F32)],
        scratch_shapes=[pltpu.VMEM((nseg, POOL_HALO + seg_len, pool_w), F32)],
        compiler_params=_cparams(("arbitrary", "arbitrary")),
        name="inproj",
    )(x3, prefix, ng, w_in, qg, kg, bd, pw, ps)


def _half_masks(q):
    lane = lax.broadcasted_iota(jnp.int32, q.shape, 1)
    zero = jnp.zeros_like(q)
    return jnp.where(lane < HEAD_DIM, q, zero), jnp.where(lane >= HEAD_DIM, q, zero)


def _qk(qz, kblk):
    return lax.dot_general(qz, kblk, (((1,), (1,)), ((), ())), preferred_element_type=F32)


def _subln(o, sg_ref):
    ms = jnp.mean(o * o, axis=-1, keepdims=True)
    return o * lax.rsqrt(ms + RMS_EPS) * sg_ref[...]


def _attn_body(slope_ref, lam_ref, q_ref, k_ref, v_ref, ka_ref, sg_ref, o_ref,
               kf_ref, vf_ref, s_ref, p_ref, *, tq, nq):
    h = pl.program_id(1)
    slope = slope_ref[h]
    lam = lam_ref[0]
    lane = lax.broadcasted_iota(jnp.int32, (tq, LANES), 1)
    kf_ref[:, 0:V_DIM] = k_ref[0]
    kf_ref[:, V_DIM:] = ka_ref[...]
    vf_ref[:, 0:V_DIM] = v_ref[0]
    t_len = vf_ref.shape[0]
    vf_ref[:, V_DIM:] = (lax.broadcasted_iota(jnp.int32, (t_len, LANES), 1) == 0).astype(BF16)

    row = lax.broadcasted_iota(jnp.int32, (tq, tq), 0)
    col = lax.broadcasted_iota(jnp.int32, (tq, tq), 1)
    rc = (row - col).astype(F32)
    vis = lax.shift_right_logical(col, 6) <= lax.shift_right_logical(row, 6)
    corr = jnp.where(vis, jnp.minimum(rc, 0.0) * (2.0 * slope), NEG_INF)

    for qi in range(nq):
        rows = slice(qi * tq, (qi + 1) * tq)
        nk = (qi + 1) * tq
        q1z, q2z = _half_masks(q_ref[0, rows, :])
        t = lax.broadcasted_iota(jnp.int32, (tq, LANES), 0) + qi * tq
        t_hi = lax.shift_left(lax.shift_right_logical(t, 8), 8).astype(F32)
        t_lo = jnp.bitwise_and(t, 255).astype(F32)
        qaug = jnp.where(lane == 0, -slope * t_hi,
                         jnp.where(lane == 1, -slope * t_lo,
                                   jnp.where(lane < 4, slope, 0.0))).astype(BF16)
        outs = []
        for m, qz in enumerate((q1z, q2z)):
            qa = jnp.concatenate([qz, qaug], axis=1)
            s_ref[m, :, 0:nk] = _qk(qa, kf_ref[0:nk, :])
            s_ref[m, :, nk - tq:nk] += corr
            mx = jnp.max(s_ref[m, :, 0:nk], axis=-1, keepdims=True)
            p_ref[m, :, 0:nk] = jnp.exp(s_ref[m, :, 0:nk] - mx).astype(BF16)
            outs.append(jnp.dot(p_ref[m, :, 0:nk], vf_ref[0:nk, :], preferred_element_type=F32))
        o1, o2 = outs
        c1 = 1.0 / o1[:, V_DIM:V_DIM + 1]
        c2 = lam / o2[:, V_DIM:V_DIM + 1]
        o = o1[:, 0:V_DIM] * c1 - o2[:, 0:V_DIM] * c2
        o_ref[0, rows, :] = _subln(o, sg_ref).astype(BF16)


def _attn_prompt(q, kb, vb, slopes, lam, sg):
    batch, t_len, width = q.shape
    tq = min(ATTN_TILE, t_len)
    nq = t_len // tq
    pos = jnp.arange(t_len, dtype=jnp.int32)
    ka = jnp.zeros((t_len, LANES), F32)
    ka = ka.at[:, 0:2].set(1.0).at[:, 2].set(((pos >> 8) << 8).astype(F32)).at[:, 3].set((pos & 255).astype(F32))
    smem = pl.BlockSpec(memory_space=pltpu.SMEM)
    seq = pl.BlockSpec((1, t_len, V_DIM), lambda b, h: (b, 0, h))
    return pl.pallas_call(
        functools.partial(_attn_body, tq=tq, nq=nq),
        grid=(batch, N_HEADS),
        in_specs=[
            smem, smem, seq, seq, seq,
            pl.BlockSpec((t_len, LANES), lambda b, h: (0, 0)),
            pl.BlockSpec((1, V_DIM), lambda b, h: (0, 0)),
        ],
        out_specs=seq,
        out_shape=jax.ShapeDtypeStruct((batch, t_len, width), BF16),
        scratch_shapes=[
            pltpu.VMEM((t_len, V_DIM + LANES), BF16),
            pltpu.VMEM((t_len, V_DIM + LANES), BF16),
            pltpu.VMEM((2, tq, t_len), F32),
            pltpu.VMEM((2, tq, t_len), BF16),
        ],
        compiler_params=_cparams(("arbitrary", "arbitrary")),
        name="attn_prompt",
    )(slopes, lam, q, kb, vb, ka.astype(BF16), sg)


def _attn_dec_body(slope_ref, lam_ref, q_ref, kc_ref, vc_ref, kn_ref, vn_ref, sg_ref, o_ref, *, past):
    lam = lam_ref[0]
    tq = q_ref.shape[1]
    qpos_a = lax.broadcasted_iota(jnp.int32, (tq, past), 0) + past
    kpos_a = lax.broadcasted_iota(jnp.int32, (tq, past), 1)
    qpos_b = lax.broadcasted_iota(jnp.int32, (tq, tq), 0) + past
    kpos_b = lax.broadcasted_iota(jnp.int32, (tq, tq), 1) + past

    def dist_and_vis(qpos, kpos):
        vis = lax.shift_right_logical(kpos, 6) <= lax.shift_right_logical(qpos, 6)
        return jnp.abs(qpos - kpos).astype(F32), vis

    dist_a, vis_a = dist_and_vis(qpos_a, kpos_a)
    dist_b, vis_b = dist_and_vis(qpos_b, kpos_b)

    for h in range(N_HEADS):
        cs = slice(h * V_DIM, (h + 1) * V_DIM)
        slope = slope_ref[h]
        bias_a = jnp.where(vis_a, -slope * dist_a, NEG_INF)
        bias_b = jnp.where(vis_b, -slope * dist_b, NEG_INF)
        q1z, q2z = _half_masks(q_ref[0, :, cs])
        kc = kc_ref[0, pl.ds(h, past, stride=N_HEADS), :].astype(BF16)
        vc = vc_ref[0, pl.ds(h, past, stride=N_HEADS), :].astype(BF16)
        kn = kn_ref[0, :, cs]

        def softmax_parts(qz):
            sa = _qk(qz, kc) + bias_a
            sb = _qk(qz, kn) + bias_b
            m = jnp.maximum(jnp.max(sa, axis=-1, keepdims=True), jnp.max(sb, axis=-1, keepdims=True))
            pa = jnp.exp(sa - m)
            pb = jnp.exp(sb - m)
            l = jnp.sum(pa, axis=-1, keepdims=True) + jnp.sum(pb, axis=-1, keepdims=True)
            return pa, pb, l

        pa1, pb1, l1 = softmax_parts(q1z)
        pa2, pb2, l2 = softmax_parts(q2z)
        c1 = 1.0 / l1
        c2 = lam / l2
        wa = (pa1 * c1 - pa2 * c2).astype(BF16)
        wb = (pb1 * c1 - pb2 * c2).astype(BF16)
        o = (jnp.dot(wa, vc, preferred_element_type=F32)
             + jnp.dot(wb, vn_ref[0, :, cs], preferred_element_type=F32))
        o_ref[0, :, cs] = _subln(o, sg_ref).astype(BF16)


def _attn_sample(q, kn, vn, cache_k, cache_v, slopes, lam, sg):
    batch, tq, width = q.shape
    past = cache_k.shape[1] // N_HEADS
    smem = pl.BlockSpec(memory_space=pltpu.SMEM)
    new = pl.BlockSpec((1, tq, width), lambda b: (b, 0, 0))
    old = pl.BlockSpec((1, past * N_HEADS, V_DIM), lambda b: (b, 0, 0))
    return pl.pallas_call(
        functools.partial(_attn_dec_body, past=past),
        grid=(batch,),
        in_specs=[smem, smem, new, old, old, new, new,
                  pl.BlockSpec((1, V_DIM), lambda b: (0, 0))],
        out_specs=new,
        out_shape=jax.ShapeDtypeStruct((batch, tq, width), BF16),
        compiler_params=_cparams(("arbitrary",)),
        name="attn_sample",
    )(slopes, lam, q, cache_k, cache_v, kn, vn, sg)


def _outproj_body(at_ref, sa_ref, gp_ref, x_ref, wo_ref, g_ref, rw_ref, rb_ref,
                  h_ref, ti_ref, tg_ref, cnt_ref):
    i = pl.program_id(0)
    merged = (sa_ref[...].astype(F32) * at_ref[...].astype(F32) + gp_ref[...].astype(F32)).astype(BF16)
    hh = x_ref[...] + jnp.dot(merged, wo_ref[...], preferred_element_type=F32)
    h_ref[...] = hh
    ms = jnp.mean(hh * hh, axis=-1, keepdims=True)
    hn = (hh * lax.rsqrt(ms + RMS_EPS) * g_ref[...]).astype(BF16)
    logits = jnp.dot(hn, rw_ref[...], preferred_element_type=F32) + rb_ref[...]
    lt = jnp.transpose(logits)[0:N_EXPERTS, :]
    tm = lt.shape[1]
    e_iota = lax.broadcasted_iota(jnp.int32, (N_EXPERTS, tm), 0)
    vals, idxs, hots = [], [], []
    cur = lt
    for _ in range(TOP_K):
        m = jnp.max(cur, axis=0, keepdims=True)
        idx = jnp.min(jnp.where(cur == m, e_iota, N_EXPERTS), axis=0, keepdims=True)
        hit = e_iota == idx
        vals.append(m)
        idxs.append(idx)
        hots.append(hit)
        cur = jnp.where(hit, -jnp.inf, cur)
    ex = [jnp.exp(v - vals[0]) for v in vals]
    den = ex[0] + ex[1] + ex[2] + ex[3]
    inv = 1.0 / den
    zi = jnp.zeros((8 - TOP_K, tm), jnp.int32)
    zf = jnp.zeros((8 - TOP_K, tm), F32)
    ti_ref[...] = jnp.concatenate(idxs + [zi], axis=0)
    tg_ref[...] = jnp.concatenate([e * inv for e in ex] + [zf], axis=0)
    hot = jnp.concatenate([hh_.astype(F32) for hh_ in hots], axis=0)
    csum = jnp.sum(hot, axis=1, keepdims=True)

    @pl.when(i == 0)
    def _():
        cnt_ref[...] = jnp.zeros_like(cnt_ref)

    cnt_ref[...] += jnp.broadcast_to(csum, cnt_ref.shape)


def _outproj(attn, sa, gp, x, w_out, g, rw, rb):
    n_tok, d_model = x.shape
    tm = min(ROW_TILE, n_tok)
    row = lambda i: (i, 0)
    fixed = lambda i: (0, 0)
    colb = lambda i: (0, i)
    return pl.pallas_call(
        _outproj_body,
        grid=(n_tok // tm,),
        in_specs=[
            pl.BlockSpec((tm, d_model), row),
            pl.BlockSpec((tm, d_model), row),
            pl.BlockSpec((tm, d_model), row),
            pl.BlockSpec((tm, d_model), row),
            pl.BlockSpec((d_model, d_model), fixed),
            pl.BlockSpec((1, d_model), fixed),
            pl.BlockSpec((d_model, LANES), fixed),
            pl.BlockSpec((1, LANES), fixed),
        ],
        out_specs=[
            pl.BlockSpec((tm, d_model), row),
            pl.BlockSpec((8, tm), colb),
            pl.BlockSpec((8, tm), colb),
            pl.BlockSpec((TOP_K * N_EXPERTS, LANES), fixed),
        ],
        out_shape=[
            jax.ShapeDtypeStruct((n_tok, d_model), F32),
            jax.ShapeDtypeStruct((8, n_tok), jnp.int32),
            jax.ShapeDtypeStruct((8, n_tok), F32),
            jax.ShapeDtypeStruct((TOP_K * N_EXPERTS, LANES), F32),
        ],
        compiler_params=_cparams(("arbitrary",)),
        name="outproj",
    )(attn, sa, gp, x, w_out, g, rw, rb)


def _rank_body(ti_ref, base_ref, tri_ref, dest_ref, carry_ref):
    i = pl.program_id(0)

    @pl.when(i == 0)
    def _():
        carry_ref[...] = jnp.zeros_like(carry_ref)

    tt = ti_ref.shape[1]
    e_iota = lax.broadcasted_iota(jnp.int32, (N_EXPERTS, tt), 0)
    hot = jnp.concatenate([(ti_ref[k:k + 1, :] == e_iota).astype(F32) for k in range(TOP_K)], axis=0)
    incl = jnp.dot(hot.astype(BF16), tri_ref[...], preferred_element_type=F32)
    slot = base_ref[:, 0:1] + carry_ref[:, 0:1] + incl - 1.0
    picked = hot * slot
    rows = [jnp.sum(picked[k * N_EXPERTS:(k + 1) * N_EXPERTS, :], axis=0, keepdims=True)
            for k in range(TOP_K)]
    rows.append(jnp.zeros((8 - TOP_K, tt), F32))
    dest_ref[...] = jnp.concatenate(rows, axis=0).astype(jnp.int32)
    carry_ref[...] += jnp.broadcast_to(jnp.sum(hot, axis=1, keepdims=True), carry_ref.shape)


def _rank(topi, base, tri):
    n_tok = topi.shape[1]
    tt = tri.shape[0]
    return pl.pallas_call(
        _rank_body,
        grid=(n_tok // tt,),
        in_specs=[
            pl.BlockSpec((8, tt), lambda i: (0, i)),
            pl.BlockSpec(base.shape, lambda i: (0, 0)),
            pl.BlockSpec((tt, tt), lambda i: (0, 0)),
        ],
        out_specs=pl.BlockSpec((8, tt), lambda i: (0, i)),
        out_shape=jax.ShapeDtypeStruct((8, n_tok), jnp.int32),
        scratch_shapes=[pltpu.VMEM((TOP_K * N_EXPERTS, LANES), F32)],
        compiler_params=_cparams(("arbitrary",)),
        name="rank",
    )(topi, base, tri)


def _row_copy(src_hbm, dst_buf, sem, src_row, dst_row):
    return pltpu.make_async_copy(src_hbm.at[pl.ds(src_row, 1)], dst_buf.at[pl.ds(dst_row, 1)], sem)


GATHER_CHUNKS = 4


MOE_BUFS = 3


def _moe_body(te_ref, nu_ref, rt_hbm, h_hbm, g_ref, wg_ref, bg_ref, wu_ref, bu_ref, wd_ref, bd_ref,
              y_ref, idx0, idx1, idx2, xb0, xb1, xb2, wgb, wub, wdb, isem, gsem, *, tm):
    i = pl.program_id(0)
    n_used = nu_ref[0]
    idx = (idx0, idx1, idx2)
    xbuf = (xb0, xb1, xb2)
    nb = MOE_BUFS

    def idx_copy(tile, s):
        return pltpu.make_async_copy(rt_hbm.at[tile], idx[s], isem.at[s])

    def issue_gather(s, r0, r1):
        for r in range(r0, r1):
            _row_copy(h_hbm, xbuf[s], gsem.at[s], idx[s][r], r).start()

    def wait_gather(s):
        pltpu.make_async_copy(h_hbm.at[pl.ds(0, tm)], xbuf[s], gsem.at[s]).wait()

    @pl.when(i == 0)
    def _():
        for s in range(nb - 1):
            idx_copy(s, s).start()
            idx_copy(s, s).wait()
            issue_gather(s, 0, tm)
        idx_copy(nb - 1, nb - 1).start()

    def step(s):
        o = (s + nb - 1) % nb
        wait_gather(s)
        idx_copy(i + nb - 1, o).wait()
        idx_copy(i + nb, s).start()

        prev = te_ref[jnp.maximum(i - 1, 0)]
        changed = jnp.logical_or(i == 0, te_ref[i] != prev)

        @pl.when(changed)
        def _():
            wgb[...] = wg_ref[0].astype(BF16)
            wub[...] = wu_ref[0].astype(BF16)
            wdb[...] = wd_ref[0].astype(BF16)

        chunk = tm // GATHER_CHUNKS
        x = xbuf[s][...]
        ms = jnp.mean(x * x, axis=-1, keepdims=True)
        xb = (x * lax.rsqrt(ms + RMS_EPS) * g_ref[...]).astype(BF16)
        issue_gather(o, 0, chunk)
        gt = jnp.dot(xb, wgb[...], preferred_element_type=F32) + bg_ref[0]
        issue_gather(o, chunk, 2 * chunk)
        up = jnp.dot(xb, wub[...], preferred_element_type=F32) + bu_ref[0]
        gt = jnp.minimum(gt, SWIGLU_LIMIT)
        up = jnp.clip(up, -SWIGLU_LIMIT, SWIGLU_LIMIT)
        hdn = (up + 1.0) * (gt * _sigmoid(SWIGLU_ALPHA * gt))
        issue_gather(o, 2 * chunk, 3 * chunk)
        y = jnp.dot(hdn.astype(BF16), wdb[...], preferred_element_type=F32) + bd_ref[0]
        for c in range(y.shape[1] // LANES):
            y_ref[pl.ds(c, tm, stride=ROW_SUB), :] = y[:, c * LANES:(c + 1) * LANES]
        issue_gather(o, 3 * chunk, tm)

    for s in range(nb):
        @pl.when(jnp.logical_and(i < n_used, i % nb == s))
        def _(s=s):
            step(s)

        @pl.when(jnp.logical_and(i == n_used, i % nb == s))
        def _(s=s):
            for d in range(nb - 1):
                wait_gather((s + d) % nb)
            idx_copy(i + nb - 1, (s + nb - 1) % nb).wait()

    @pl.when(i >= n_used)
    def _():
        y_ref[...] = jnp.zeros_like(y_ref)


def _moe(tile_expert, n_used, row_tok, h_all, g, w_gate, b_gate, w_up, b_up, w_down, b_down):
    n_tiles, tm = row_tok.shape[0] - MOE_BUFS, row_tok.shape[1]
    d_model = h_all.shape[1]
    d_ff = w_gate.shape[2]
    wspec = lambda shape: pl.BlockSpec((1,) + shape, lambda i, te, nu: (te[i], 0, 0))
    any_spec = pl.BlockSpec(memory_space=pl.ANY)
    grid_spec = pltpu.PrefetchScalarGridSpec(
        num_scalar_prefetch=2,
        grid=(n_tiles + 1,),
        in_specs=[
            any_spec, any_spec,
            pl.BlockSpec((1, d_model), lambda i, te, nu: (0, 0)),
            wspec((d_model, d_ff)), wspec((1, d_ff)),
            wspec((d_model, d_ff)), wspec((1, d_ff)),
            wspec((d_ff, d_model)), wspec((1, d_model)),
        ],
        out_specs=pl.BlockSpec((tm * ROW_SUB, LANES), lambda i, te, nu: (i, 0)),
        scratch_shapes=[pltpu.SMEM((tm,), jnp.int32)] * MOE_BUFS
        + [pltpu.VMEM((tm, d_model), F32)] * MOE_BUFS
        + [
            pltpu.VMEM((d_model, d_ff), BF16),
            pltpu.VMEM((d_model, d_ff), BF16),
            pltpu.VMEM((d_ff, d_model), BF16),
            pltpu.SemaphoreType.DMA((MOE_BUFS,)),
            pltpu.SemaphoreType.DMA((MOE_BUFS,)),
        ],
    )
    return pl.pallas_call(
        functools.partial(_moe_body, tm=tm),
        grid_spec=grid_spec,
        out_shape=jax.ShapeDtypeStruct(((n_tiles + 1) * tm * ROW_SUB, LANES), F32),
        compiler_params=_cparams(("arbitrary",)),
        name="moe",
    )(tile_expert, n_used, row_tok, h_all, g,
      w_gate, b_gate.reshape(N_EXPERTS, 1, d_ff), w_up, b_up.reshape(N_EXPERTS, 1, d_ff),
      w_down, b_down.reshape(N_EXPERTS, 1, d_model))


def _combine_body(dest_hbm, y_hbm, h_ref, gt_ref, o_ref, idx0, idx1, yb0, yb1, isem, gsem, *, tt):
    i = pl.program_id(0)
    n = pl.num_programs(0)
    idx = (idx0, idx1)
    ybuf = (yb0, yb1)

    def idx_copy(tile, s):
        return pltpu.make_async_copy(dest_hbm.at[tile], idx[s], isem.at[s])

    def issue_gather(s):
        for k in range(TOP_K):
            for r in range(tt):
                src = pl.multiple_of(idx[s][k * tt + r] * ROW_SUB, ROW_SUB)
                pltpu.make_async_copy(y_hbm.at[pl.ds(src, ROW_SUB)],
                                      ybuf[s].at[k, pl.ds(r * ROW_SUB, ROW_SUB)], gsem.at[s]).start()

    def wait_gather(s):
        for k in range(TOP_K):
            pltpu.make_async_copy(y_hbm.at[pl.ds(0, tt * ROW_SUB)], ybuf[s].at[k], gsem.at[s]).wait()

    @pl.when(i == 0)
    def _():
        idx_copy(0, 0).start()
        idx_copy(0, 0).wait()
        issue_gather(0)
        idx_copy(1, 1).start()

    for s in range(2):
        @pl.when(i % 2 == s)
        def _(s=s):
            o = 1 - s
            idx_copy(i + 1, o).wait()
            issue_gather(o)
            idx_copy(i + 2, s).start()
            wait_gather(s)
            gts = gt_ref[...]
            for c in range(ROW_SUB):
                cs = slice(c * LANES, (c + 1) * LANES)
                acc = h_ref[:, cs]
                for k in range(TOP_K):
                    acc = acc + gts[:, k:k + 1] * ybuf[s][k, pl.ds(c, tt, stride=ROW_SUB), :]
                o_ref[:, cs] = acc

            @pl.when(i == n - 1)
            def _():
                wait_gather(o)
                idx_copy(i + 2, s).wait()


def _combine(dest_tiles, y_rows, h, gates_tok):
    n_tiles, width = dest_tiles.shape[0] - 2, dest_tiles.shape[1]
    tt = width // TOP_K
    n_tok, d_model = h.shape
    any_spec = pl.BlockSpec(memory_space=pl.ANY)
    return pl.pallas_call(
        functools.partial(_combine_body, tt=tt),
        grid=(n_tiles,),
        in_specs=[any_spec, any_spec,
                  pl.BlockSpec((tt, d_model), lambda i: (i, 0)),
                  pl.BlockSpec((tt, 8), lambda i: (i, 0))],
        out_specs=pl.BlockSpec((tt, d_model), lambda i: (i, 0)),
        out_shape=jax.ShapeDtypeStruct((n_tok, d_model), F32),
        scratch_shapes=[
            pltpu.SMEM((TOP_K * tt,), jnp.int32),
            pltpu.SMEM((TOP_K * tt,), jnp.int32),
            pltpu.VMEM((TOP_K, tt * ROW_SUB, LANES), F32),
            pltpu.VMEM((TOP_K, tt * ROW_SUB, LANES), F32),
            pltpu.SemaphoreType.DMA((2,)),
            pltpu.SemaphoreType.DMA((2,)),
        ],
        compiler_params=_cparams(("arbitrary",)),
        name="combine",
    )(dest_tiles, y_rows, h, gates_tok)


def _block_diag_ones():
    r = lax.broadcasted_iota(jnp.int32, (NORM_BLOCK, NORM_BLOCK), 0) // HEAD_DIM
    c = lax.broadcasted_iota(jnp.int32, (NORM_BLOCK, NORM_BLOCK), 1) // HEAD_DIM
    return (r == c).astype(BF16)


def _dest_tiles(dest, tt):
    n_tok = dest.shape[1]
    tiles = dest[:TOP_K].reshape(TOP_K, n_tok // tt, tt).transpose(1, 0, 2).reshape(n_tok // tt, TOP_K * tt)
    return jnp.pad(tiles, ((0, 2), (0, 0)))


def _layer(xp, xs, ck, cv, sp, lam, lam_init, ng, w_in, qng, kng, slg, pw, ps, w_out, fg,
           rw, rb, w_gate, b_gate, w_up, b_up, w_down, b_down):
    bp, t_len, d_model = xp.shape
    bs, ts, _ = xs.shape
    past = ck.shape[1]
    pool_w = sp.shape[-1]
    n_p, n_s = bp * t_len, bs * ts

    w_in_b = w_in.astype(BF16)
    w_out_b = w_out.astype(BF16)
    pw_b = pw.astype(BF16)
    reps = NORM_BLOCK // HEAD_DIM
    qg = (jnp.tile(qng.astype(F32), reps) * (HEAD_DIM ** -0.5)).reshape(1, NORM_BLOCK)
    kg = jnp.tile(kng.astype(F32), reps).reshape(1, NORM_BLOCK)
    bd = _block_diag_ones()
    ng2 = ng.astype(F32).reshape(1, d_model)
    ps2 = ps.astype(F32).reshape(1, d_model)
    sg = (slg.astype(F32) * (1.0 - lam_init)).reshape(1, V_DIM)
    slopes = jnp.exp2(-(8.0 / N_HEADS) * jnp.arange(1, N_HEADS + 1, dtype=F32))
    lam1 = lam.reshape(1).astype(F32)
    fg2 = fg.astype(F32).reshape(1, d_model)
    rw_b = jnp.zeros((d_model, LANES), BF16).at[:, :N_EXPERTS].set(rw.astype(BF16))
    rb2 = jnp.zeros((1, LANES), F32).at[0, :N_EXPERTS].set(rb.astype(F32))

    tm = min(ROW_TILE, t_len)
    zero_pre = jnp.zeros((bp, POOL_HALO, pool_w), F32)
    qp, kp, kpb, vp, vpb, sap, gpp, utp = _inproj(
        xp, zero_pre, ng2, w_in_b, qg, kg, bd, pw_b, ps2, nseg=1, seg_len=tm, start_pos=0, carry=True)
    atp = _attn_prompt(qp, kpb, vpb, slopes, lam1, sg)
    hp, tip, tgp, cntp = _outproj(atp.reshape(n_p, d_model), sap.reshape(n_p, d_model),
                                  gpp.reshape(n_p, d_model), xp.reshape(n_p, d_model),
                                  w_out_b, fg2, rw_b, rb2)

    pre_s = jnp.concatenate([jnp.zeros((bs, POOL_HALO - sp.shape[1], pool_w), F32), sp.astype(F32)], axis=1)
    qs, ks, ksb, vs, vsb, sas, gps, uts = _inproj(
        xs.reshape(1, n_s, d_model), pre_s, ng2, w_in_b, qg, kg, bd, pw_b, ps2,
        nseg=bs, seg_len=ts, start_pos=past, carry=False)
    ats = _attn_sample(qs.reshape(bs, ts, d_model), ksb.reshape(bs, ts, d_model), vsb.reshape(bs, ts, d_model),
                       ck.reshape(bs, past * N_HEADS, V_DIM), cv.reshape(bs, past * N_HEADS, V_DIM),
                       slopes, lam1, sg)
    hs, tis, tgs, cnts = _outproj(ats.reshape(n_s, d_model), sas.reshape(n_s, d_model),
                                  gps.reshape(n_s, d_model), xs.reshape(n_s, d_model),
                                  w_out_b, fg2, rw_b, rb2)

    n_tok = n_p + n_s
    h_all = jnp.concatenate([hp, hs], axis=0)
    topi = jnp.concatenate([tip, tis], axis=1)
    gates = jnp.concatenate([tgp, tgs], axis=1)
    cnt = (cntp[:, 0] + cnts[:, 0]).astype(jnp.int32).reshape(TOP_K, N_EXPERTS)
    per_expert = jnp.sum(cnt, axis=0)
    padded = (per_expert + MOE_TILE - 1) // MOE_TILE * MOE_TILE
    pad_end = jnp.cumsum(padded)
    pad_start = pad_end - padded
    base = pad_start[None, :] + jnp.cumsum(cnt, axis=0) - cnt
    base_f = jnp.broadcast_to(base.reshape(-1, 1).astype(F32), (TOP_K * N_EXPERTS, LANES))
    tt = TOK_TILE
    tri = (lax.broadcasted_iota(jnp.int32, (tt, tt), 0) <= lax.broadcasted_iota(jnp.int32, (tt, tt), 1)).astype(BF16)
    dest = _rank(topi, base_f, tri)

    n_tiles = (n_tok * TOP_K) // MOE_TILE + N_EXPERTS
    tok_ids = jnp.tile(jnp.arange(n_tok, dtype=jnp.int32), TOP_K)
    row_tok = jnp.zeros(((n_tiles + MOE_BUFS) * MOE_TILE,), jnp.int32).at[dest[:TOP_K].reshape(-1)].set(
        tok_ids, unique_indices=True, mode='promise_in_bounds')
    n_used = (pad_end[-1] // MOE_TILE).astype(jnp.int32)
    tile_start = jnp.arange(n_tiles + 1, dtype=jnp.int32) * MOE_TILE
    last_start = jnp.maximum(pad_end[-1] - MOE_TILE, 0)
    tile_expert = jnp.minimum(
        jnp.sum((jnp.minimum(tile_start, last_start)[:, None] >= pad_end[None, :]).astype(jnp.int32), axis=1),
        N_EXPERTS - 1)
    y_rows = _moe(tile_expert, n_used.reshape(1), row_tok.reshape(n_tiles + MOE_BUFS, MOE_TILE), h_all, fg2,
                  w_gate, b_gate, w_up, b_up, w_down, b_down)

    gates_tok = jnp.transpose(gates)
    yp = _combine(_dest_tiles(dest[:, :n_p], tt), y_rows, hp, gates_tok[:n_p])
    ys = _combine(_dest_tiles(dest[:, n_p:], tt), y_rows, hs, gates_tok[n_p:])

    heads = (N_HEADS, V_DIM)
    return (yp.reshape(bp, t_len, d_model), ys.reshape(bs, ts, d_model),
            kp.reshape(bp, t_len, *heads), vp.reshape(bp, t_len, *heads), utp[:, 1:],
            ks.reshape(bs, ts, *heads), vs.reshape(bs, ts, *heads), uts[:, 1:])


def kernel(x_prompt, x_sample, cache_k, cache_v, state_pool, norm_mix_g, w_in, q_norm_g, k_norm_g,
           lambda_q1, lambda_k1, lambda_q2, lambda_k2, subln_g, pool_w, pool_scale, w_out, norm_ffn_g,
           router_w, router_b, w_gate, b_gate, w_up, b_up, w_down, b_down):
    depth = w_in.shape[0]
    hp, hs = x_prompt, x_sample
    outs = [[] for _ in range(6)]
    for layer in range(depth):
        lam_init = 0.8 - 0.6 * math.exp(-0.3 * layer)
        lam = (jnp.exp(jnp.sum(lambda_q1[layer].astype(F32) * lambda_k1[layer].astype(F32)))
               - jnp.exp(jnp.sum(lambda_q2[layer].astype(F32) * lambda_k2[layer].astype(F32)))
               + lam_init)
        hp, hs, kp, vp, up, ks, vs, us = _layer(
            hp, hs, cache_k[layer], cache_v[layer], state_pool[layer], lam, lam_init,
            norm_mix_g[layer], w_in[layer], q_norm_g[layer], k_norm_g[layer], subln_g[layer],
            pool_w[layer], pool_scale[layer], w_out[layer], norm_ffn_g[layer],
            router_w[layer], router_b[layer], w_gate[layer], b_gate[layer], w_up[layer], b_up[layer],
            w_down[layer], b_down[layer])
        for lst, val in zip(outs, (kp, vp, up, ks, vs, us)):
            lst.append(val)
    return (hp, hs) + tuple(jnp.stack(o) for o in outs)
```

```python
import functools
import math

import jax
import jax.numpy as jnp
from jax import lax
from jax.experimental import pallas as pl
from jax.experimental.pallas import tpu as pltpu

F32 = jnp.float32
BF16 = jnp.bfloat16

CHUNK = 64
N_HEADS = 8
HEAD_DIM = 64
V_DIM = 2 * HEAD_DIM
POOL_WINDOWS = (2, 4, 8, 16)
POOL_GROUP_DIM = 128
POOL_OUT_DIM = 256
POOL_HALO = 16
N_EXPERTS = 32
TOP_K = 4
SWIGLU_LIMIT = 7.0
SWIGLU_ALPHA = 1.702
RMS_EPS = 1e-6
NEG_INF = -1e30

LANES = 128
ROW_SUB = 8
NORM_BLOCK = 256
VMEM_LIMIT = 56 * 1024 * 1024

ROW_TILE = 512
ATTN_TILE = 256
MOE_TILE = 256
TOK_TILE = 256


def _sigmoid(x):
    return 1.0 / (1.0 + jnp.exp(-x))


def _cparams(sem):
    return pltpu.CompilerParams(dimension_semantics=sem, vmem_limit_bytes=VMEM_LIMIT)


def _inproj_body(x_ref, pre_ref, ng_ref, w_ref, qg_ref, kg_ref, bd_ref, pw_ref, ps_ref,
                 q_ref, k_ref, kb_ref, v_ref, vb_ref, sa_ref, gp_ref, ut_ref, ext_ref,
                 *, nseg, seg_len, start_pos, carry, attn_w, pool_w):
    rows = nseg * seg_len
    t = pl.program_id(1)
    x = x_ref[0]
    ms = jnp.mean(x * x, axis=-1, keepdims=True)
    xn = (x * lax.rsqrt(ms + RMS_EPS) * ng_ref[...]).astype(BF16)

    def proj(c0, width):
        return jnp.dot(xn, w_ref[:, c0:c0 + width], preferred_element_type=F32)

    bd = bd_ref[...]

    def group_norm(p, g_ref):
        ss = jnp.dot((p * p).astype(BF16), bd, preferred_element_type=F32)
        return p * lax.rsqrt(ss * (1.0 / HEAD_DIM) + RMS_EPS) * g_ref[...]

    nb = NORM_BLOCK
    heads_per_block = nb // V_DIM

    def store_heads(ref, val, c):
        for j in range(heads_per_block):
            head = c * heads_per_block + j
            ref[0, pl.ds(head, rows, stride=N_HEADS), :] = val[:, j * V_DIM:(j + 1) * V_DIM]

    def sink_q(val, c):
        q_ref[0, :, c * nb:(c + 1) * nb] = group_norm(val, qg_ref).astype(BF16)

    def sink_k(val, c):
        kn = group_norm(val, kg_ref)
        store_heads(k_ref, kn, c)
        kb_ref[0, :, c * nb:(c + 1) * nb] = kn.astype(BF16)

    def sink_v(val, c):
        store_heads(v_ref, val, c)
        vb_ref[0, :, c * nb:(c + 1) * nb] = val.astype(BF16)

    n_chunks = attn_w // nb
    work = [(part * attn_w + c * nb, sink, c)
            for part, sink in enumerate((sink_q, sink_k, sink_v)) for c in range(n_chunks)]
    pending = proj(work[0][0], nb)
    for j, (_, sink, c) in enumerate(work):
        cur = pending
        if j + 1 < len(work):
            pending = proj(work[j + 1][0], nb)
        sink(cur, c)

    if carry:
        @pl.when(t == 0)
        def _():
            ext_ref[:, 0:POOL_HALO, :] = pre_ref[...]
    else:
        ext_ref[:, 0:POOL_HALO, :] = pre_ref[...]
    for c in range(pool_w // nb):
        cs = slice(c * nb, (c + 1) * nb)
        u = proj(3 * attn_w + c * nb, nb)
        ext_ref[:, POOL_HALO:POOL_HALO + seg_len, cs] = u.reshape(nseg, seg_len, nb)

    ga0 = 3 * attn_w + pool_w
    d_model = attn_w
    for c in range(d_model // nb):
        cs = slice(c * nb, (c + 1) * nb)
        sa_ref[0, :, cs] = _sigmoid(proj(ga0 + c * nb, nb)).astype(BF16)

    gb0 = ga0 + d_model
    row = lax.broadcasted_iota(jnp.int32, (1, seg_len, 1), 1)
    pos = row + start_pos
    if carry:
        pos = pos + t * seg_len
    for g, w in enumerate(POOL_WINDOWS):
        cs = slice(g * POOL_GROUP_DIM, (g + 1) * POOL_GROUP_DIM)
        own = ext_ref[:, POOL_HALO:POOL_HALO + seg_len, cs]
        acc = own
        for i in range(1, w):
            acc = acc + ext_ref[:, POOL_HALO - i:POOL_HALO - i + seg_len, cs]
        inv = 1.0 / jnp.minimum(w, pos + 1).astype(F32)
        z = (acc * inv - own).reshape(rows, POOL_GROUP_DIM)
        os_ = slice(g * POOL_OUT_DIM, (g + 1) * POOL_OUT_DIM)
        yp = jnp.dot(z.astype(BF16), pw_ref[g], preferred_element_type=F32) * ps_ref[:, os_]
        gb = proj(gb0 + g * POOL_OUT_DIM, POOL_OUT_DIM)
        gp_ref[0, :, os_] = (_sigmoid(gb) * yp).astype(BF16)

    tail = ext_ref[:, seg_len:seg_len + POOL_HALO, :]
    ut_ref[...] = tail
    if carry:
        ext_ref[:, 0:POOL_HALO, :] = tail


def _inproj(x3, prefix, ng, w_in, qg, kg, bd, pw, ps, *, nseg, seg_len, start_pos, carry):
    groups, t_len, d_model = x3.shape
    rows = nseg * seg_len
    steps = t_len // rows
    in_cols = w_in.shape[1]
    pool_w = prefix.shape[-1]
    attn_w = d_model
    assert in_cols == 3 * attn_w + pool_w + 2 * d_model
    tok = lambda b, t: (b, t, 0)
    fixed2 = lambda b, t: (0, 0)
    act = lambda dt: jax.ShapeDtypeStruct((groups, t_len, d_model), dt)
    by_head = jax.ShapeDtypeStruct((groups, t_len * N_HEADS, V_DIM), F32)
    tok_spec = pl.BlockSpec((1, rows, d_model), tok)
    head_spec = pl.BlockSpec((1, rows * N_HEADS, V_DIM), tok)
    body = functools.partial(_inproj_body, nseg=nseg, seg_len=seg_len, start_pos=start_pos,
                             carry=carry, attn_w=attn_w, pool_w=pool_w)
    return pl.pallas_call(
        body,
        grid=(groups, steps),
        in_specs=[
            pl.BlockSpec((1, rows, d_model), tok),
            pl.BlockSpec((nseg, POOL_HALO, pool_w), lambda b, t: (b, 0, 0)),
            pl.BlockSpec((1, d_model), fixed2),
            pl.BlockSpec((d_model, in_cols), fixed2, pipeline_mode=pl.Buffered(1)),
            pl.BlockSpec((1, NORM_BLOCK), fixed2),
            pl.BlockSpec((1, NORM_BLOCK), fixed2),
            pl.BlockSpec((NORM_BLOCK, NORM_BLOCK), fixed2),
            pl.BlockSpec(pw.shape, lambda b, t: (0, 0, 0)),
            pl.BlockSpec((1, d_model), fixed2),
        ],
        out_specs=[tok_spec, head_spec, tok_spec, head_spec, tok_spec, tok_spec, tok_spec,
                   pl.BlockSpec((nseg, POOL_HALO, pool_w), lambda b, t: (b, 0, 0))],
        out_shape=[act(BF16), by_head, act(BF16), by_head, act(BF16), act(BF16), act(BF16),
                   jax.ShapeDtypeStruct(prefix.shape, F32)],
        scratch_shapes=[pltpu.VMEM((nseg, POOL_HALO + seg_len, pool_w), F32)],
        compiler_params=_cparams(("arbitrary", "arbitrary")),
        name="inproj",
    )(x3, prefix, ng, w_in, qg, kg, bd, pw, ps)


def _half_masks(q):
    lane = lax.broadcasted_iota(jnp.int32, q.shape, 1)
    zero = jnp.zeros_like(q)
    return jnp.where(lane < HEAD_DIM, q, zero), jnp.where(lane >= HEAD_DIM, q, zero)


def _qk(qz, kblk):
    return lax.dot_general(qz, kblk, (((1,), (1,)), ((), ())), preferred_element_type=F32)


def _subln(o, sg_ref):
    ms = jnp.mean(o * o, axis=-1, keepdims=True)
    return o * lax.rsqrt(ms + RMS_EPS) * sg_ref[...]


def _attn_body(slope_ref, lam_ref, q_ref, k_ref, v_ref, ka_ref, sg_ref, o_ref,
               kf_ref, vf_ref, s_ref, p_ref, *, tq, nq):
    h = pl.program_id(1)
    slope = slope_ref[h]
    lam = lam_ref[0]
    lane = lax.broadcasted_iota(jnp.int32, (tq, LANES), 1)
    kf_ref[:, 0:V_DIM] = k_ref[0]
    kf_ref[:, V_DIM:] = ka_ref[...]
    vf_ref[:, 0:V_DIM] = v_ref[0]
    t_len = vf_ref.shape[0]
    vf_ref[:, V_DIM:] = (lax.broadcasted_iota(jnp.int32, (t_len, LANES), 1) == 0).astype(BF16)

    row = lax.broadcasted_iota(jnp.int32, (tq, tq), 0)
    col = lax.broadcasted_iota(jnp.int32, (tq, tq), 1)
    rc = (row - col).astype(F32)
    vis = lax.shift_right_logical(col, 6) <= lax.shift_right_logical(row, 6)
    corr = jnp.where(vis, jnp.minimum(rc, 0.0) * (2.0 * slope), NEG_INF)

    for qi in range(nq):
        rows = slice(qi * tq, (qi + 1) * tq)
        nk = (qi + 1) * tq
        q1z, q2z = _half_masks(q_ref[0, rows, :])
        t = lax.broadcasted_iota(jnp.int32, (tq, LANES), 0) + qi * tq
        t_hi = lax.shift_left(lax.shift_right_logical(t, 8), 8).astype(F32)
        t_lo = jnp.bitwise_and(t, 255).astype(F32)
        qaug = jnp.where(lane == 0, -slope * t_hi,
                         jnp.where(lane == 1, -slope * t_lo,
                                   jnp.where(lane < 4, slope, 0.0))).astype(BF16)
        outs = []
        for m, qz in enumerate((q1z, q2z)):
            qa = jnp.concatenate([qz, qaug], axis=1)
            s_ref[m, :, 0:nk] = _qk(qa, kf_ref[0:nk, :])
            s_ref[m, :, nk - tq:nk] += corr
            mx = jnp.max(s_ref[m, :, 0:nk], axis=-1, keepdims=True)
            p_ref[m, :, 0:nk] = jnp.exp(s_ref[m, :, 0:nk] - mx).astype(BF16)
            outs.append(jnp.dot(p_ref[m, :, 0:nk], vf_ref[0:nk, :], preferred_element_type=F32))
        o1, o2 = outs
        c1 = 1.0 / o1[:, V_DIM:V_DIM + 1]
        c2 = lam / o2[:, V_DIM:V_DIM + 1]
        o = o1[:, 0:V_DIM] * c1 - o2[:, 0:V_DIM] * c2
        o_ref[0, rows, :] = _subln(o, sg_ref).astype(BF16)


def _attn_prompt(q, kb, vb, slopes, lam, sg):
    batch, t_len, width = q.shape
    tq = min(ATTN_TILE, t_len)
    nq = t_len // tq
    pos = jnp.arange(t_len, dtype=jnp.int32)
    ka = jnp.zeros((t_len, LANES), F32)
    ka = ka.at[:, 0:2].set(1.0).at[:, 2].set(((pos >> 8) << 8).astype(F32)).at[:, 3].set((pos & 255).astype(F32))
    smem = pl.BlockSpec(memory_space=pltpu.SMEM)
    seq = pl.BlockSpec((1, t_len, V_DIM), lambda b, h: (b, 0, h))
    return pl.pallas_call(
        functools.partial(_attn_body, tq=tq, nq=nq),
        grid=(batch, N_HEADS),
        in_specs=[
            smem, smem, seq, seq, seq,
            pl.BlockSpec((t_len, LANES), lambda b, h: (0, 0)),
            pl.BlockSpec((1, V_DIM), lambda b, h: (0, 0)),
        ],
        out_specs=seq,
        out_shape=jax.ShapeDtypeStruct((batch, t_len, width), BF16),
        scratch_shapes=[
            pltpu.VMEM((t_len, V_DIM + LANES), BF16),
            pltpu.VMEM((t_len, V_DIM + LANES), BF16),
            pltpu.VMEM((2, tq, t_len), F32),
            pltpu.VMEM((2, tq, t_len), BF16),
        ],
        compiler_params=_cparams(("arbitrary", "arbitrary")),
        name="attn_prompt",
    )(slopes, lam, q, kb, vb, ka.astype(BF16), sg)


def _attn_dec_body(slope_ref, lam_ref, q_ref, kc_ref, vc_ref, kn_ref, vn_ref, sg_ref, o_ref, *, past):
    lam = lam_ref[0]
    tq = q_ref.shape[1]
    qpos_a = lax.broadcasted_iota(jnp.int32, (tq, past), 0) + past
    kpos_a = lax.broadcasted_iota(jnp.int32, (tq, past), 1)
    qpos_b = lax.broadcasted_iota(jnp.int32, (tq, tq), 0) + past
    kpos_b = lax.broadcasted_iota(jnp.int32, (tq, tq), 1) + past

    def dist_and_vis(qpos, kpos):
        vis = lax.shift_right_logical(kpos, 6) <= lax.shift_right_logical(qpos, 6)
        return jnp.abs(qpos - kpos).astype(F32), vis

    dist_a, vis_a = dist_and_vis(qpos_a, kpos_a)
    dist_b, vis_b = dist_and_vis(qpos_b, kpos_b)

    for h in range(N_HEADS):
        cs = slice(h * V_DIM, (h + 1) * V_DIM)
        slope = slope_ref[h]
        bias_a = jnp.where(vis_a, -slope * dist_a, NEG_INF)
        bias_b = jnp.where(vis_b, -slope * dist_b, NEG_INF)
        q1z, q2z = _half_masks(q_ref[0, :, cs])
        kc = kc_ref[0, pl.ds(h, past, stride=N_HEADS), :].astype(BF16)
        vc = vc_ref[0, pl.ds(h, past, stride=N_HEADS), :].astype(BF16)
        kn = kn_ref[0, :, cs]

        def softmax_parts(qz):
            sa = _qk(qz, kc) + bias_a
            sb = _qk(qz, kn) + bias_b
            m = jnp.maximum(jnp.max(sa, axis=-1, keepdims=True), jnp.max(sb, axis=-1, keepdims=True))
            pa = jnp.exp(sa - m)
            pb = jnp.exp(sb - m)
            l = jnp.sum(pa, axis=-1, keepdims=True) + jnp.sum(pb, axis=-1, keepdims=True)
            return pa, pb, l

        pa1, pb1, l1 = softmax_parts(q1z)
        pa2, pb2, l2 = softmax_parts(q2z)
        c1 = 1.0 / l1
        c2 = lam / l2
        wa = (pa1 * c1 - pa2 * c2).astype(BF16)
        wb = (pb1 * c1 - pb2 * c2).astype(BF16)
        o = (jnp.dot(wa, vc, preferred_element_type=F32)
             + jnp.dot(wb, vn_ref[0, :, cs], preferred_element_type=F32))
        o_ref[0, :, cs] = _subln(o, sg_ref).astype(BF16)


def _attn_sample(q, kn, vn, cache_k, cache_v, slopes, lam, sg):
    batch, tq, width = q.shape
    past = cache_k.shape[1] // N_HEADS
    smem = pl.BlockSpec(memory_space=pltpu.SMEM)
    new = pl.BlockSpec((1, tq, width), lambda b: (b, 0, 0))
    old = pl.BlockSpec((1, past * N_HEADS, V_DIM), lambda b: (b, 0, 0))
    return pl.pallas_call(
        functools.partial(_attn_dec_body, past=past),
        grid=(batch,),
        in_specs=[smem, smem, new, old, old, new, new,
                  pl.BlockSpec((1, V_DIM), lambda b: (0, 0))],
        out_specs=new,
        out_shape=jax.ShapeDtypeStruct((batch, tq, width), BF16),
        compiler_params=_cparams(("arbitrary",)),
        name="attn_sample",
    )(slopes, lam, q, cache_k, cache_v, kn, vn, sg)


def _outproj_body(at_ref, sa_ref, gp_ref, x_ref, wo_ref, g_ref, rw_ref, rb_ref,
                  h_ref, ti_ref, tg_ref, cnt_ref):
    i = pl.program_id(0)
    merged = (sa_ref[...].astype(F32) * at_ref[...].astype(F32) + gp_ref[...].astype(F32)).astype(BF16)
    hh = x_ref[...] + jnp.dot(merged, wo_ref[...], preferred_element_type=F32)
    h_ref[...] = hh
    ms = jnp.mean(hh * hh, axis=-1, keepdims=True)
    hn = (hh * lax.rsqrt(ms + RMS_EPS) * g_ref[...]).astype(BF16)
    logits = jnp.dot(hn, rw_ref[...], preferred_element_type=F32) + rb_ref[...]
    lt = jnp.transpose(logits)[0:N_EXPERTS, :]
    tm = lt.shape[1]
    e_iota = lax.broadcasted_iota(jnp.int32, (N_EXPERTS, tm), 0)
    vals, idxs, hots = [], [], []
    cur = lt
    for _ in range(TOP_K):
        m = jnp.max(cur, axis=0, keepdims=True)
        idx = jnp.min(jnp.where(cur == m, e_iota, N_EXPERTS), axis=0, keepdims=True)
        hit = e_iota == idx
        vals.append(m)
        idxs.append(idx)
        hots.append(hit)
        cur = jnp.where(hit, -jnp.inf, cur)
    ex = [jnp.exp(v - vals[0]) for v in vals]
    den = ex[0] + ex[1] + ex[2] + ex[3]
    inv = 1.0 / den
    zi = jnp.zeros((8 - TOP_K, tm), jnp.int32)
    zf = jnp.zeros((8 - TOP_K, tm), F32)
    ti_ref[...] = jnp.concatenate(idxs + [zi], axis=0)
    tg_ref[...] = jnp.concatenate([e * inv for e in ex] + [zf], axis=0)
    hot = jnp.concatenate([hh_.astype(F32) for hh_ in hots], axis=0)
    csum = jnp.sum(hot, axis=1, keepdims=True)

    @pl.when(i == 0)
    def _():
        cnt_ref[...] = jnp.zeros_like(cnt_ref)

    cnt_ref[...] += jnp.broadcast_to(csum, cnt_ref.shape)


def _outproj(attn, sa, gp, x, w_out, g, rw, rb):
    n_tok, d_model = x.shape
    tm = min(ROW_TILE, n_tok)
    row = lambda i: (i, 0)
    fixed = lambda i: (0, 0)
    colb = lambda i: (0, i)
    return pl.pallas_call(
        _outproj_body,
        grid=(n_tok // tm,),
        in_specs=[
            pl.BlockSpec((tm, d_model), row),
            pl.BlockSpec((tm, d_model), row),
            pl.BlockSpec((tm, d_model), row),
            pl.BlockSpec((tm, d_model), row),
            pl.BlockSpec((d_model, d_model), fixed),
            pl.BlockSpec((1, d_model), fixed),
            pl.BlockSpec((d_model, LANES), fixed),
            pl.BlockSpec((1, LANES), fixed),
        ],
        out_specs=[
            pl.BlockSpec((tm, d_model), row),
            pl.BlockSpec((8, tm), colb),
            pl.BlockSpec((8, tm), colb),
            pl.BlockSpec((TOP_K * N_EXPERTS, LANES), fixed),
        ],
        out_shape=[
            jax.ShapeDtypeStruct((n_tok, d_model), F32),
            jax.ShapeDtypeStruct((8, n_tok), jnp.int32),
            jax.ShapeDtypeStruct((8, n_tok), F32),
            jax.ShapeDtypeStruct((TOP_K * N_EXPERTS, LANES), F32),
        ],
        compiler_params=_cparams(("arbitrary",)),
        name="outproj",
    )(attn, sa, gp, x, w_out, g, rw, rb)


def _rank_body(ti_ref, base_ref, tri_ref, dest_ref, carry_ref):
    i = pl.program_id(0)

    @pl.when(i == 0)
    def _():
        carry_ref[...] = jnp.zeros_like(carry_ref)

    tt = ti_ref.shape[1]
    e_iota = lax.broadcasted_iota(jnp.int32, (N_EXPERTS, tt), 0)
    hot = jnp.concatenate([(ti_ref[k:k + 1, :] == e_iota).astype(F32) for k in range(TOP_K)], axis=0)
    incl = jnp.dot(hot.astype(BF16), tri_ref[...], preferred_element_type=F32)
    slot = base_ref[:, 0:1] + carry_ref[:, 0:1] + incl - 1.0
    picked = hot * slot
    rows = [jnp.sum(picked[k * N_EXPERTS:(k + 1) * N_EXPERTS, :], axis=0, keepdims=True)
            for k in range(TOP_K)]
    rows.append(jnp.zeros((8 - TOP_K, tt), F32))
    dest_ref[...] = jnp.concatenate(rows, axis=0).astype(jnp.int32)
    carry_ref[...] += jnp.broadcast_to(jnp.sum(hot, axis=1, keepdims=True), carry_ref.shape)


def _rank(topi, base, tri):
    n_tok = topi.shape[1]
    tt = tri.shape[0]
    return pl.pallas_call(
        _rank_body,
        grid=(n_tok // tt,),
        in_specs=[
            pl.BlockSpec((8, tt), lambda i: (0, i)),
            pl.BlockSpec(base.shape, lambda i: (0, 0)),
            pl.BlockSpec((tt, tt), lambda i: (0, 0)),
        ],
        out_specs=pl.BlockSpec((8, tt), lambda i: (0, i)),
        out_shape=jax.ShapeDtypeStruct((8, n_tok), jnp.int32),
        scratch_shapes=[pltpu.VMEM((TOP_K * N_EXPERTS, LANES), F32)],
        compiler_params=_cparams(("arbitrary",)),
        name="rank",
    )(topi, base, tri)


def _row_copy(src_hbm, dst_buf, sem, src_row, dst_row):
    return pltpu.make_async_copy(src_hbm.at[pl.ds(src_row, 1)], dst_buf.at[pl.ds(dst_row, 1)], sem)


GATHER_CHUNKS = 4


MOE_BUFS = 3


def _moe_body(te_ref, nu_ref, rt_hbm, h_hbm, g_ref, wg_ref, bg_ref, wu_ref, bu_ref, wd_ref, bd_ref,
              y_ref, idx0, idx1, idx2, xb0, xb1, xb2, wgb, wub, wdb, isem, gsem, *, tm):
    i = pl.program_id(0)
    n_used = nu_ref[0]
    idx = (idx0, idx1, idx2)
    xbuf = (xb0, xb1, xb2)
    nb = MOE_BUFS

    def idx_copy(tile, s):
        return pltpu.make_async_copy(rt_hbm.at[tile], idx[s], isem.at[s])

    def issue_gather(s, r0, r1):
        for r in range(r0, r1):
            _row_copy(h_hbm, xbuf[s], gsem.at[s], idx[s][r], r).start()

    def wait_gather(s):
        pltpu.make_async_copy(h_hbm.at[pl.ds(0, tm)], xbuf[s], gsem.at[s]).wait()

    @pl.when(i == 0)
    def _():
        for s in range(nb - 1):
            idx_copy(s, s).start()
            idx_copy(s, s).wait()
            issue_gather(s, 0, tm)
        idx_copy(nb - 1, nb - 1).start()

    def step(s):
        o = (s + nb - 1) % nb
        wait_gather(s)
        idx_copy(i + nb - 1, o).wait()
        idx_copy(i + nb, s).start()

        prev = te_ref[jnp.maximum(i - 1, 0)]
        changed = jnp.logical_or(i == 0, te_ref[i] != prev)

        @pl.when(changed)
        def _():
            wgb[...] = wg_ref[0].astype(BF16)
            wub[...] = wu_ref[0].astype(BF16)
            wdb[...] = wd_ref[0].astype(BF16)

        chunk = tm // GATHER_CHUNKS
        x = xbuf[s][...]
        ms = jnp.mean(x * x, axis=-1, keepdims=True)
        xb = (x * lax.rsqrt(ms + RMS_EPS) * g_ref[...]).astype(BF16)
        issue_gather(o, 0, chunk)
        gt = jnp.dot(xb, wgb[...], preferred_element_type=F32) + bg_ref[0]
        issue_gather(o, chunk, 2 * chunk)
        up = jnp.dot(xb, wub[...], preferred_element_type=F32) + bu_ref[0]
        gt = jnp.minimum(gt, SWIGLU_LIMIT)
        up = jnp.clip(up, -SWIGLU_LIMIT, SWIGLU_LIMIT)
        hdn = (up + 1.0) * (gt * _sigmoid(SWIGLU_ALPHA * gt))
        issue_gather(o, 2 * chunk, tm)
        y = jnp.dot(hdn.astype(BF16), wdb[...], preferred_element_type=F32) + bd_ref[0]
        for c in range(y.shape[1] // LANES):
            y_ref[pl.ds(c, tm, stride=ROW_SUB), :] = y[:, c * LANES:(c + 1) * LANES]

    for s in range(nb):
        @pl.when(jnp.logical_and(i < n_used, i % nb == s))
        def _(s=s):
            step(s)

        @pl.when(jnp.logical_and(i == n_used, i % nb == s))
        def _(s=s):
            for d in range(nb - 1):
                wait_gather((s + d) % nb)
            idx_copy(i + nb - 1, (s + nb - 1) % nb).wait()

    @pl.when(i >= n_used)
    def _():
        y_ref[...] = jnp.zeros_like(y_ref)


def _moe(tile_expert, n_used, row_tok, h_all, g, w_gate, b_gate, w_up, b_up, w_down, b_down):
    n_tiles, tm = row_tok.shape[0] - MOE_BUFS, row_tok.shape[1]
    d_model = h_all.shape[1]
    d_ff = w_gate.shape[2]
    wspec = lambda shape: pl.BlockSpec((1,) + shape, lambda i, te, nu: (te[i], 0, 0))
    any_spec = pl.BlockSpec(memory_space=pl.ANY)
    grid_spec = pltpu.PrefetchScalarGridSpec(
        num_scalar_prefetch=2,
        grid=(n_tiles + 1,),
        in_specs=[
            any_spec, any_spec,
            pl.BlockSpec((1, d_model), lambda i, te, nu: (0, 0)),
            wspec((d_model, d_ff)), wspec((1, d_ff)),
            wspec((d_model, d_ff)), wspec((1, d_ff)),
            wspec((d_ff, d_model)), wspec((1, d_model)),
        ],
        out_specs=pl.BlockSpec((tm * ROW_SUB, LANES), lambda i, te, nu: (i, 0)),
        scratch_shapes=[pltpu.SMEM((tm,), jnp.int32)] * MOE_BUFS
        + [pltpu.VMEM((tm, d_model), F32)] * MOE_BUFS
        + [
            pltpu.VMEM((d_model, d_ff), BF16),
            pltpu.VMEM((d_model, d_ff), BF16),
            pltpu.VMEM((d_ff, d_model), BF16),
            pltpu.SemaphoreType.DMA((MOE_BUFS,)),
            pltpu.SemaphoreType.DMA((MOE_BUFS,)),
        ],
    )
    return pl.pallas_call(
        functools.partial(_moe_body, tm=tm),
        grid_spec=grid_spec,
        out_shape=jax.ShapeDtypeStruct(((n_tiles + 1) * tm * ROW_SUB, LANES), F32),
        compiler_params=_cparams(("arbitrary",)),
        name="moe",
    )(tile_expert, n_used, row_tok, h_all, g,
      w_gate, b_gate.reshape(N_EXPERTS, 1, d_ff), w_up, b_up.reshape(N_EXPERTS, 1, d_ff),
      w_down, b_down.reshape(N_EXPERTS, 1, d_model))


def _combine_body(dest_hbm, y_hbm, h_ref, gt_ref, o_ref, idx0, idx1, yb0, yb1, isem, gsem, *, tt):
    i = pl.program_id(0)
    n = pl.num_programs(0)
    idx = (idx0, idx1)
    ybuf = (yb0, yb1)

    def idx_copy(tile, s):
        return pltpu.make_async_copy(dest_hbm.at[tile], idx[s], isem.at[s])

    def issue_gather(s):
        for k in range(TOP_K):
            for r in range(tt):
                src = pl.multiple_of(idx[s][k * tt + r] * ROW_SUB, ROW_SUB)
                pltpu.make_async_copy(y_hbm.at[pl.ds(src, ROW_SUB)],
                                      ybuf[s].at[k, pl.ds(r * ROW_SUB, ROW_SUB)], gsem.at[s]).start(priority=r % 2)

    def wait_gather(s):
        for k in range(TOP_K):
            pltpu.make_async_copy(y_hbm.at[pl.ds(0, tt * ROW_SUB)], ybuf[s].at[k], gsem.at[s]).wait()

    @pl.when(i == 0)
    def _():
        idx_copy(0, 0).start()
        idx_copy(0, 0).wait()
        issue_gather(0)
        idx_copy(1, 1).start()

    for s in range(2):
        @pl.when(i % 2 == s)
        def _(s=s):
            o = 1 - s
            idx_copy(i + 1, o).wait()
            issue_gather(o)
            idx_copy(i + 2, s).start()
            wait_gather(s)
            gts = gt_ref[...]
            for c in range(ROW_SUB):
                cs = slice(c * LANES, (c + 1) * LANES)
                acc = h_ref[:, cs]
                for k in range(TOP_K):
                    acc = acc + gts[:, k:k + 1] * ybuf[s][k, pl.ds(c, tt, stride=ROW_SUB), :]
                o_ref[:, cs] = acc

            @pl.when(i == n - 1)
            def _():
                wait_gather(o)
                idx_copy(i + 2, s).wait()


def _combine(dest_tiles, y_rows, h, gates_tok):
    n_tiles, width = dest_tiles.shape[0] - 2, dest_tiles.shape[1]
    tt = width // TOP_K
    n_tok, d_model = h.shape
    any_spec = pl.BlockSpec(memory_space=pl.ANY)
    return pl.pallas_call(
        functools.partial(_combine_body, tt=tt),
        grid=(n_tiles,),
        in_specs=[any_spec, any_spec,
                  pl.BlockSpec((tt, d_model), lambda i: (i, 0)),
                  pl.BlockSpec((tt, 8), lambda i: (i, 0))],
        out_specs=pl.BlockSpec((tt, d_model), lambda i: (i, 0)),
        out_shape=jax.ShapeDtypeStruct((n_tok, d_model), F32),
        scratch_shapes=[
            pltpu.SMEM((TOP_K * tt,), jnp.int32),
            pltpu.SMEM((TOP_K * tt,), jnp.int32),
            pltpu.VMEM((TOP_K, tt * ROW_SUB, LANES), F32),
            pltpu.VMEM((TOP_K, tt * ROW_SUB, LANES), F32),
            pltpu.SemaphoreType.DMA((2,)),
            pltpu.SemaphoreType.DMA((2,)),
        ],
        compiler_params=_cparams(("arbitrary",)),
        name="combine",
    )(dest_tiles, y_rows, h, gates_tok)


def _block_diag_ones():
    r = lax.broadcasted_iota(jnp.int32, (NORM_BLOCK, NORM_BLOCK), 0) // HEAD_DIM
    c = lax.broadcasted_iota(jnp.int32, (NORM_BLOCK, NORM_BLOCK), 1) // HEAD_DIM
    return (r == c).astype(BF16)


def _dest_tiles(dest, tt):
    n_tok = dest.shape[1]
    tiles = dest[:TOP_K].reshape(TOP_K, n_tok // tt, tt).transpose(1, 0, 2).reshape(n_tok // tt, TOP_K * tt)
    return jnp.pad(tiles, ((0, 2), (0, 0)))


def _layer(xp, xs, ck, cv, sp, lam, lam_init, ng, w_in, qng, kng, slg, pw, ps, w_out, fg,
           rw, rb, w_gate, b_gate, w_up, b_up, w_down, b_down):
    bp, t_len, d_model = xp.shape
    bs, ts, _ = xs.shape
    past = ck.shape[1]
    pool_w = sp.shape[-1]
    n_p, n_s = bp * t_len, bs * ts

    w_in_b = w_in.astype(BF16)
    w_out_b = w_out.astype(BF16)
    pw_b = pw.astype(BF16)
    reps = NORM_BLOCK // HEAD_DIM
    qg = (jnp.tile(qng.astype(F32), reps) * (HEAD_DIM ** -0.5)).reshape(1, NORM_BLOCK)
    kg = jnp.tile(kng.astype(F32), reps).reshape(1, NORM_BLOCK)
    bd = _block_diag_ones()
    ng2 = ng.astype(F32).reshape(1, d_model)
    ps2 = ps.astype(F32).reshape(1, d_model)
    sg = (slg.astype(F32) * (1.0 - lam_init)).reshape(1, V_DIM)
    slopes = jnp.exp2(-(8.0 / N_HEADS) * jnp.arange(1, N_HEADS + 1, dtype=F32))
    lam1 = lam.reshape(1).astype(F32)
    fg2 = fg.astype(F32).reshape(1, d_model)
    rw_b = jnp.zeros((d_model, LANES), BF16).at[:, :N_EXPERTS].set(rw.astype(BF16))
    rb2 = jnp.zeros((1, LANES), F32).at[0, :N_EXPERTS].set(rb.astype(F32))

    tm = min(ROW_TILE, t_len)
    zero_pre = jnp.zeros((bp, POOL_HALO, pool_w), F32)
    qp, kp, kpb, vp, vpb, sap, gpp, utp = _inproj(
        xp, zero_pre, ng2, w_in_b, qg, kg, bd, pw_b, ps2, nseg=1, seg_len=tm, start_pos=0, carry=True)
    atp = _attn_prompt(qp, kpb, vpb, slopes, lam1, sg)
    hp, tip, tgp, cntp = _outproj(atp.reshape(n_p, d_model), sap.reshape(n_p, d_model),
                                  gpp.reshape(n_p, d_model), xp.reshape(n_p, d_model),
                                  w_out_b, fg2, rw_b, rb2)

    pre_s = jnp.concatenate([jnp.zeros((bs, POOL_HALO - sp.shape[1], pool_w), F32), sp.astype(F32)], axis=1)
    qs, ks, ksb, vs, vsb, sas, gps, uts = _inproj(
        xs.reshape(1, n_s, d_model), pre_s, ng2, w_in_b, qg, kg, bd, pw_b, ps2,
        nseg=bs, seg_len=ts, start_pos=past, carry=False)
    ats = _attn_sample(qs.reshape(bs, ts, d_model), ksb.reshape(bs, ts, d_model), vsb.reshape(bs, ts, d_model),
                       ck.reshape(bs, past * N_HEADS, V_DIM), cv.reshape(bs, past * N_HEADS, V_DIM),
                       slopes, lam1, sg)
    hs, tis, tgs, cnts = _outproj(ats.reshape(n_s, d_model), sas.reshape(n_s, d_model),
                                  gps.reshape(n_s, d_model), xs.reshape(n_s, d_model),
                                  w_out_b, fg2, rw_b, rb2)

    n_tok = n_p + n_s
    h_all = jnp.concatenate([hp, hs], axis=0)
    topi = jnp.concatenate([tip, tis], axis=1)
    gates = jnp.concatenate([tgp, tgs], axis=1)
    cnt = (cntp[:, 0] + cnts[:, 0]).astype(jnp.int32).reshape(TOP_K, N_EXPERTS)
    per_expert = jnp.sum(cnt, axis=0)
    padded = (per_expert + MOE_TILE - 1) // MOE_TILE * MOE_TILE
    pad_end = jnp.cumsum(padded)
    pad_start = pad_end - padded
    base = pad_start[None, :] + jnp.cumsum(cnt, axis=0) - cnt
    base_f = jnp.broadcast_to(base.reshape(-1, 1).astype(F32), (TOP_K * N_EXPERTS, LANES))
    tt = TOK_TILE
    tri = (lax.broadcasted_iota(jnp.int32, (tt, tt), 0) <= lax.broadcasted_iota(jnp.int32, (tt, tt), 1)).astype(BF16)
    dest = _rank(topi, base_f, tri)

    n_tiles = (n_tok * TOP_K) // MOE_TILE + N_EXPERTS
    tok_ids = jnp.tile(jnp.arange(n_tok, dtype=jnp.int32), TOP_K)
    row_tok = jnp.zeros(((n_tiles + MOE_BUFS) * MOE_TILE,), jnp.int32).at[dest[:TOP_K].reshape(-1)].set(
        tok_ids, unique_indices=True, mode='promise_in_bounds')
    n_used = (pad_end[-1] // MOE_TILE).astype(jnp.int32)
    tile_start = jnp.arange(n_tiles + 1, dtype=jnp.int32) * MOE_TILE
    last_start = jnp.maximum(pad_end[-1] - MOE_TILE, 0)
    tile_expert = jnp.minimum(
        jnp.sum((jnp.minimum(tile_start, last_start)[:, None] >= pad_end[None, :]).astype(jnp.int32), axis=1),
        N_EXPERTS - 1)
    y_rows = _moe(tile_expert, n_used.reshape(1), row_tok.reshape(n_tiles + MOE_BUFS, MOE_TILE), h_all, fg2,
                  w_gate, b_gate, w_up, b_up, w_down, b_down)

    gates_tok = jnp.transpose(gates)
    yp = _combine(_dest_tiles(dest[:, :n_p], tt), y_rows, hp, gates_tok[:n_p])
    ys = _combine(_dest_tiles(dest[:, n_p:], tt), y_rows, hs, gates_tok[n_p:])

    heads = (N_HEADS, V_DIM)
    return (yp.reshape(bp, t_len, d_model), ys.reshape(bs, ts, d_model),
            kp.reshape(bp, t_len, *heads), vp.reshape(bp, t_len, *heads), utp[:, 1:],
            ks.reshape(bs, ts, *heads), vs.reshape(bs, ts, *heads), uts[:, 1:])


def kernel(x_prompt, x_sample, cache_k, cache_v, state_pool, norm_mix_g, w_in, q_norm_g, k_norm_g,
           lambda_q1, lambda_k1, lambda_q2, lambda_k2, subln_g, pool_w, pool_scale, w_out, norm_ffn_g,
           router_w, router_b, w_gate, b_gate, w_up, b_up, w_down, b_down):
    depth = w_in.shape[0]
    hp, hs = x_prompt, x_sample
    outs = [[] for _ in range(6)]
    for layer in range(depth):
        lam_init = 0.8 - 0.6 * math.exp(-0.3 * layer)
        lam = (jnp.exp(jnp.sum(lambda_q1[layer].astype(F32) * lambda_k1[layer].astype(F32)))
               - jnp.exp(jnp.sum(lambda_q2[layer].astype(F32) * lambda_k2[layer].astype(F32)))
               + lam_init)
        hp, hs, kp, vp, up, ks, vs, us = _layer(
            hp, hs, cache_k[layer], cache_v[layer], state_pool[layer], lam, lam_init,
            norm_mix_g[layer], w_in[layer], q_norm_g[layer], k_norm_g[layer], subln_g[layer],
            pool_w[layer], pool_scale[layer], w_out[layer], norm_ffn_g[layer],
            router_w[layer], router_b[layer], w_gate[layer], b_gate[layer], w_up[layer], b_up[layer],
            w_down[layer], b_down[layer])
        for lst, val in zip(outs, (kp, vp, up, ks, vs, us)):
            lst.append(val)
    return (hp, hs) + tuple(jnp.stack(o) for o in outs)
```

```python
import functools
import math

import jax
import jax.numpy as jnp
from jax import lax
from jax.experimental import pallas as pl
from jax.experimental.pallas import tpu as pltpu

F32 = jnp.float32
BF16 = jnp.bfloat16

CHUNK = 64
N_HEADS = 8
HEAD_DIM = 64
V_DIM = 2 * HEAD_DIM
POOL_WINDOWS = (2, 4, 8, 16)
POOL_GROUP_DIM = 128
POOL_OUT_DIM = 256
POOL_HALO = 16
N_EXPERTS = 32
TOP_K = 4
SWIGLU_LIMIT = 7.0
SWIGLU_ALPHA = 1.702
RMS_EPS = 1e-6
NEG_INF = -1e30

LANES = 128
ROW_SUB = 8
NORM_BLOCK = 256
VMEM_LIMIT = 56 * 1024 * 1024

ROW_TILE = 512
ATTN_TILE = 256
MOE_TILE = 512
TOK_TILE = 256


def _sigmoid(x):
    return 1.0 / (1.0 + jnp.exp(-x))


def _cparams(sem):
    return pltpu.CompilerParams(dimension_semantics=sem, vmem_limit_bytes=VMEM_LIMIT)


def _store_slabs(ref, val):
    rows = val.shape[0]
    for c in range(ROW_SUB):
        ref[pl.ds(c, rows, stride=ROW_SUB), :] = val[:, c * LANES:(c + 1) * LANES]


def _load_slab_cols(ref, rows, c):
    return ref[pl.ds(c, rows, stride=ROW_SUB), :]


def _inproj_body(x_ref, pre_ref, ng_ref, w_ref, qg_ref, kg_ref, bd_ref, pw_ref, ps_ref,
                 q_ref, k_ref, kb_ref, v_ref, vb_ref, sa_ref, gp_ref, ut_ref, ext_ref,
                 *, nseg, seg_len, start_pos, carry, attn_w, pool_w):
    rows = nseg * seg_len
    t = pl.program_id(1)
    x = x_ref[0]
    ms = jnp.mean(x * x, axis=-1, keepdims=True)
    xn = (x * lax.rsqrt(ms + RMS_EPS) * ng_ref[...]).astype(BF16)

    def proj(c0, width):
        return jnp.dot(xn, w_ref[:, c0:c0 + width], preferred_element_type=F32)

    bd = bd_ref[...]

    def group_norm(p, g_ref):
        ss = jnp.dot((p * p).astype(BF16), bd, preferred_element_type=F32)
        return p * lax.rsqrt(ss * (1.0 / HEAD_DIM) + RMS_EPS) * g_ref[...]

    nb = NORM_BLOCK
    heads_per_block = nb // V_DIM

    def store_heads(ref, val, c):
        for j in range(heads_per_block):
            head = c * heads_per_block + j
            ref[0, pl.ds(head, rows, stride=N_HEADS), :] = val[:, j * V_DIM:(j + 1) * V_DIM]

    def sink_q(val, c):
        q_ref[0, :, c * nb:(c + 1) * nb] = group_norm(val, qg_ref).astype(BF16)

    def sink_k(val, c):
        kn = group_norm(val, kg_ref)
        store_heads(k_ref, kn, c)
        kb_ref[0, :, c * nb:(c + 1) * nb] = kn.astype(BF16)

    def sink_v(val, c):
        store_heads(v_ref, val, c)
        vb_ref[0, :, c * nb:(c + 1) * nb] = val.astype(BF16)

    n_chunks = attn_w // nb
    work = [(part * attn_w + c * nb, sink, c)
            for part, sink in enumerate((sink_q, sink_k, sink_v)) for c in range(n_chunks)]
    pending = proj(work[0][0], nb)
    for j, (_, sink, c) in enumerate(work):
        cur = pending
        if j + 1 < len(work):
            pending = proj(work[j + 1][0], nb)
        sink(cur, c)

    if carry:
        @pl.when(t == 0)
        def _():
            ext_ref[:, 0:POOL_HALO, :] = pre_ref[...]
    else:
        ext_ref[:, 0:POOL_HALO, :] = pre_ref[...]
    for c in range(pool_w // nb):
        cs = slice(c * nb, (c + 1) * nb)
        u = proj(3 * attn_w + c * nb, nb)
        ext_ref[:, POOL_HALO:POOL_HALO + seg_len, cs] = u.reshape(nseg, seg_len, nb)

    ga0 = 3 * attn_w + pool_w
    d_model = attn_w
    for c in range(d_model // nb):
        cs = slice(c * nb, (c + 1) * nb)
        sa_ref[0, :, cs] = _sigmoid(proj(ga0 + c * nb, nb)).astype(BF16)

    gb0 = ga0 + d_model
    row = lax.broadcasted_iota(jnp.int32, (1, seg_len, 1), 1)
    pos = row + start_pos
    if carry:
        pos = pos + t * seg_len
    for g, w in enumerate(POOL_WINDOWS):
        cs = slice(g * POOL_GROUP_DIM, (g + 1) * POOL_GROUP_DIM)
        own = ext_ref[:, POOL_HALO:POOL_HALO + seg_len, cs]
        acc = own
        for i in range(1, w):
            acc = acc + ext_ref[:, POOL_HALO - i:POOL_HALO - i + seg_len, cs]
        inv = 1.0 / jnp.minimum(w, pos + 1).astype(F32)
        z = (acc * inv - own).reshape(rows, POOL_GROUP_DIM)
        os_ = slice(g * POOL_OUT_DIM, (g + 1) * POOL_OUT_DIM)
        yp = jnp.dot(z.astype(BF16), pw_ref[g], preferred_element_type=F32) * ps_ref[:, os_]
        gb = proj(gb0 + g * POOL_OUT_DIM, POOL_OUT_DIM)
        gp_ref[0, :, os_] = (_sigmoid(gb) * yp).astype(BF16)

    tail = ext_ref[:, seg_len:seg_len + POOL_HALO, :]
    ut_ref[...] = tail
    if carry:
        ext_ref[:, 0:POOL_HALO, :] = tail


def _inproj(x3, prefix, ng, w_in, qg, kg, bd, pw, ps, *, nseg, seg_len, start_pos, carry):
    groups, t_len, d_model = x3.shape
    rows = nseg * seg_len
    steps = t_len // rows
    in_cols = w_in.shape[1]
    pool_w = prefix.shape[-1]
    attn_w = d_model
    assert in_cols == 3 * attn_w + pool_w + 2 * d_model
    tok = lambda b, t: (b, t, 0)
    fixed2 = lambda b, t: (0, 0)
    act = lambda dt: jax.ShapeDtypeStruct((groups, t_len, d_model), dt)
    by_head = jax.ShapeDtypeStruct((groups, t_len * N_HEADS, V_DIM), F32)
    tok_spec = pl.BlockSpec((1, rows, d_model), tok)
    head_spec = pl.BlockSpec((1, rows * N_HEADS, V_DIM), tok)
    body = functools.partial(_inproj_body, nseg=nseg, seg_len=seg_len, start_pos=start_pos,
                             carry=carry, attn_w=attn_w, pool_w=pool_w)
    return pl.pallas_call(
        body,
        grid=(groups, steps),
        in_specs=[
            pl.BlockSpec((1, rows, d_model), tok),
            pl.BlockSpec((nseg, POOL_HALO, pool_w), lambda b, t: (b, 0, 0)),
            pl.BlockSpec((1, d_model), fixed2),
            pl.BlockSpec((d_model, in_cols), fixed2, pipeline_mode=pl.Buffered(1)),
            pl.BlockSpec((1, NORM_BLOCK), fixed2),
            pl.BlockSpec((1, NORM_BLOCK), fixed2),
            pl.BlockSpec((NORM_BLOCK, NORM_BLOCK), fixed2),
            pl.BlockSpec(pw.shape, lambda b, t: (0, 0, 0)),
            pl.BlockSpec((1, d_model), fixed2),
        ],
        out_specs=[tok_spec, head_spec, tok_spec, head_spec, tok_spec, tok_spec, tok_spec,
                   pl.BlockSpec((nseg, POOL_HALO, pool_w), lambda b, t: (b, 0, 0))],
        out_shape=[act(BF16), by_head, act(BF16), by_head, act(BF16), act(BF16), act(BF16),
                   jax.ShapeDtypeStruct(prefix.shape, F32)],
        scratch_shapes=[pltpu.VMEM((nseg, POOL_HALO + seg_len, pool_w), F32)],
        compiler_params=_cparams(("arbitrary", "arbitrary")),
        name="inproj",
    )(x3, prefix, ng, w_in, qg, kg, bd, pw, ps)


def _half_masks(q):
    lane = lax.broadcasted_iota(jnp.int32, q.shape, 1)
    zero = jnp.zeros_like(q)
    return jnp.where(lane < HEAD_DIM, q, zero), jnp.where(lane >= HEAD_DIM, q, zero)


def _qk(qz, kblk):
    return lax.dot_general(qz, kblk, (((1,), (1,)), ((), ())), preferred_element_type=F32)


def _subln(o, sg_ref):
    ms = jnp.mean(o * o, axis=-1, keepdims=True)
    return o * lax.rsqrt(ms + RMS_EPS) * sg_ref[...]


def _attn_body(slope_ref, lam_ref, q_ref, k_ref, v_ref, ka_ref, sg_ref, o_ref,
               kf_ref, vf_ref, s_ref, p_ref, *, tq, nq):
    h = pl.program_id(1)
    slope = slope_ref[h]
    lam = lam_ref[0]
    lane = lax.broadcasted_iota(jnp.int32, (tq, LANES), 1)
    kf_ref[:, 0:V_DIM] = k_ref[0]
    kf_ref[:, V_DIM:] = ka_ref[...]
    vf_ref[:, 0:V_DIM] = v_ref[0]
    t_len = vf_ref.shape[0]
    vf_ref[:, V_DIM:] = (lax.broadcasted_iota(jnp.int32, (t_len, LANES), 1) == 0).astype(BF16)

    row = lax.broadcasted_iota(jnp.int32, (tq, tq), 0)
    col = lax.broadcasted_iota(jnp.int32, (tq, tq), 1)
    rc = (row - col).astype(F32)
    vis = lax.shift_right_logical(col, 6) <= lax.shift_right_logical(row, 6)
    corr = jnp.where(vis, jnp.minimum(rc, 0.0) * (2.0 * slope), NEG_INF)

    def scores(qi):
        nk = (qi + 1) * tq
        q1z, q2z = _half_masks(q_ref[0, qi * tq:(qi + 1) * tq, :])
        t = lax.broadcasted_iota(jnp.int32, (tq, LANES), 0) + qi * tq
        t_hi = lax.shift_left(lax.shift_right_logical(t, 8), 8).astype(F32)
        t_lo = jnp.bitwise_and(t, 255).astype(F32)
        qaug = jnp.where(lane == 0, -slope * t_hi,
                         jnp.where(lane == 1, -slope * t_lo,
                                   jnp.where(lane < 4, slope, 0.0))).astype(BF16)
        for m, qz in enumerate((q1z, q2z)):
            qa = jnp.concatenate([qz, qaug], axis=1)
            s_ref[qi % 2, m, :, 0:nk] = _qk(qa, kf_ref[0:nk, :])

    def finish(qi):
        nk = (qi + 1) * tq
        b = qi % 2
        outs = []
        for m in range(2):
            s_ref[b, m, :, nk - tq:nk] += corr
            mx = jnp.max(s_ref[b, m, :, 0:nk], axis=-1, keepdims=True)
            p_ref[m, :, 0:nk] = jnp.exp(s_ref[b, m, :, 0:nk] - mx).astype(BF16)
            outs.append(jnp.dot(p_ref[m, :, 0:nk], vf_ref[0:nk, :], preferred_element_type=F32))
        o1, o2 = outs
        c1 = 1.0 / o1[:, V_DIM:V_DIM + 1]
        c2 = lam / o2[:, V_DIM:V_DIM + 1]
        o = o1[:, 0:V_DIM] * c1 - o2[:, 0:V_DIM] * c2
        o_ref[0, qi * tq:(qi + 1) * tq, :] = _subln(o, sg_ref).astype(BF16)

    scores(0)
    for qi in range(nq):
        if qi + 1 < nq:
            scores(qi + 1)
        finish(qi)


def _attn_prompt(q, kb, vb, slopes, lam, sg):
    batch, t_len, width = q.shape
    tq = min(ATTN_TILE, t_len)
    nq = t_len // tq
    pos = jnp.arange(t_len, dtype=jnp.int32)
    ka = jnp.zeros((t_len, LANES), F32)
    ka = ka.at[:, 0:2].set(1.0).at[:, 2].set(((pos >> 8) << 8).astype(F32)).at[:, 3].set((pos & 255).astype(F32))
    smem = pl.BlockSpec(memory_space=pltpu.SMEM)
    seq = pl.BlockSpec((1, t_len, V_DIM), lambda b, h: (b, 0, h))
    return pl.pallas_call(
        functools.partial(_attn_body, tq=tq, nq=nq),
        grid=(batch, N_HEADS),
        in_specs=[
            smem, smem, seq, seq, seq,
            pl.BlockSpec((t_len, LANES), lambda b, h: (0, 0)),
            pl.BlockSpec((1, V_DIM), lambda b, h: (0, 0)),
        ],
        out_specs=seq,
        out_shape=jax.ShapeDtypeStruct((batch, t_len, width), BF16),
        scratch_shapes=[
            pltpu.VMEM((t_len, V_DIM + LANES), BF16),
            pltpu.VMEM((t_len, V_DIM + LANES), BF16),
            pltpu.VMEM((2, 2, tq, t_len), F32),
            pltpu.VMEM((2, tq, t_len), BF16),
        ],
        compiler_params=_cparams(("arbitrary", "arbitrary")),
        name="attn_prompt",
    )(slopes, lam, q, kb, vb, ka.astype(BF16), sg)


def _attn_dec_body(slope_ref, lam_ref, q_ref, kc_ref, vc_ref, kn_ref, vn_ref, sg_ref, o_ref, *, past):
    lam = lam_ref[0]
    tq = q_ref.shape[1]
    qpos_a = lax.broadcasted_iota(jnp.int32, (tq, past), 0) + past
    kpos_a = lax.broadcasted_iota(jnp.int32, (tq, past), 1)
    qpos_b = lax.broadcasted_iota(jnp.int32, (tq, tq), 0) + past
    kpos_b = lax.broadcasted_iota(jnp.int32, (tq, tq), 1) + past

    def dist_and_vis(qpos, kpos):
        vis = lax.shift_right_logical(kpos, 6) <= lax.shift_right_logical(qpos, 6)
        return jnp.abs(qpos - kpos).astype(F32), vis

    dist_a, vis_a = dist_and_vis(qpos_a, kpos_a)
    dist_b, vis_b = dist_and_vis(qpos_b, kpos_b)

    for h in range(N_HEADS):
        cs = slice(h * V_DIM, (h + 1) * V_DIM)
        slope = slope_ref[h]
        bias_a = jnp.where(vis_a, -slope * dist_a, NEG_INF)
        bias_b = jnp.where(vis_b, -slope * dist_b, NEG_INF)
        q1z, q2z = _half_masks(q_ref[0, :, cs])
        kc = kc_ref[0, pl.ds(h, past, stride=N_HEADS), :].astype(BF16)
        vc = vc_ref[0, pl.ds(h, past, stride=N_HEADS), :].astype(BF16)
        kn = kn_ref[0, :, cs]

        def softmax_parts(qz):
            sa = _qk(qz, kc) + bias_a
            sb = _qk(qz, kn) + bias_b
            m = jnp.maximum(jnp.max(sa, axis=-1, keepdims=True), jnp.max(sb, axis=-1, keepdims=True))
            pa = jnp.exp(sa - m)
            pb = jnp.exp(sb - m)
            l = jnp.sum(pa, axis=-1, keepdims=True) + jnp.sum(pb, axis=-1, keepdims=True)
            return pa, pb, l

        pa1, pb1, l1 = softmax_parts(q1z)
        pa2, pb2, l2 = softmax_parts(q2z)
        c1 = 1.0 / l1
        c2 = lam / l2
        wa = (pa1 * c1 - pa2 * c2).astype(BF16)
        wb = (pb1 * c1 - pb2 * c2).astype(BF16)
        o = (jnp.dot(wa, vc, preferred_element_type=F32)
             + jnp.dot(wb, vn_ref[0, :, cs], preferred_element_type=F32))
        o_ref[0, :, cs] = _subln(o, sg_ref).astype(BF16)


def _attn_sample(q, kn, vn, cache_k, cache_v, slopes, lam, sg):
    batch, tq, width = q.shape
    past = cache_k.shape[1] // N_HEADS
    smem = pl.BlockSpec(memory_space=pltpu.SMEM)
    new = pl.BlockSpec((1, tq, width), lambda b: (b, 0, 0))
    old = pl.BlockSpec((1, past * N_HEADS, V_DIM), lambda b: (b, 0, 0))
    return pl.pallas_call(
        functools.partial(_attn_dec_body, past=past),
        grid=(batch,),
        in_specs=[smem, smem, new, old, old, new, new,
                  pl.BlockSpec((1, V_DIM), lambda b: (0, 0))],
        out_specs=new,
        out_shape=jax.ShapeDtypeStruct((batch, tq, width), BF16),
        compiler_params=_cparams(("arbitrary",)),
        name="attn_sample",
    )(slopes, lam, q, cache_k, cache_v, kn, vn, sg)


def _outproj_body(at_ref, sa_ref, gp_ref, x_ref, wo_ref, g_ref, rw_ref, rb_ref,
                  h_ref, ti_ref, tg_ref, cnt_ref):
    i = pl.program_id(0)
    merged = (sa_ref[...].astype(F32) * at_ref[...].astype(F32) + gp_ref[...].astype(F32)).astype(BF16)
    hh = x_ref[...] + jnp.dot(merged, wo_ref[...], preferred_element_type=F32)
    _store_slabs(h_ref, hh)
    ms = jnp.mean(hh * hh, axis=-1, keepdims=True)
    hn = (hh * lax.rsqrt(ms + RMS_EPS) * g_ref[...]).astype(BF16)
    logits = jnp.dot(hn, rw_ref[...], preferred_element_type=F32) + rb_ref[...]
    lt = jnp.transpose(logits)[0:N_EXPERTS, :]
    tm = lt.shape[1]
    e_iota = lax.broadcasted_iota(jnp.int32, (N_EXPERTS, tm), 0)
    vals, idxs, hots = [], [], []
    cur = lt
    for _ in range(TOP_K):
        m = jnp.max(cur, axis=0, keepdims=True)
        idx = jnp.min(jnp.where(cur == m, e_iota, N_EXPERTS), axis=0, keepdims=True)
        hit = e_iota == idx
        vals.append(m)
        idxs.append(idx)
        hots.append(hit)
        cur = jnp.where(hit, -jnp.inf, cur)
    ex = [jnp.exp(v - vals[0]) for v in vals]
    den = ex[0] + ex[1] + ex[2] + ex[3]
    inv = 1.0 / den
    zi = jnp.zeros((8 - TOP_K, tm), jnp.int32)
    zf = jnp.zeros((8 - TOP_K, tm), F32)
    ti_ref[...] = jnp.concatenate(idxs + [zi], axis=0)
    tg_ref[...] = jnp.concatenate([e * inv for e in ex] + [zf], axis=0)
    hot = jnp.concatenate([hh_.astype(F32) for hh_ in hots], axis=0)
    csum = jnp.sum(hot, axis=1, keepdims=True)

    @pl.when(i == 0)
    def _():
        cnt_ref[...] = jnp.zeros_like(cnt_ref)

    cnt_ref[...] += jnp.broadcast_to(csum, cnt_ref.shape)


def _outproj(attn, sa, gp, x, w_out, g, rw, rb):
    n_tok, d_model = x.shape
    tm = min(ROW_TILE, n_tok)
    row = lambda i: (i, 0)
    fixed = lambda i: (0, 0)
    colb = lambda i: (0, i)
    return pl.pallas_call(
        _outproj_body,
        grid=(n_tok // tm,),
        in_specs=[
            pl.BlockSpec((tm, d_model), row),
            pl.BlockSpec((tm, d_model), row),
            pl.BlockSpec((tm, d_model), row),
            pl.BlockSpec((tm, d_model), row),
            pl.BlockSpec((d_model, d_model), fixed),
            pl.BlockSpec((1, d_model), fixed),
            pl.BlockSpec((d_model, LANES), fixed),
            pl.BlockSpec((1, LANES), fixed),
        ],
        out_specs=[
            pl.BlockSpec((tm * ROW_SUB, LANES), row),
            pl.BlockSpec((8, tm), colb),
            pl.BlockSpec((8, tm), colb),
            pl.BlockSpec((TOP_K * N_EXPERTS, LANES), fixed),
        ],
        out_shape=[
            jax.ShapeDtypeStruct((n_tok * ROW_SUB, LANES), F32),
            jax.ShapeDtypeStruct((8, n_tok), jnp.int32),
            jax.ShapeDtypeStruct((8, n_tok), F32),
            jax.ShapeDtypeStruct((TOP_K * N_EXPERTS, LANES), F32),
        ],
        compiler_params=_cparams(("arbitrary",)),
        name="outproj",
    )(attn, sa, gp, x, w_out, g, rw, rb)


def _rank_body(ti_ref, base_ref, tri_ref, dest_ref, carry_ref):
    i = pl.program_id(0)

    @pl.when(i == 0)
    def _():
        carry_ref[...] = jnp.zeros_like(carry_ref)

    tt = ti_ref.shape[1]
    e_iota = lax.broadcasted_iota(jnp.int32, (N_EXPERTS, tt), 0)
    hot = jnp.concatenate([(ti_ref[k:k + 1, :] == e_iota).astype(F32) for k in range(TOP_K)], axis=0)
    incl = jnp.dot(hot.astype(BF16), tri_ref[...], preferred_element_type=F32)
    slot = base_ref[:, 0:1] + carry_ref[:, 0:1] + incl - 1.0
    picked = hot * slot
    rows = [jnp.sum(picked[k * N_EXPERTS:(k + 1) * N_EXPERTS, :], axis=0, keepdims=True)
            for k in range(TOP_K)]
    rows.append(jnp.zeros((8 - TOP_K, tt), F32))
    dest_ref[...] = jnp.concatenate(rows, axis=0).astype(jnp.int32)
    carry_ref[...] += jnp.broadcast_to(jnp.sum(hot, axis=1, keepdims=True), carry_ref.shape)


def _rank(topi, base, tri):
    n_tok = topi.shape[1]
    tt = tri.shape[0]
    return pl.pallas_call(
        _rank_body,
        grid=(n_tok // tt,),
        in_specs=[
            pl.BlockSpec((8, tt), lambda i: (0, i)),
            pl.BlockSpec(base.shape, lambda i: (0, 0)),
            pl.BlockSpec((tt, tt), lambda i: (0, 0)),
        ],
        out_specs=pl.BlockSpec((8, tt), lambda i: (0, i)),
        out_shape=jax.ShapeDtypeStruct((8, n_tok), jnp.int32),
        scratch_shapes=[pltpu.VMEM((TOP_K * N_EXPERTS, LANES), F32)],
        compiler_params=_cparams(("arbitrary",)),
        name="rank",
    )(topi, base, tri)


def _slab_copy(src, src_row, dst, dst_row, sem):
    def first_sublane(row):
        return row * ROW_SUB if isinstance(row, int) else pl.multiple_of(row * ROW_SUB, ROW_SUB)

    return pltpu.make_async_copy(src.at[pl.ds(first_sublane(src_row), ROW_SUB)],
                                 dst.at[pl.ds(first_sublane(dst_row), ROW_SUB)], sem)


def _dispatch_body(zs_ref, zl_ref, dest_hbm, hp_hbm, hs_hbm, xs_hbm, idx0, idx1, zbuf, isem, dsem, zsem,
                   *, tt, n_prompt_tiles):
    i = pl.program_id(0)
    n = pl.num_programs(0)
    idx = (idx0, idx1)

    def idx_copy(tile, s):
        return pltpu.make_async_copy(dest_hbm.at[tile], idx[s], isem.at[s])

    def wait_rows(s):
        for _ in range(TOP_K):
            pltpu.make_async_copy(hp_hbm.at[pl.ds(0, tt * ROW_SUB)], xs_hbm.at[pl.ds(0, tt * ROW_SUB)],
                                  dsem.at[s]).wait()

    @pl.when(i == 0)
    def _():
        idx_copy(0, 0).start()
        idx_copy(1, 1).start()
        zbuf[...] = jnp.zeros_like(zbuf)

        def per_expert(e, c):
            def one(r, c2):
                _slab_copy(zbuf, 0, xs_hbm, zs_ref[e] + r, zsem).start()
                return c2
            lax.fori_loop(0, zl_ref[e], one, 0)

            def one_wait(r, c2):
                _slab_copy(zbuf, 0, xs_hbm, zs_ref[e] + r, zsem).wait()
                return c2
            lax.fori_loop(0, zl_ref[e], one_wait, 0)
            return c
        lax.fori_loop(0, N_EXPERTS, per_expert, 0)

    def issue(src_hbm, tile, s):
        base = tile * tt
        for k in range(TOP_K):
            for r in range(tt):
                _slab_copy(src_hbm, base + r, xs_hbm, idx[s][k * tt + r], dsem.at[s]).start(priority=r % 2)

    for s in range(2):
        @pl.when(i % 2 == s)
        def _(s=s):
            o = 1 - s
            idx_copy(i, s).wait()

            @pl.when(i < n_prompt_tiles)
            def _():
                issue(hp_hbm, i, s)

            @pl.when(i >= n_prompt_tiles)
            def _():
                issue(hs_hbm, i - n_prompt_tiles, s)

            idx_copy(i + 2, s).start()

            @pl.when(i > 0)
            def _():
                wait_rows(o)

            @pl.when(i == n - 1)
            def _():
                wait_rows(s)
                idx_copy(i + 1, o).wait()
                idx_copy(i + 2, s).wait()


def _dispatch(zero_start, zero_len, dest_tiles, hp, hs, n_rows):
    n_tiles, width = dest_tiles.shape[0] - 2, dest_tiles.shape[1]
    tt = width // TOP_K
    n_prompt_tiles = hp.shape[0] // (tt * ROW_SUB)
    any_spec = pl.BlockSpec(memory_space=pl.ANY)
    grid_spec = pltpu.PrefetchScalarGridSpec(
        num_scalar_prefetch=2,
        grid=(n_tiles,),
        in_specs=[any_spec, any_spec, any_spec],
        out_specs=any_spec,
        scratch_shapes=[
            pltpu.SMEM((width,), jnp.int32),
            pltpu.SMEM((width,), jnp.int32),
            pltpu.VMEM((ROW_SUB, LANES), F32),
            pltpu.SemaphoreType.DMA((2,)),
            pltpu.SemaphoreType.DMA((2,)),
            pltpu.SemaphoreType.DMA,
        ],
    )
    return pl.pallas_call(
        functools.partial(_dispatch_body, tt=tt, n_prompt_tiles=n_prompt_tiles),
        grid_spec=grid_spec,
        out_shape=jax.ShapeDtypeStruct((n_rows * ROW_SUB, LANES), F32),
        compiler_params=_cparams(("arbitrary",)),
        name="dispatch",
    )(zero_start, zero_len, dest_tiles, hp, hs)


def _moe_body(te_ref, nu_ref, x_ref, g_ref, wg_ref, bg_ref, wu_ref, bu_ref, wd_ref, bd_ref,
              y_ref, wgb, wub, wdb, *, tm):
    i = pl.program_id(0)
    n_used = nu_ref[0]

    @pl.when(i >= n_used)
    def _():
        y_ref[...] = jnp.zeros_like(y_ref)

    @pl.when(i < n_used)
    def _():
        prev = te_ref[jnp.maximum(i - 1, 0)]
        changed = jnp.logical_or(i == 0, te_ref[i] != prev)

        @pl.when(changed)
        def _():
            wgb[...] = wg_ref[0].astype(BF16)
            wub[...] = wu_ref[0].astype(BF16)
            wdb[...] = wd_ref[0].astype(BF16)

        half = tm // 2
        d_model = ROW_SUB * LANES

        def normed_half(r0):
            cols = [x_ref[pl.ds(r0 * ROW_SUB + c, half, stride=ROW_SUB), :] for c in range(ROW_SUB)]
            ssq = cols[0] * cols[0]
            for xc in cols[1:]:
                ssq = ssq + xc * xc
            r = lax.rsqrt(jnp.sum(ssq, axis=-1, keepdims=True) * (1.0 / d_model) + RMS_EPS)
            return jnp.concatenate(
                [(xc * r * g_ref[:, c * LANES:(c + 1) * LANES]).astype(BF16) for c, xc in enumerate(cols)], axis=1)

        def gate_up(xb):
            gt = jnp.dot(xb, wgb[...], preferred_element_type=F32) + bg_ref[0]
            up = jnp.dot(xb, wub[...], preferred_element_type=F32) + bu_ref[0]
            return gt, up

        def down(gt, up, r0):
            gt = jnp.minimum(gt, SWIGLU_LIMIT)
            up = jnp.clip(up, -SWIGLU_LIMIT, SWIGLU_LIMIT)
            hdn = (up + 1.0) * (gt * _sigmoid(SWIGLU_ALPHA * gt))
            y = jnp.dot(hdn.astype(BF16), wdb[...], preferred_element_type=F32) + bd_ref[0]
            for c in range(ROW_SUB):
                y_ref[pl.ds(r0 * ROW_SUB + c, half, stride=ROW_SUB), :] = y[:, c * LANES:(c + 1) * LANES]

        ga, ua = gate_up(normed_half(0))
        gb, ub = gate_up(normed_half(half))
        down(ga, ua, 0)
        down(gb, ub, half)


def _moe(tile_expert, n_used, x_rows, g, w_gate, b_gate, w_up, b_up, w_down, b_down, *, tm):
    n_tiles = x_rows.shape[0] // (tm * ROW_SUB)
    d_model = ROW_SUB * LANES
    d_ff = w_gate.shape[2]
    assert w_gate.shape[1] == d_model
    wspec = lambda shape: pl.BlockSpec((1,) + shape, lambda i, te, nu: (te[i], 0, 0))
    grid_spec = pltpu.PrefetchScalarGridSpec(
        num_scalar_prefetch=2,
        grid=(n_tiles,),
        in_specs=[
            pl.BlockSpec((tm * ROW_SUB, LANES), lambda i, te, nu: (jnp.minimum(i, nu[0] - 1), 0)),
            pl.BlockSpec((1, d_model), lambda i, te, nu: (0, 0)),
            wspec((d_model, d_ff)), wspec((1, d_ff)),
            wspec((d_model, d_ff)), wspec((1, d_ff)),
            wspec((d_ff, d_model)), wspec((1, d_model)),
        ],
        out_specs=pl.BlockSpec((tm * ROW_SUB, LANES), lambda i, te, nu: (i, 0)),
        scratch_shapes=[
            pltpu.VMEM((d_model, d_ff), BF16),
            pltpu.VMEM((d_model, d_ff), BF16),
            pltpu.VMEM((d_ff, d_model), BF16),
        ],
    )
    return pl.pallas_call(
        functools.partial(_moe_body, tm=tm),
        grid_spec=grid_spec,
        out_shape=jax.ShapeDtypeStruct((n_tiles * tm * ROW_SUB, LANES), F32),
        compiler_params=_cparams(("arbitrary",)),
        name="moe",
    )(tile_expert, n_used, x_rows, g,
      w_gate, b_gate.reshape(N_EXPERTS, 1, d_ff), w_up, b_up.reshape(N_EXPERTS, 1, d_ff),
      w_down, b_down.reshape(N_EXPERTS, 1, d_model))


def _combine_body(dest_hbm, y_hbm, h_ref, gt_ref, o_ref, idx0, idx1, yb0, yb1, isem, gsem, *, tt):
    i = pl.program_id(0)
    n = pl.num_programs(0)
    idx = (idx0, idx1)
    ybuf = (yb0, yb1)

    def idx_copy(tile, s):
        return pltpu.make_async_copy(dest_hbm.at[tile], idx[s], isem.at[s])

    def issue_gather(s):
        for k in range(TOP_K):
            for r in range(tt):
                src = pl.multiple_of(idx[s][k * tt + r] * ROW_SUB, ROW_SUB)
                pltpu.make_async_copy(y_hbm.at[pl.ds(src, ROW_SUB)],
                                      ybuf[s].at[k, pl.ds(r * ROW_SUB, ROW_SUB)], gsem.at[s]).start(priority=r % 2)

    def wait_gather(s):
        for k in range(TOP_K):
            pltpu.make_async_copy(y_hbm.at[pl.ds(0, tt * ROW_SUB)], ybuf[s].at[k], gsem.at[s]).wait()

    @pl.when(i == 0)
    def _():
        idx_copy(0, 0).start()
        idx_copy(0, 0).wait()
        issue_gather(0)
        idx_copy(1, 1).start()

    for s in range(2):
        @pl.when(i % 2 == s)
        def _(s=s):
            o = 1 - s
            idx_copy(i + 1, o).wait()
            issue_gather(o)
            idx_copy(i + 2, s).start()
            wait_gather(s)
            gts = gt_ref[...]
            for c in range(ROW_SUB):
                cs = slice(c * LANES, (c + 1) * LANES)
                acc = _load_slab_cols(h_ref, tt, c)
                for k in range(TOP_K):
                    acc = acc + gts[:, k:k + 1] * ybuf[s][k, pl.ds(c, tt, stride=ROW_SUB), :]
                o_ref[:, cs] = acc

            @pl.when(i == n - 1)
            def _():
                wait_gather(o)
                idx_copy(i + 2, s).wait()


def _combine(dest_tiles, y_rows, h, gates_tok):
    n_tiles, width = dest_tiles.shape[0] - 2, dest_tiles.shape[1]
    tt = width // TOP_K
    n_tok, d_model = h.shape[0] // ROW_SUB, ROW_SUB * LANES
    any_spec = pl.BlockSpec(memory_space=pl.ANY)
    return pl.pallas_call(
        functools.partial(_combine_body, tt=tt),
        grid=(n_tiles,),
        in_specs=[any_spec, any_spec,
                  pl.BlockSpec((tt * ROW_SUB, LANES), lambda i: (i, 0)),
                  pl.BlockSpec((tt, 8), lambda i: (i, 0))],
        out_specs=pl.BlockSpec((tt, d_model), lambda i: (i, 0)),
        out_shape=jax.ShapeDtypeStruct((n_tok, d_model), F32),
        scratch_shapes=[
            pltpu.SMEM((TOP_K * tt,), jnp.int32),
            pltpu.SMEM((TOP_K * tt,), jnp.int32),
            pltpu.VMEM((TOP_K, tt * ROW_SUB, LANES), F32),
            pltpu.VMEM((TOP_K, tt * ROW_SUB, LANES), F32),
            pltpu.SemaphoreType.DMA((2,)),
            pltpu.SemaphoreType.DMA((2,)),
        ],
        compiler_params=_cparams(("arbitrary",)),
        name="combine",
    )(dest_tiles, y_rows, h, gates_tok)


def _block_diag_ones():
    r = lax.broadcasted_iota(jnp.int32, (NORM_BLOCK, NORM_BLOCK), 0) // HEAD_DIM
    c = lax.broadcasted_iota(jnp.int32, (NORM_BLOCK, NORM_BLOCK), 1) // HEAD_DIM
    return (r == c).astype(BF16)


def _dest_tiles(dest, tt):
    n_tok = dest.shape[1]
    tiles = dest[:TOP_K].reshape(TOP_K, n_tok // tt, tt).transpose(1, 0, 2).reshape(n_tok // tt, TOP_K * tt)
    return jnp.pad(tiles, ((0, 2), (0, 0)))


def _layer(xp, xs, ck, cv, sp, lam, lam_init, ng, w_in, qng, kng, slg, pw, ps, w_out, fg,
           rw, rb, w_gate, b_gate, w_up, b_up, w_down, b_down):
    bp, t_len, d_model = xp.shape
    bs, ts, _ = xs.shape
    past = ck.shape[1]
    pool_w = sp.shape[-1]
    n_p, n_s = bp * t_len, bs * ts

    w_in_b = w_in.astype(BF16)
    w_out_b = w_out.astype(BF16)
    pw_b = pw.astype(BF16)
    reps = NORM_BLOCK // HEAD_DIM
    qg = (jnp.tile(qng.astype(F32), reps) * (HEAD_DIM ** -0.5)).reshape(1, NORM_BLOCK)
    kg = jnp.tile(kng.astype(F32), reps).reshape(1, NORM_BLOCK)
    bd = _block_diag_ones()
    ng2 = ng.astype(F32).reshape(1, d_model)
    ps2 = ps.astype(F32).reshape(1, d_model)
    sg = (slg.astype(F32) * (1.0 - lam_init)).reshape(1, V_DIM)
    slopes = jnp.exp2(-(8.0 / N_HEADS) * jnp.arange(1, N_HEADS + 1, dtype=F32))
    lam1 = lam.reshape(1).astype(F32)
    fg2 = fg.astype(F32).reshape(1, d_model)
    rw_b = jnp.zeros((d_model, LANES), BF16).at[:, :N_EXPERTS].set(rw.astype(BF16))
    rb2 = jnp.zeros((1, LANES), F32).at[0, :N_EXPERTS].set(rb.astype(F32))

    tm = min(ROW_TILE, t_len)
    zero_pre = jnp.zeros((bp, POOL_HALO, pool_w), F32)
    qp, kp, kpb, vp, vpb, sap, gpp, utp = _inproj(
        xp, zero_pre, ng2, w_in_b, qg, kg, bd, pw_b, ps2, nseg=1, seg_len=tm, start_pos=0, carry=True)
    atp = _attn_prompt(qp, kpb, vpb, slopes, lam1, sg)
    hp, tip, tgp, cntp = _outproj(atp.reshape(n_p, d_model), sap.reshape(n_p, d_model),
                                  gpp.reshape(n_p, d_model), xp.reshape(n_p, d_model),
                                  w_out_b, fg2, rw_b, rb2)

    pre_s = jnp.concatenate([jnp.zeros((bs, POOL_HALO - sp.shape[1], pool_w), F32), sp.astype(F32)], axis=1)
    qs, ks, ksb, vs, vsb, sas, gps, uts = _inproj(
        xs.reshape(1, n_s, d_model), pre_s, ng2, w_in_b, qg, kg, bd, pw_b, ps2,
        nseg=bs, seg_len=ts, start_pos=past, carry=False)
    ats = _attn_sample(qs.reshape(bs, ts, d_model), ksb.reshape(bs, ts, d_model), vsb.reshape(bs, ts, d_model),
                       ck.reshape(bs, past * N_HEADS, V_DIM), cv.reshape(bs, past * N_HEADS, V_DIM),
                       slopes, lam1, sg)
    hs, tis, tgs, cnts = _outproj(ats.reshape(n_s, d_model), sas.reshape(n_s, d_model),
                                  gps.reshape(n_s, d_model), xs.reshape(n_s, d_model),
                                  w_out_b, fg2, rw_b, rb2)

    n_tok = n_p + n_s
    topi = jnp.concatenate([tip, tis], axis=1)
    gates = jnp.concatenate([tgp, tgs], axis=1)
    cnt = (cntp[:, 0] + cnts[:, 0]).astype(jnp.int32).reshape(TOP_K, N_EXPERTS)
    per_expert = jnp.sum(cnt, axis=0)
    padded = (per_expert + MOE_TILE - 1) // MOE_TILE * MOE_TILE
    pad_end = jnp.cumsum(padded)
    pad_start = pad_end - padded
    base = pad_start[None, :] + jnp.cumsum(cnt, axis=0) - cnt
    base_f = jnp.broadcast_to(base.reshape(-1, 1).astype(F32), (TOP_K * N_EXPERTS, LANES))
    tt = TOK_TILE
    tri = (lax.broadcasted_iota(jnp.int32, (tt, tt), 0) <= lax.broadcasted_iota(jnp.int32, (tt, tt), 1)).astype(BF16)
    dest = _rank(topi, base_f, tri)

    n_tiles = -(-(n_tok * TOP_K) // MOE_TILE) + N_EXPERTS
    n_used = (pad_end[-1] // MOE_TILE).astype(jnp.int32)
    tile_start = jnp.arange(n_tiles, dtype=jnp.int32) * MOE_TILE
    last_start = jnp.maximum(pad_end[-1] - MOE_TILE, 0)
    tile_expert = jnp.minimum(
        jnp.sum((jnp.minimum(tile_start, last_start)[:, None] >= pad_end[None, :]).astype(jnp.int32), axis=1),
        N_EXPERTS - 1)
    dest_p = _dest_tiles(dest[:, :n_p], tt)
    dest_s = _dest_tiles(dest[:, n_p:], tt)
    dest_all = jnp.concatenate([dest_p[:-2], dest_s], axis=0)
    x_rows = _dispatch(pad_start + per_expert, padded - per_expert, dest_all, hp, hs, n_tiles * MOE_TILE)
    y_rows = _moe(tile_expert, n_used.reshape(1), x_rows, fg2,
                  w_gate, b_gate, w_up, b_up, w_down, b_down, tm=MOE_TILE)

    gates_tok = jnp.transpose(gates)
    yp = _combine(dest_p, y_rows, hp, gates_tok[:n_p])
    ys = _combine(dest_s, y_rows, hs, gates_tok[n_p:])

    heads = (N_HEADS, V_DIM)
    return (yp.reshape(bp, t_len, d_model), ys.reshape(bs, ts, d_model),
            kp.reshape(bp, t_len, *heads), vp.reshape(bp, t_len, *heads), utp[:, 1:],
            ks.reshape(bs, ts, *heads), vs.reshape(bs, ts, *heads), uts[:, 1:])


def kernel(x_prompt, x_sample, cache_k, cache_v, state_pool, norm_mix_g, w_in, q_norm_g, k_norm_g,
           lambda_q1, lambda_k1, lambda_q2, lambda_k2, subln_g, pool_w, pool_scale, w_out, norm_ffn_g,
           router_w, router_b, w_gate, b_gate, w_up, b_up, w_down, b_down):
    depth = w_in.shape[0]
    hp, hs = x_prompt, x_sample
    outs = [[] for _ in range(6)]
    for layer in range(depth):
        lam_init = 0.8 - 0.6 * math.exp(-0.3 * layer)
        lam = (jnp.exp(jnp.sum(lambda_q1[layer].astype(F32) * lambda_k1[layer].astype(F32)))
               - jnp.exp(jnp.sum(lambda_q2[layer].astype(F32) * lambda_k2[layer].astype(F32)))
               + lam_init)
        hp, hs, kp, vp, up, ks, vs, us = _layer(
            hp, hs, cache_k[layer], cache_v[layer], state_pool[layer], lam, lam_init,
            norm_mix_g[layer], w_in[layer], q_norm_g[layer], k_norm_g[layer], subln_g[layer],
            pool_w[layer], pool_scale[layer], w_out[layer], norm_ffn_g[layer],
            router_w[layer], router_b[layer], w_gate[layer], b_gate[layer], w_up[layer], b_up[layer],
            w_down[layer], b_down[layer])
        for lst, val in zip(outs, (kp, vp, up, ks, vs, us)):
            lst.append(val)
    return (hp, hs) + tuple(jnp.stack(o) for o in outs)
```

```python
import functools
import math

import jax
import jax.numpy as jnp
from jax import lax
from jax.experimental import pallas as pl
from jax.experimental.pallas import tpu as pltpu

F32 = jnp.float32
BF16 = jnp.bfloat16

CHUNK = 64
N_HEADS = 8
HEAD_DIM = 64
V_DIM = 2 * HEAD_DIM
POOL_WINDOWS = (2, 4, 8, 16)
POOL_GROUP_DIM = 128
POOL_OUT_DIM = 256
POOL_HALO = 16
N_EXPERTS = 32
TOP_K = 4
SWIGLU_LIMIT = 7.0
SWIGLU_ALPHA = 1.702
RMS_EPS = 1e-6
NEG_INF = -1e30

LANES = 128
ROW_SUB = 8
NORM_BLOCK = 256
VMEM_LIMIT = 56 * 1024 * 1024

ROW_TILE = 512
ATTN_TILE = 256
MOE_TILE = 512
TOK_TILE = 256


def _sigmoid(x):
    return 1.0 / (1.0 + jnp.exp(-x))


def _cparams(sem):
    return pltpu.CompilerParams(dimension_semantics=sem, vmem_limit_bytes=VMEM_LIMIT)


def _store_slabs(ref, val):
    rows = val.shape[0]
    for c in range(ROW_SUB):
        ref[pl.ds(c, rows, stride=ROW_SUB), :] = val[:, c * LANES:(c + 1) * LANES]


def _load_slab_cols(ref, rows, c):
    return ref[pl.ds(c, rows, stride=ROW_SUB), :]


def _inproj_body(x_ref, pre_ref, ng_ref, w_ref, qg_ref, kg_ref, bd_ref, pw_ref, ps_ref,
                 q_ref, k_ref, kb_ref, v_ref, vb_ref, sa_ref, gp_ref, ut_ref, ext_ref,
                 *, nseg, seg_len, start_pos, carry, attn_w, pool_w):
    rows = nseg * seg_len
    t = pl.program_id(1)
    x = x_ref[0]
    ms = jnp.mean(x * x, axis=-1, keepdims=True)
    xn = (x * lax.rsqrt(ms + RMS_EPS) * ng_ref[...]).astype(BF16)

    def proj(c0, width):
        return jnp.dot(xn, w_ref[:, c0:c0 + width], preferred_element_type=F32)

    bd = bd_ref[...]

    def group_norm(p, g_ref):
        ss = jnp.dot((p * p).astype(BF16), bd, preferred_element_type=F32)
        return p * lax.rsqrt(ss * (1.0 / HEAD_DIM) + RMS_EPS) * g_ref[...]

    nb = NORM_BLOCK
    heads_per_block = nb // V_DIM

    def store_heads(ref, val, c):
        for j in range(heads_per_block):
            head = c * heads_per_block + j
            ref[0, pl.ds(head, rows, stride=N_HEADS), :] = val[:, j * V_DIM:(j + 1) * V_DIM]

    def sink_q(val, c):
        q_ref[0, :, c * nb:(c + 1) * nb] = group_norm(val, qg_ref).astype(BF16)

    def sink_k(val, c):
        kn = group_norm(val, kg_ref)
        store_heads(k_ref, kn, c)
        kb_ref[0, :, c * nb:(c + 1) * nb] = kn.astype(BF16)

    def sink_v(val, c):
        store_heads(v_ref, val, c)
        vb_ref[0, :, c * nb:(c + 1) * nb] = val.astype(BF16)

    n_chunks = attn_w // nb
    work = [(part * attn_w + c * nb, sink, c)
            for part, sink in enumerate((sink_q, sink_k, sink_v)) for c in range(n_chunks)]
    pending = proj(work[0][0], nb)
    for j, (_, sink, c) in enumerate(work):
        cur = pending
        if j + 1 < len(work):
            pending = proj(work[j + 1][0], nb)
        sink(cur, c)

    if carry:
        @pl.when(t == 0)
        def _():
            ext_ref[:, 0:POOL_HALO, :] = pre_ref[...]
    else:
        ext_ref[:, 0:POOL_HALO, :] = pre_ref[...]
    for c in range(pool_w // nb):
        cs = slice(c * nb, (c + 1) * nb)
        u = proj(3 * attn_w + c * nb, nb)
        ext_ref[:, POOL_HALO:POOL_HALO + seg_len, cs] = u.reshape(nseg, seg_len, nb)

    ga0 = 3 * attn_w + pool_w
    d_model = attn_w
    for c in range(d_model // nb):
        cs = slice(c * nb, (c + 1) * nb)
        sa_ref[0, :, cs] = _sigmoid(proj(ga0 + c * nb, nb)).astype(BF16)

    gb0 = ga0 + d_model
    row = lax.broadcasted_iota(jnp.int32, (1, seg_len, 1), 1)
    pos = row + start_pos
    if carry:
        pos = pos + t * seg_len
    for g, w in enumerate(POOL_WINDOWS):
        cs = slice(g * POOL_GROUP_DIM, (g + 1) * POOL_GROUP_DIM)
        own = ext_ref[:, POOL_HALO:POOL_HALO + seg_len, cs]
        acc = own
        for i in range(1, w):
            acc = acc + ext_ref[:, POOL_HALO - i:POOL_HALO - i + seg_len, cs]
        inv = 1.0 / jnp.minimum(w, pos + 1).astype(F32)
        z = (acc * inv - own).reshape(rows, POOL_GROUP_DIM)
        os_ = slice(g * POOL_OUT_DIM, (g + 1) * POOL_OUT_DIM)
        yp = jnp.dot(z.astype(BF16), pw_ref[g], preferred_element_type=F32) * ps_ref[:, os_]
        gb = proj(gb0 + g * POOL_OUT_DIM, POOL_OUT_DIM)
        gp_ref[0, :, os_] = (_sigmoid(gb) * yp).astype(BF16)

    tail = ext_ref[:, seg_len:seg_len + POOL_HALO, :]
    ut_ref[...] = tail
    if carry:
        ext_ref[:, 0:POOL_HALO, :] = tail


def _inproj(x3, prefix, ng, w_in, qg, kg, bd, pw, ps, *, nseg, seg_len, start_pos, carry):
    groups, t_len, d_model = x3.shape
    rows = nseg * seg_len
    steps = t_len // rows
    in_cols = w_in.shape[1]
    pool_w = prefix.shape[-1]
    attn_w = d_model
    assert in_cols == 3 * attn_w + pool_w + 2 * d_model
    tok = lambda b, t: (b, t, 0)
    fixed2 = lambda b, t: (0, 0)
    act = lambda dt: jax.ShapeDtypeStruct((groups, t_len, d_model), dt)
    by_head = jax.ShapeDtypeStruct((groups, t_len * N_HEADS, V_DIM), F32)
    tok_spec = pl.BlockSpec((1, rows, d_model), tok)
    head_spec = pl.BlockSpec((1, rows * N_HEADS, V_DIM), tok)
    body = functools.partial(_inproj_body, nseg=nseg, seg_len=seg_len, start_pos=start_pos,
                             carry=carry, attn_w=attn_w, pool_w=pool_w)
    return pl.pallas_call(
        body,
        grid=(groups, steps),
        in_specs=[
            pl.BlockSpec((1, rows, d_model), tok),
            pl.BlockSpec((nseg, POOL_HALO, pool_w), lambda b, t: (b, 0, 0)),
            pl.BlockSpec((1, d_model), fixed2),
            pl.BlockSpec((d_model, in_cols), fixed2, pipeline_mode=pl.Buffered(1)),
            pl.BlockSpec((1, NORM_BLOCK), fixed2),
            pl.BlockSpec((1, NORM_BLOCK), fixed2),
            pl.BlockSpec((NORM_BLOCK, NORM_BLOCK), fixed2),
            pl.BlockSpec(pw.shape, lambda b, t: (0, 0, 0)),
            pl.BlockSpec((1, d_model), fixed2),
        ],
        out_specs=[tok_spec, head_spec, tok_spec, head_spec, tok_spec, tok_spec, tok_spec,
                   pl.BlockSpec((nseg, POOL_HALO, pool_w), lambda b, t: (b, 0, 0))],
        out_shape=[act(BF16), by_head, act(BF16), by_head, act(BF16), act(BF16), act(BF16),
                   jax.ShapeDtypeStruct(prefix.shape, F32)],
        scratch_shapes=[pltpu.VMEM((nseg, POOL_HALO + seg_len, pool_w), F32)],
        compiler_params=_cparams(("arbitrary", "arbitrary")),
        name="inproj",
    )(x3, prefix, ng, w_in, qg, kg, bd, pw, ps)


def _half_masks(q):
    lane = lax.broadcasted_iota(jnp.int32, q.shape, 1)
    zero = jnp.zeros_like(q)
    return jnp.where(lane < HEAD_DIM, q, zero), jnp.where(lane >= HEAD_DIM, q, zero)


def _qk(qz, kblk):
    return lax.dot_general(qz, kblk, (((1,), (1,)), ((), ())), preferred_element_type=F32)


def _subln(o, sg_ref):
    ms = jnp.mean(o * o, axis=-1, keepdims=True)
    return o * lax.rsqrt(ms + RMS_EPS) * sg_ref[...]


def _attn_body(slope_ref, lam_ref, q_ref, k_ref, v_ref, ka_ref, sg_ref, o_ref,
               kf_ref, vf_ref, s_ref, p_ref, *, tq, nq):
    h = pl.program_id(1)
    slope = slope_ref[h]
    lam = lam_ref[0]
    lane = lax.broadcasted_iota(jnp.int32, (tq, LANES), 1)
    kf_ref[:, 0:V_DIM] = k_ref[0]
    kf_ref[:, V_DIM:] = ka_ref[...]
    vf_ref[:, 0:V_DIM] = v_ref[0]
    t_len = vf_ref.shape[0]
    vf_ref[:, V_DIM:] = (lax.broadcasted_iota(jnp.int32, (t_len, LANES), 1) == 0).astype(BF16)

    row = lax.broadcasted_iota(jnp.int32, (tq, tq), 0)
    col = lax.broadcasted_iota(jnp.int32, (tq, tq), 1)
    rc = (row - col).astype(F32)
    vis = lax.shift_right_logical(col, 6) <= lax.shift_right_logical(row, 6)
    corr = jnp.where(vis, jnp.minimum(rc, 0.0) * (2.0 * slope), NEG_INF)

    def scores(qi):
        nk = (qi + 1) * tq
        q1z, q2z = _half_masks(q_ref[0, qi * tq:(qi + 1) * tq, :])
        t = lax.broadcasted_iota(jnp.int32, (tq, LANES), 0) + qi * tq
        t_hi = lax.shift_left(lax.shift_right_logical(t, 8), 8).astype(F32)
        t_lo = jnp.bitwise_and(t, 255).astype(F32)
        qaug = jnp.where(lane == 0, -slope * t_hi,
                         jnp.where(lane == 1, -slope * t_lo,
                                   jnp.where(lane < 4, slope, 0.0))).astype(BF16)
        for m, qz in enumerate((q1z, q2z)):
            qa = jnp.concatenate([qz, qaug], axis=1)
            s_ref[qi % 2, m, :, 0:nk] = _qk(qa, kf_ref[0:nk, :])

    def finish(qi):
        nk = (qi + 1) * tq
        b = qi % 2
        outs = []
        for m in range(2):
            s_ref[b, m, :, nk - tq:nk] += corr
            mx = jnp.max(s_ref[b, m, :, 0:nk], axis=-1, keepdims=True)
            p_ref[m, :, 0:nk] = jnp.exp(s_ref[b, m, :, 0:nk] - mx).astype(BF16)
            outs.append(jnp.dot(p_ref[m, :, 0:nk], vf_ref[0:nk, :], preferred_element_type=F32))
        o1, o2 = outs
        c1 = 1.0 / o1[:, V_DIM:V_DIM + 1]
        c2 = lam / o2[:, V_DIM:V_DIM + 1]
        o = o1[:, 0:V_DIM] * c1 - o2[:, 0:V_DIM] * c2
        o_ref[0, qi * tq:(qi + 1) * tq, :] = _subln(o, sg_ref).astype(BF16)

    scores(0)
    for qi in range(nq):
        if qi + 1 < nq:
            scores(qi + 1)
        finish(qi)


def _attn_prompt(q, kb, vb, slopes, lam, sg):
    batch, t_len, width = q.shape
    tq = min(ATTN_TILE, t_len)
    nq = t_len // tq
    pos = jnp.arange(t_len, dtype=jnp.int32)
    ka = jnp.zeros((t_len, LANES), F32)
    ka = ka.at[:, 0:2].set(1.0).at[:, 2].set(((pos >> 8) << 8).astype(F32)).at[:, 3].set((pos & 255).astype(F32))
    smem = pl.BlockSpec(memory_space=pltpu.SMEM)
    seq = pl.BlockSpec((1, t_len, V_DIM), lambda b, h: (b, 0, h))
    return pl.pallas_call(
        functools.partial(_attn_body, tq=tq, nq=nq),
        grid=(batch, N_HEADS),
        in_specs=[
            smem, smem, seq, seq, seq,
            pl.BlockSpec((t_len, LANES), lambda b, h: (0, 0)),
            pl.BlockSpec((1, V_DIM), lambda b, h: (0, 0)),
        ],
        out_specs=seq,
        out_shape=jax.ShapeDtypeStruct((batch, t_len, width), BF16),
        scratch_shapes=[
            pltpu.VMEM((t_len, V_DIM + LANES), BF16),
            pltpu.VMEM((t_len, V_DIM + LANES), BF16),
            pltpu.VMEM((2, 2, tq, t_len), F32),
            pltpu.VMEM((2, tq, t_len), BF16),
        ],
        compiler_params=_cparams(("arbitrary", "arbitrary")),
        name="attn_prompt",
    )(slopes, lam, q, kb, vb, ka.astype(BF16), sg)


def _attn_dec_body(slope_ref, lam_ref, q_ref, kc_ref, vc_ref, kn_ref, vn_ref, sg_ref, o_ref, *, past):
    lam = lam_ref[0]
    tq = q_ref.shape[1]
    qpos_a = lax.broadcasted_iota(jnp.int32, (tq, past), 0) + past
    kpos_a = lax.broadcasted_iota(jnp.int32, (tq, past), 1)
    qpos_b = lax.broadcasted_iota(jnp.int32, (tq, tq), 0) + past
    kpos_b = lax.broadcasted_iota(jnp.int32, (tq, tq), 1) + past

    def dist_and_vis(qpos, kpos):
        vis = lax.shift_right_logical(kpos, 6) <= lax.shift_right_logical(qpos, 6)
        return jnp.abs(qpos - kpos).astype(F32), vis

    dist_a, vis_a = dist_and_vis(qpos_a, kpos_a)
    dist_b, vis_b = dist_and_vis(qpos_b, kpos_b)

    for h in range(N_HEADS):
        cs = slice(h * V_DIM, (h + 1) * V_DIM)
        slope = slope_ref[h]
        bias_a = jnp.where(vis_a, -slope * dist_a, NEG_INF)
        bias_b = jnp.where(vis_b, -slope * dist_b, NEG_INF)
        q1z, q2z = _half_masks(q_ref[0, :, cs])
        kc = kc_ref[0, pl.ds(h, past, stride=N_HEADS), :].astype(BF16)
        vc = vc_ref[0, pl.ds(h, past, stride=N_HEADS), :].astype(BF16)
        kn = kn_ref[0, :, cs]

        def softmax_parts(qz):
            sa = _qk(qz, kc) + bias_a
            sb = _qk(qz, kn) + bias_b
            m = jnp.maximum(jnp.max(sa, axis=-1, keepdims=True), jnp.max(sb, axis=-1, keepdims=True))
            pa = jnp.exp(sa - m)
            pb = jnp.exp(sb - m)
            l = jnp.sum(pa, axis=-1, keepdims=True) + jnp.sum(pb, axis=-1, keepdims=True)
            return pa, pb, l

        pa1, pb1, l1 = softmax_parts(q1z)
        pa2, pb2, l2 = softmax_parts(q2z)
        c1 = 1.0 / l1
        c2 = lam / l2
        wa = (pa1 * c1 - pa2 * c2).astype(BF16)
        wb = (pb1 * c1 - pb2 * c2).astype(BF16)
        o = (jnp.dot(wa, vc, preferred_element_type=F32)
             + jnp.dot(wb, vn_ref[0, :, cs], preferred_element_type=F32))
        o_ref[0, :, cs] = _subln(o, sg_ref).astype(BF16)


def _attn_sample(q, kn, vn, cache_k, cache_v, slopes, lam, sg):
    batch, tq, width = q.shape
    past = cache_k.shape[1] // N_HEADS
    smem = pl.BlockSpec(memory_space=pltpu.SMEM)
    new = pl.BlockSpec((1, tq, width), lambda b: (b, 0, 0))
    old = pl.BlockSpec((1, past * N_HEADS, V_DIM), lambda b: (b, 0, 0))
    return pl.pallas_call(
        functools.partial(_attn_dec_body, past=past),
        grid=(batch,),
        in_specs=[smem, smem, new, old, old, new, new,
                  pl.BlockSpec((1, V_DIM), lambda b: (0, 0))],
        out_specs=new,
        out_shape=jax.ShapeDtypeStruct((batch, tq, width), BF16),
        compiler_params=_cparams(("arbitrary",)),
        name="attn_sample",
    )(slopes, lam, q, cache_k, cache_v, kn, vn, sg)


def _outproj_body(at_ref, sa_ref, gp_ref, x_ref, wo_ref, g_ref, rw_ref, rb_ref,
                  h_ref, ti_ref, tg_ref, cnt_ref):
    i = pl.program_id(0)
    merged = (sa_ref[...].astype(F32) * at_ref[...].astype(F32) + gp_ref[...].astype(F32)).astype(BF16)
    hh = x_ref[...] + jnp.dot(merged, wo_ref[...], preferred_element_type=F32)
    _store_slabs(h_ref, hh)
    ms = jnp.mean(hh * hh, axis=-1, keepdims=True)
    hn = (hh * lax.rsqrt(ms + RMS_EPS) * g_ref[...]).astype(BF16)
    logits = jnp.dot(hn, rw_ref[...], preferred_element_type=F32) + rb_ref[...]
    lt = jnp.transpose(logits)[0:N_EXPERTS, :]
    tm = lt.shape[1]
    e_iota = lax.broadcasted_iota(jnp.int32, (N_EXPERTS, tm), 0)
    vals, idxs, hots = [], [], []
    cur = lt
    for _ in range(TOP_K):
        m = jnp.max(cur, axis=0, keepdims=True)
        idx = jnp.min(jnp.where(cur == m, e_iota, N_EXPERTS), axis=0, keepdims=True)
        hit = e_iota == idx
        vals.append(m)
        idxs.append(idx)
        hots.append(hit)
        cur = jnp.where(hit, -jnp.inf, cur)
    ex = [jnp.exp(v - vals[0]) for v in vals]
    den = ex[0] + ex[1] + ex[2] + ex[3]
    inv = 1.0 / den
    zi = jnp.zeros((8 - TOP_K, tm), jnp.int32)
    zf = jnp.zeros((8 - TOP_K, tm), F32)
    ti_ref[...] = jnp.concatenate(idxs + [zi], axis=0)
    tg_ref[...] = jnp.concatenate([e * inv for e in ex] + [zf], axis=0)
    hot = jnp.concatenate([hh_.astype(F32) for hh_ in hots], axis=0)
    csum = jnp.sum(hot, axis=1, keepdims=True)

    @pl.when(i == 0)
    def _():
        cnt_ref[...] = jnp.zeros_like(cnt_ref)

    cnt_ref[...] += jnp.broadcast_to(csum, cnt_ref.shape)


def _outproj(attn, sa, gp, x, w_out, g, rw, rb):
    n_tok, d_model = x.shape
    tm = min(ROW_TILE, n_tok)
    row = lambda i: (i, 0)
    fixed = lambda i: (0, 0)
    colb = lambda i: (0, i)
    return pl.pallas_call(
        _outproj_body,
        grid=(n_tok // tm,),
        in_specs=[
            pl.BlockSpec((tm, d_model), row),
            pl.BlockSpec((tm, d_model), row),
            pl.BlockSpec((tm, d_model), row),
            pl.BlockSpec((tm, d_model), row),
            pl.BlockSpec((d_model, d_model), fixed),
            pl.BlockSpec((1, d_model), fixed),
            pl.BlockSpec((d_model, LANES), fixed),
            pl.BlockSpec((1, LANES), fixed),
        ],
        out_specs=[
            pl.BlockSpec((tm * ROW_SUB, LANES), row),
            pl.BlockSpec((8, tm), colb),
            pl.BlockSpec((8, tm), colb),
            pl.BlockSpec((TOP_K * N_EXPERTS, LANES), fixed),
        ],
        out_shape=[
            jax.ShapeDtypeStruct((n_tok * ROW_SUB, LANES), F32),
            jax.ShapeDtypeStruct((8, n_tok), jnp.int32),
            jax.ShapeDtypeStruct((8, n_tok), F32),
            jax.ShapeDtypeStruct((TOP_K * N_EXPERTS, LANES), F32),
        ],
        compiler_params=_cparams(("arbitrary",)),
        name="outproj",
    )(attn, sa, gp, x, w_out, g, rw, rb)


def _rank_body(ti_ref, base_ref, tri_ref, dest_ref, carry_ref):
    i = pl.program_id(0)

    @pl.when(i == 0)
    def _():
        carry_ref[...] = jnp.zeros_like(carry_ref)

    tt = ti_ref.shape[1]
    e_iota = lax.broadcasted_iota(jnp.int32, (N_EXPERTS, tt), 0)
    hot = jnp.concatenate([(ti_ref[k:k + 1, :] == e_iota).astype(F32) for k in range(TOP_K)], axis=0)
    incl = jnp.dot(hot.astype(BF16), tri_ref[...], preferred_element_type=F32)
    slot = base_ref[:, 0:1] + carry_ref[:, 0:1] + incl - 1.0
    picked = hot * slot
    rows = [jnp.sum(picked[k * N_EXPERTS:(k + 1) * N_EXPERTS, :], axis=0, keepdims=True)
            for k in range(TOP_K)]
    rows.append(jnp.zeros((8 - TOP_K, tt), F32))
    dest_ref[...] = jnp.concatenate(rows, axis=0).astype(jnp.int32)
    carry_ref[...] += jnp.broadcast_to(jnp.sum(hot, axis=1, keepdims=True), carry_ref.shape)


def _rank(topi, base, tri):
    n_tok = topi.shape[1]
    tt = tri.shape[0]
    return pl.pallas_call(
        _rank_body,
        grid=(n_tok // tt,),
        in_specs=[
            pl.BlockSpec((8, tt), lambda i: (0, i)),
            pl.BlockSpec(base.shape, lambda i: (0, 0)),
            pl.BlockSpec((tt, tt), lambda i: (0, 0)),
        ],
        out_specs=pl.BlockSpec((8, tt), lambda i: (0, i)),
        out_shape=jax.ShapeDtypeStruct((8, n_tok), jnp.int32),
        scratch_shapes=[pltpu.VMEM((TOP_K * N_EXPERTS, LANES), F32)],
        compiler_params=_cparams(("arbitrary",)),
        name="rank",
    )(topi, base, tri)


DISPATCH_BUFS = 3


def _slab_copy(src, src_row, dst, dst_row, sem):
    def first_sublane(row):
        return row * ROW_SUB if isinstance(row, int) else pl.multiple_of(row * ROW_SUB, ROW_SUB)

    return pltpu.make_async_copy(src.at[pl.ds(first_sublane(src_row), ROW_SUB)],
                                 dst.at[pl.ds(first_sublane(dst_row), ROW_SUB)], sem)


def _dispatch_body(zs_ref, zl_ref, tz_ref, dest_hbm, hp_hbm, hs_hbm, xs_hbm,
                   idx0, idx1, idx2, hb0, hb1, hb2, lsem, dsem, zsem,
                   *, tt, n_prompt_tiles, n_out_tiles):
    i = pl.program_id(0)
    n = pl.num_programs(0)
    idx = (idx0, idx1, idx2)
    hbuf = (hb0, hb1, hb2)
    nbuf = DISPATCH_BUFS
    tile_sub = tt * ROW_SUB

    def load_wait(s):
        pltpu.make_async_copy(dest_hbm.at[0], idx[s], lsem.at[s]).wait()
        pltpu.make_async_copy(hp_hbm.at[pl.ds(0, tile_sub)], hbuf[s], lsem.at[s]).wait()

    def load_start(tile, s):
        pltpu.make_async_copy(dest_hbm.at[tile], idx[s], lsem.at[s]).start()

        @pl.when(tile < n_prompt_tiles)
        def _():
            r0 = pl.multiple_of(tile * tile_sub, tile_sub)
            pltpu.make_async_copy(hp_hbm.at[pl.ds(r0, tile_sub)], hbuf[s], lsem.at[s]).start()

        @pl.when(tile >= n_prompt_tiles)
        def _():
            r0 = pl.multiple_of((tile - n_prompt_tiles) * tile_sub, tile_sub)
            pltpu.make_async_copy(hs_hbm.at[pl.ds(r0, tile_sub)], hbuf[s], lsem.at[s]).start()

    def scatter_start(s):
        for k in range(TOP_K):
            for r in range(tt):
                _slab_copy(hbuf[s], r, xs_hbm, idx[s][k * tt + r], dsem.at[s]).start(priority=r % 2)

    def scatter_wait(s):
        for _ in range(TOP_K):
            pltpu.make_async_copy(hbuf[s], xs_hbm.at[pl.ds(0, tile_sub)], dsem.at[s]).wait()

    @pl.when(i == 0)
    def _():
        zero = hbuf[nbuf - 1]
        zero[...] = jnp.zeros_like(zero)

        def per_expert(e, c):
            def one(r, c2):
                _slab_copy(zero, 0, xs_hbm, zs_ref[e] + r, zsem).start()
                return c2
            lax.fori_loop(0, zl_ref[e], one, 0)

            def one_wait(r, c2):
                _slab_copy(zero, 0, xs_hbm, zs_ref[e] + r, zsem).wait()
                return c2
            lax.fori_loop(0, zl_ref[e], one_wait, 0)
            return c
        lax.fori_loop(0, N_EXPERTS, per_expert, 0)

        def tail_copy(j):
            r0 = pl.multiple_of(j * tile_sub, tile_sub)
            return pltpu.make_async_copy(zero, xs_hbm.at[pl.ds(r0, tile_sub)], zsem)

        def tail_one(j, c):
            tail_copy(j).start()
            tail_copy(j).wait()
            return c
        lax.fori_loop(tz_ref[0], n_out_tiles, tail_one, 0)

        load_start(0, 0)

        @pl.when(n > 1)
        def _():
            load_start(1, 1)

    for s in range(nbuf):
        @pl.when(i % nbuf == s)
        def _(s=s):
            prev = (s + nbuf - 1) % nbuf
            load_wait(s)
            scatter_start(s)

            @pl.when(i > 0)
            def _():
                scatter_wait(prev)

            @pl.when(i + 2 < n)
            def _():
                load_start(i + 2, prev)

            @pl.when(i == n - 1)
            def _():
                scatter_wait(s)


def _dispatch(zero_start, zero_len, tail_tile, dest_tiles, hp, hs, n_rows):
    n_tiles, width = dest_tiles.shape
    tt = width // TOP_K
    n_prompt_tiles = hp.shape[0] // (tt * ROW_SUB)
    any_spec = pl.BlockSpec(memory_space=pl.ANY)
    grid_spec = pltpu.PrefetchScalarGridSpec(
        num_scalar_prefetch=3,
        grid=(n_tiles,),
        in_specs=[any_spec, any_spec, any_spec],
        out_specs=any_spec,
        scratch_shapes=[pltpu.SMEM((width,), jnp.int32)] * DISPATCH_BUFS
        + [pltpu.VMEM((tt * ROW_SUB, LANES), F32)] * DISPATCH_BUFS
        + [
            pltpu.SemaphoreType.DMA((DISPATCH_BUFS,)),
            pltpu.SemaphoreType.DMA((DISPATCH_BUFS,)),
            pltpu.SemaphoreType.DMA,
        ],
    )
    return pl.pallas_call(
        functools.partial(_dispatch_body, tt=tt, n_prompt_tiles=n_prompt_tiles, n_out_tiles=n_rows // tt),
        grid_spec=grid_spec,
        out_shape=jax.ShapeDtypeStruct((n_rows * ROW_SUB, LANES), F32),
        compiler_params=_cparams(("arbitrary",)),
        name="dispatch",
    )(zero_start, zero_len, tail_tile, dest_tiles, hp, hs)


def _moe_body(te_ref, nu_ref, x_ref, g_ref, wg_ref, bg_ref, wu_ref, bu_ref, wd_ref, bd_ref,
              y_ref, wgb, wub, wdb, *, tm):
    i = pl.program_id(0)
    n_used = nu_ref[0]

    @pl.when(i >= n_used)
    def _():
        y_ref[...] = jnp.zeros_like(y_ref)

    @pl.when(i < n_used)
    def _():
        prev = te_ref[jnp.maximum(i - 1, 0)]
        changed = jnp.logical_or(i == 0, te_ref[i] != prev)

        @pl.when(changed)
        def _():
            wgb[...] = wg_ref[0].astype(BF16)
            wub[...] = wu_ref[0].astype(BF16)
            wdb[...] = wd_ref[0].astype(BF16)

        half = tm // 2
        d_model = ROW_SUB * LANES

        def normed_half(r0):
            cols = [x_ref[pl.ds(r0 * ROW_SUB + c, half, stride=ROW_SUB), :] for c in range(ROW_SUB)]
            ssq = cols[0] * cols[0]
            for xc in cols[1:]:
                ssq = ssq + xc * xc
            r = lax.rsqrt(jnp.sum(ssq, axis=-1, keepdims=True) * (1.0 / d_model) + RMS_EPS)
            return jnp.concatenate(
                [(xc * r * g_ref[:, c * LANES:(c + 1) * LANES]).astype(BF16) for c, xc in enumerate(cols)], axis=1)

        def gate_up(xb):
            gt = jnp.dot(xb, wgb[...], preferred_element_type=F32) + bg_ref[0]
            up = jnp.dot(xb, wub[...], preferred_element_type=F32) + bu_ref[0]
            return gt, up

        def down(gt, up, r0):
            gt = jnp.minimum(gt, SWIGLU_LIMIT)
            up = jnp.clip(up, -SWIGLU_LIMIT, SWIGLU_LIMIT)
            hdn = (up + 1.0) * (gt * _sigmoid(SWIGLU_ALPHA * gt))
            y = jnp.dot(hdn.astype(BF16), wdb[...], preferred_element_type=F32) + bd_ref[0]
            for c in range(ROW_SUB):
                y_ref[pl.ds(r0 * ROW_SUB + c, half, stride=ROW_SUB), :] = y[:, c * LANES:(c + 1) * LANES]

        ga, ua = gate_up(normed_half(0))
        gb, ub = gate_up(normed_half(half))
        down(ga, ua, 0)
        down(gb, ub, half)


def _moe(tile_expert, n_used, x_rows, g, w_gate, b_gate, w_up, b_up, w_down, b_down, *, tm):
    n_tiles = x_rows.shape[0] // (tm * ROW_SUB)
    d_model = ROW_SUB * LANES
    d_ff = w_gate.shape[2]
    assert w_gate.shape[1] == d_model
    wspec = lambda shape: pl.BlockSpec((1,) + shape, lambda i, te, nu: (te[i], 0, 0))
    grid_spec = pltpu.PrefetchScalarGridSpec(
        num_scalar_prefetch=2,
        grid=(n_tiles,),
        in_specs=[
            pl.BlockSpec((tm * ROW_SUB, LANES), lambda i, te, nu: (jnp.minimum(i, nu[0] - 1), 0)),
            pl.BlockSpec((1, d_model), lambda i, te, nu: (0, 0)),
            wspec((d_model, d_ff)), wspec((1, d_ff)),
            wspec((d_model, d_ff)), wspec((1, d_ff)),
            wspec((d_ff, d_model)), wspec((1, d_model)),
        ],
        out_specs=pl.BlockSpec((tm * ROW_SUB, LANES), lambda i, te, nu: (i, 0)),
        scratch_shapes=[
            pltpu.VMEM((d_model, d_ff), BF16),
            pltpu.VMEM((d_model, d_ff), BF16),
            pltpu.VMEM((d_ff, d_model), BF16),
        ],
    )
    return pl.pallas_call(
        functools.partial(_moe_body, tm=tm),
        grid_spec=grid_spec,
        out_shape=jax.ShapeDtypeStruct((n_tiles * tm * ROW_SUB, LANES), F32),
        compiler_params=_cparams(("arbitrary",)),
        name="moe",
    )(tile_expert, n_used, x_rows, g,
      w_gate, b_gate.reshape(N_EXPERTS, 1, d_ff), w_up, b_up.reshape(N_EXPERTS, 1, d_ff),
      w_down, b_down.reshape(N_EXPERTS, 1, d_model))


def _combine_body(dest_hbm, y_hbm, h_ref, gt_ref, o_ref, idx0, idx1, yb0, yb1, isem, gsem, *, tt):
    i = pl.program_id(0)
    n = pl.num_programs(0)
    idx = (idx0, idx1)
    ybuf = (yb0, yb1)

    def idx_copy(tile, s):
        return pltpu.make_async_copy(dest_hbm.at[tile], idx[s], isem.at[s])

    def issue_gather(s):
        for k in range(TOP_K):
            for r in range(tt):
                src = pl.multiple_of(idx[s][k * tt + r] * ROW_SUB, ROW_SUB)
                pltpu.make_async_copy(y_hbm.at[pl.ds(src, ROW_SUB)],
                                      ybuf[s].at[k, pl.ds(r * ROW_SUB, ROW_SUB)], gsem.at[s]).start(priority=r % 2)

    def wait_gather(s):
        for k in range(TOP_K):
            pltpu.make_async_copy(y_hbm.at[pl.ds(0, tt * ROW_SUB)], ybuf[s].at[k], gsem.at[s]).wait()

    @pl.when(i == 0)
    def _():
        idx_copy(0, 0).start()
        idx_copy(0, 0).wait()
        issue_gather(0)
        idx_copy(1, 1).start()

    for s in range(2):
        @pl.when(i % 2 == s)
        def _(s=s):
            o = 1 - s
            idx_copy(i + 1, o).wait()
            issue_gather(o)
            idx_copy(i + 2, s).start()
            wait_gather(s)
            gts = gt_ref[...]
            for c in range(ROW_SUB):
                cs = slice(c * LANES, (c + 1) * LANES)
                acc = _load_slab_cols(h_ref, tt, c)
                for k in range(TOP_K):
                    acc = acc + gts[:, k:k + 1] * ybuf[s][k, pl.ds(c, tt, stride=ROW_SUB), :]
                o_ref[:, cs] = acc

            @pl.when(i == n - 1)
            def _():
                wait_gather(o)
                idx_copy(i + 2, s).wait()


def _combine(dest_tiles, y_rows, h, gates_tok):
    n_tiles, width = dest_tiles.shape[0] - 2, dest_tiles.shape[1]
    tt = width // TOP_K
    n_tok, d_model = h.shape[0] // ROW_SUB, ROW_SUB * LANES
    any_spec = pl.BlockSpec(memory_space=pl.ANY)
    return pl.pallas_call(
        functools.partial(_combine_body, tt=tt),
        grid=(n_tiles,),
        in_specs=[any_spec, any_spec,
                  pl.BlockSpec((tt * ROW_SUB, LANES), lambda i: (i, 0)),
                  pl.BlockSpec((tt, 8), lambda i: (i, 0))],
        out_specs=pl.BlockSpec((tt, d_model), lambda i: (i, 0)),
        out_shape=jax.ShapeDtypeStruct((n_tok, d_model), F32),
        scratch_shapes=[
            pltpu.SMEM((TOP_K * tt,), jnp.int32),
            pltpu.SMEM((TOP_K * tt,), jnp.int32),
            pltpu.VMEM((TOP_K, tt * ROW_SUB, LANES), F32),
            pltpu.VMEM((TOP_K, tt * ROW_SUB, LANES), F32),
            pltpu.SemaphoreType.DMA((2,)),
            pltpu.SemaphoreType.DMA((2,)),
        ],
        compiler_params=_cparams(("arbitrary",)),
        name="combine",
    )(dest_tiles, y_rows, h, gates_tok)


def _block_diag_ones():
    r = lax.broadcasted_iota(jnp.int32, (NORM_BLOCK, NORM_BLOCK), 0) // HEAD_DIM
    c = lax.broadcasted_iota(jnp.int32, (NORM_BLOCK, NORM_BLOCK), 1) // HEAD_DIM
    return (r == c).astype(BF16)


def _dest_tiles(dest, tt):
    n_tok = dest.shape[1]
    tiles = dest[:TOP_K].reshape(TOP_K, n_tok // tt, tt).transpose(1, 0, 2).reshape(n_tok // tt, TOP_K * tt)
    return jnp.pad(tiles, ((0, 2), (0, 0)))


def _layer(xp, xs, ck, cv, sp, lam, lam_init, ng, w_in, qng, kng, slg, pw, ps, w_out, fg,
           rw, rb, w_gate, b_gate, w_up, b_up, w_down, b_down):
    bp, t_len, d_model = xp.shape
    bs, ts, _ = xs.shape
    past = ck.shape[1]
    pool_w = sp.shape[-1]
    n_p, n_s = bp * t_len, bs * ts

    w_in_b = w_in.astype(BF16)
    w_out_b = w_out.astype(BF16)
    pw_b = pw.astype(BF16)
    reps = NORM_BLOCK // HEAD_DIM
    qg = (jnp.tile(qng.astype(F32), reps) * (HEAD_DIM ** -0.5)).reshape(1, NORM_BLOCK)
    kg = jnp.tile(kng.astype(F32), reps).reshape(1, NORM_BLOCK)
    bd = _block_diag_ones()
    ng2 = ng.astype(F32).reshape(1, d_model)
    ps2 = ps.astype(F32).reshape(1, d_model)
    sg = (slg.astype(F32) * (1.0 - lam_init)).reshape(1, V_DIM)
    slopes = jnp.exp2(-(8.0 / N_HEADS) * jnp.arange(1, N_HEADS + 1, dtype=F32))
    lam1 = lam.reshape(1).astype(F32)
    fg2 = fg.astype(F32).reshape(1, d_model)
    rw_b = jnp.zeros((d_model, LANES), BF16).at[:, :N_EXPERTS].set(rw.astype(BF16))
    rb2 = jnp.zeros((1, LANES), F32).at[0, :N_EXPERTS].set(rb.astype(F32))

    tm = min(ROW_TILE, t_len)
    zero_pre = jnp.zeros((bp, POOL_HALO, pool_w), F32)
    qp, kp, kpb, vp, vpb, sap, gpp, utp = _inproj(
        xp, zero_pre, ng2, w_in_b, qg, kg, bd, pw_b, ps2, nseg=1, seg_len=tm, start_pos=0, carry=True)
    atp = _attn_prompt(qp, kpb, vpb, slopes, lam1, sg)
    hp, tip, tgp, cntp = _outproj(atp.reshape(n_p, d_model), sap.reshape(n_p, d_model),
                                  gpp.reshape(n_p, d_model), xp.reshape(n_p, d_model),
                                  w_out_b, fg2, rw_b, rb2)

    pre_s = jnp.concatenate([jnp.zeros((bs, POOL_HALO - sp.shape[1], pool_w), F32), sp.astype(F32)], axis=1)
    qs, ks, ksb, vs, vsb, sas, gps, uts = _inproj(
        xs.reshape(1, n_s, d_model), pre_s, ng2, w_in_b, qg, kg, bd, pw_b, ps2,
        nseg=bs, seg_len=ts, start_pos=past, carry=False)
    ats = _attn_sample(qs.reshape(bs, ts, d_model), ksb.reshape(bs, ts, d_model), vsb.reshape(bs, ts, d_model),
                       ck.reshape(bs, past * N_HEADS, V_DIM), cv.reshape(bs, past * N_HEADS, V_DIM),
                       slopes, lam1, sg)
    hs, tis, tgs, cnts = _outproj(ats.reshape(n_s, d_model), sas.reshape(n_s, d_model),
                                  gps.reshape(n_s, d_model), xs.reshape(n_s, d_model),
                                  w_out_b, fg2, rw_b, rb2)

    n_tok = n_p + n_s
    topi = jnp.concatenate([tip, tis], axis=1)
    gates = jnp.concatenate([tgp, tgs], axis=1)
    cnt = (cntp[:, 0] + cnts[:, 0]).astype(jnp.int32).reshape(TOP_K, N_EXPERTS)
    per_expert = jnp.sum(cnt, axis=0)
    padded = (per_expert + MOE_TILE - 1) // MOE_TILE * MOE_TILE
    pad_end = jnp.cumsum(padded)
    pad_start = pad_end - padded
    base = pad_start[None, :] + jnp.cumsum(cnt, axis=0) - cnt
    base_f = jnp.broadcast_to(base.reshape(-1, 1).astype(F32), (TOP_K * N_EXPERTS, LANES))
    tt = TOK_TILE
    tri = (lax.broadcasted_iota(jnp.int32, (tt, tt), 0) <= lax.broadcasted_iota(jnp.int32, (tt, tt), 1)).astype(BF16)
    dest = _rank(topi, base_f, tri)

    n_tiles = -(-(n_tok * TOP_K) // MOE_TILE) + N_EXPERTS
    n_used = (pad_end[-1] // MOE_TILE).astype(jnp.int32)
    tile_start = jnp.arange(n_tiles, dtype=jnp.int32) * MOE_TILE
    last_start = jnp.maximum(pad_end[-1] - MOE_TILE, 0)
    tile_expert = jnp.minimum(
        jnp.sum((jnp.minimum(tile_start, last_start)[:, None] >= pad_end[None, :]).astype(jnp.int32), axis=1),
        N_EXPERTS - 1)
    dest_p = _dest_tiles(dest[:, :n_p], tt)
    dest_s = _dest_tiles(dest[:, n_p:], tt)
    dest_all = jnp.concatenate([dest_p[:-2], dest_s[:-2]], axis=0)
    x_rows = _dispatch(pad_start + per_expert, padded - per_expert, (pad_end[-1:] // tt).astype(jnp.int32),
                       dest_all, hp, hs, n_tiles * MOE_TILE)
    y_rows = _moe(tile_expert, n_used.reshape(1), x_rows, fg2,
                  w_gate, b_gate, w_up, b_up, w_down, b_down, tm=MOE_TILE)

    gates_tok = jnp.transpose(gates)
    yp = _combine(dest_p, y_rows, hp, gates_tok[:n_p])
    ys = _combine(dest_s, y_rows, hs, gates_tok[n_p:])

    heads = (N_HEADS, V_DIM)
    return (yp.reshape(bp, t_len, d_model), ys.reshape(bs, ts, d_model),
            kp.reshape(bp, t_len, *heads), vp.reshape(bp, t_len, *heads), utp[:, 1:],
            ks.reshape(bs, ts, *heads), vs.reshape(bs, ts, *heads), uts[:, 1:])


def kernel(x_prompt, x_sample, cache_k, cache_v, state_pool, norm_mix_g, w_in, q_norm_g, k_norm_g,
           lambda_q1, lambda_k1, lambda_q2, lambda_k2, subln_g, pool_w, pool_scale, w_out, norm_ffn_g,
           router_w, router_b, w_gate, b_gate, w_up, b_up, w_down, b_down):
    depth = w_in.shape[0]
    hp, hs = x_prompt, x_sample
    outs = [[] for _ in range(6)]
    for layer in range(depth):
        lam_init = 0.8 - 0.6 * math.exp(-0.3 * layer)
        lam = (jnp.exp(jnp.sum(lambda_q1[layer].astype(F32) * lambda_k1[layer].astype(F32)))
               - jnp.exp(jnp.sum(lambda_q2[layer].astype(F32) * lambda_k2[layer].astype(F32)))
               + lam_init)
        hp, hs, kp, vp, up, ks, vs, us = _layer(
            hp, hs, cache_k[layer], cache_v[layer], state_pool[layer], lam, lam_init,
            norm_mix_g[layer], w_in[layer], q_norm_g[layer], k_norm_g[layer], subln_g[layer],
            pool_w[layer], pool_scale[layer], w_out[layer], norm_ffn_g[layer],
            router_w[layer], router_b[layer], w_gate[layer], b_gate[layer], w_up[layer], b_up[layer],
            w_down[layer], b_down[layer])
        for lst, val in zip(outs, (kp, vp, up, ks, vs, us)):
            lst.append(val)
    return (hp, hs) + tuple(jnp.stack(o) for o in outs)
```

```python
import functools
import math

import jax
import jax.numpy as jnp
from jax import lax
from jax.experimental import pallas as pl
from jax.experimental.pallas import tpu as pltpu

F32 = jnp.float32
BF16 = jnp.bfloat16

CHUNK = 64
N_HEADS = 8
HEAD_DIM = 64
V_DIM = 2 * HEAD_DIM
POOL_WINDOWS = (2, 4, 8, 16)
POOL_GROUP_DIM = 128
POOL_OUT_DIM = 256
POOL_HALO = 16
N_EXPERTS = 32
TOP_K = 4
SWIGLU_LIMIT = 7.0
SWIGLU_ALPHA = 1.702
RMS_EPS = 1e-6
NEG_INF = -1e30

LANES = 128
ROW_SUB = 8
NORM_BLOCK = 256
VMEM_LIMIT = 56 * 1024 * 1024

ROW_TILE = 512
ATTN_TILE = 256
MOE_TILE = 512
TOK_TILE = 256
RANK_TILE_MAX = 1024


def _sigmoid(x):
    return 1.0 / (1.0 + jnp.exp(-x))


def _cparams(sem):
    return pltpu.CompilerParams(dimension_semantics=sem, vmem_limit_bytes=VMEM_LIMIT)


def _store_slabs(ref, val):
    rows = val.shape[0]
    for c in range(ROW_SUB):
        ref[pl.ds(c, rows, stride=ROW_SUB), :] = val[:, c * LANES:(c + 1) * LANES]


def _load_slab_cols(ref, rows, c):
    return ref[pl.ds(c, rows, stride=ROW_SUB), :]


def _inproj_body(x_ref, pre_ref, ng_ref, w_ref, qg_ref, kg_ref, bd_ref, pw_ref, ps_ref,
                 q_ref, k_ref, kb_ref, v_ref, vb_ref, sa_ref, gp_ref, ut_ref, ext_ref,
                 *, nseg, seg_len, start_pos, carry, attn_w, pool_w):
    rows = nseg * seg_len
    t = pl.program_id(1)
    x = x_ref[0]
    ms = jnp.mean(x * x, axis=-1, keepdims=True)
    xn = (x * lax.rsqrt(ms + RMS_EPS) * ng_ref[...]).astype(BF16)

    def proj(c0, width):
        return jnp.dot(xn, w_ref[:, c0:c0 + width], preferred_element_type=F32)

    bd = bd_ref[...]

    def group_norm(p, g_ref):
        ss = jnp.dot((p * p).astype(BF16), bd, preferred_element_type=F32)
        return p * lax.rsqrt(ss * (1.0 / HEAD_DIM) + RMS_EPS) * g_ref[...]

    nb = NORM_BLOCK
    heads_per_block = nb // V_DIM

    def store_heads(ref, val, c):
        for j in range(heads_per_block):
            head = c * heads_per_block + j
            ref[0, pl.ds(head, rows, stride=N_HEADS), :] = val[:, j * V_DIM:(j + 1) * V_DIM]

    def sink_q(val, c):
        q_ref[0, :, c * nb:(c + 1) * nb] = group_norm(val, qg_ref).astype(BF16)

    def sink_k(val, c):
        kn = group_norm(val, kg_ref)
        store_heads(k_ref, kn, c)
        kb_ref[0, :, c * nb:(c + 1) * nb] = kn.astype(BF16)

    def sink_v(val, c):
        store_heads(v_ref, val, c)
        vb_ref[0, :, c * nb:(c + 1) * nb] = val.astype(BF16)

    n_chunks = attn_w // nb
    work = [(part * attn_w + c * nb, sink, c)
            for part, sink in enumerate((sink_q, sink_k, sink_v)) for c in range(n_chunks)]
    pending = proj(work[0][0], nb)
    for j, (_, sink, c) in enumerate(work):
        cur = pending
        if j + 1 < len(work):
            pending = proj(work[j + 1][0], nb)
        sink(cur, c)

    if carry:
        @pl.when(t == 0)
        def _():
            ext_ref[:, 0:POOL_HALO, :] = pre_ref[...]
    else:
        ext_ref[:, 0:POOL_HALO, :] = pre_ref[...]
    for c in range(pool_w // nb):
        cs = slice(c * nb, (c + 1) * nb)
        u = proj(3 * attn_w + c * nb, nb)
        ext_ref[:, POOL_HALO:POOL_HALO + seg_len, cs] = u.reshape(nseg, seg_len, nb)

    ga0 = 3 * attn_w + pool_w
    d_model = attn_w
    for c in range(d_model // nb):
        cs = slice(c * nb, (c + 1) * nb)
        sa_ref[0, :, cs] = _sigmoid(proj(ga0 + c * nb, nb)).astype(BF16)

    gb0 = ga0 + d_model
    row = lax.broadcasted_iota(jnp.int32, (1, seg_len, 1), 1)
    pos = row + start_pos
    if carry:
        pos = pos + t * seg_len
    for g, w in enumerate(POOL_WINDOWS):
        cs = slice(g * POOL_GROUP_DIM, (g + 1) * POOL_GROUP_DIM)
        own = ext_ref[:, POOL_HALO:POOL_HALO + seg_len, cs]
        acc = own
        for i in range(1, w):
            acc = acc + ext_ref[:, POOL_HALO - i:POOL_HALO - i + seg_len, cs]
        inv = 1.0 / jnp.minimum(w, pos + 1).astype(F32)
        z = (acc * inv - own).reshape(rows, POOL_GROUP_DIM)
        os_ = slice(g * POOL_OUT_DIM, (g + 1) * POOL_OUT_DIM)
        yp = jnp.dot(z.astype(BF16), pw_ref[g], preferred_element_type=F32) * ps_ref[:, os_]
        gb = proj(gb0 + g * POOL_OUT_DIM, POOL_OUT_DIM)
        gp_ref[0, :, os_] = (_sigmoid(gb) * yp).astype(BF16)

    tail = ext_ref[:, seg_len:seg_len + POOL_HALO, :]
    ut_ref[...] = tail
    if carry:
        ext_ref[:, 0:POOL_HALO, :] = tail


def _inproj(x3, prefix, ng, w_in, qg, kg, bd, pw, ps, *, nseg, seg_len, start_pos, carry):
    groups, t_len, d_model = x3.shape
    rows = nseg * seg_len
    steps = t_len // rows
    in_cols = w_in.shape[1]
    pool_w = prefix.shape[-1]
    attn_w = d_model
    assert in_cols == 3 * attn_w + pool_w + 2 * d_model
    tok = lambda b, t: (b, t, 0)
    fixed2 = lambda b, t: (0, 0)
    act = lambda dt: jax.ShapeDtypeStruct((groups, t_len, d_model), dt)
    by_head = jax.ShapeDtypeStruct((groups, t_len * N_HEADS, V_DIM), F32)
    tok_spec = pl.BlockSpec((1, rows, d_model), tok)
    head_spec = pl.BlockSpec((1, rows * N_HEADS, V_DIM), tok)
    body = functools.partial(_inproj_body, nseg=nseg, seg_len=seg_len, start_pos=start_pos,
                             carry=carry, attn_w=attn_w, pool_w=pool_w)
    return pl.pallas_call(
        body,
        grid=(groups, steps),
        in_specs=[
            pl.BlockSpec((1, rows, d_model), tok),
            pl.BlockSpec((nseg, POOL_HALO, pool_w), lambda b, t: (b, 0, 0)),
            pl.BlockSpec((1, d_model), fixed2),
            pl.BlockSpec((d_model, in_cols), fixed2, pipeline_mode=pl.Buffered(1)),
            pl.BlockSpec((1, NORM_BLOCK), fixed2),
            pl.BlockSpec((1, NORM_BLOCK), fixed2),
            pl.BlockSpec((NORM_BLOCK, NORM_BLOCK), fixed2),
            pl.BlockSpec(pw.shape, lambda b, t: (0, 0, 0)),
            pl.BlockSpec((1, d_model), fixed2),
        ],
        out_specs=[tok_spec, head_spec, tok_spec, head_spec, tok_spec, tok_spec, tok_spec,
                   pl.BlockSpec((nseg, POOL_HALO, pool_w), lambda b, t: (b, 0, 0))],
        out_shape=[act(BF16), by_head, act(BF16), by_head, act(BF16), act(BF16), act(BF16),
                   jax.ShapeDtypeStruct(prefix.shape, F32)],
        scratch_shapes=[pltpu.VMEM((nseg, POOL_HALO + seg_len, pool_w), F32)],
        compiler_params=_cparams(("arbitrary", "arbitrary")),
        name="inproj",
    )(x3, prefix, ng, w_in, qg, kg, bd, pw, ps)


def _half_masks(q):
    lane = lax.broadcasted_iota(jnp.int32, q.shape, 1)
    zero = jnp.zeros_like(q)
    return jnp.where(lane < HEAD_DIM, q, zero), jnp.where(lane >= HEAD_DIM, q, zero)


def _qk(qz, kblk):
    return lax.dot_general(qz, kblk, (((1,), (1,)), ((), ())), preferred_element_type=F32)


def _subln(o, sg_ref):
    ms = jnp.mean(o * o, axis=-1, keepdims=True)
    return o * lax.rsqrt(ms + RMS_EPS) * sg_ref[...]


def _attn_body(slope_ref, lam_ref, q_ref, k_ref, v_ref, ka_ref, sg_ref, o_ref,
               kf_ref, vf_ref, s_ref, p_ref, *, tq, nq):
    h = pl.program_id(1)
    slope = slope_ref[h]
    lam = lam_ref[0]
    lane = lax.broadcasted_iota(jnp.int32, (tq, LANES), 1)
    kf_ref[:, 0:V_DIM] = k_ref[0]
    kf_ref[:, V_DIM:] = ka_ref[...]
    vf_ref[:, 0:V_DIM] = v_ref[0]
    t_len = vf_ref.shape[0]
    vf_ref[:, V_DIM:] = (lax.broadcasted_iota(jnp.int32, (t_len, LANES), 1) == 0).astype(BF16)

    row = lax.broadcasted_iota(jnp.int32, (tq, tq), 0)
    col = lax.broadcasted_iota(jnp.int32, (tq, tq), 1)
    rc = (row - col).astype(F32)
    vis = lax.shift_right_logical(col, 6) <= lax.shift_right_logical(row, 6)
    corr = jnp.where(vis, jnp.minimum(rc, 0.0) * (2.0 * slope), NEG_INF)

    def scores(qi):
        nk = (qi + 1) * tq
        q1z, q2z = _half_masks(q_ref[0, qi * tq:(qi + 1) * tq, :])
        t = lax.broadcasted_iota(jnp.int32, (tq, LANES), 0) + qi * tq
        t_hi = lax.shift_left(lax.shift_right_logical(t, 8), 8).astype(F32)
        t_lo = jnp.bitwise_and(t, 255).astype(F32)
        qaug = jnp.where(lane == 0, -slope * t_hi,
                         jnp.where(lane == 1, -slope * t_lo,
                                   jnp.where(lane < 4, slope, 0.0))).astype(BF16)
        for m, qz in enumerate((q1z, q2z)):
            qa = jnp.concatenate([qz, qaug], axis=1)
            s_ref[qi % 2, m, :, 0:nk] = _qk(qa, kf_ref[0:nk, :])

    def finish(qi):
        nk = (qi + 1) * tq
        b = qi % 2
        outs = []
        for m in range(2):
            s_ref[b, m, :, nk - tq:nk] += corr
            mx = jnp.max(s_ref[b, m, :, 0:nk], axis=-1, keepdims=True)
            p_ref[m, :, 0:nk] = jnp.exp(s_ref[b, m, :, 0:nk] - mx).astype(BF16)
            outs.append(jnp.dot(p_ref[m, :, 0:nk], vf_ref[0:nk, :], preferred_element_type=F32))
        o1, o2 = outs
        c1 = 1.0 / o1[:, V_DIM:V_DIM + 1]
        c2 = lam / o2[:, V_DIM:V_DIM + 1]
        o = o1[:, 0:V_DIM] * c1 - o2[:, 0:V_DIM] * c2
        o_ref[0, qi * tq:(qi + 1) * tq, :] = _subln(o, sg_ref).astype(BF16)

    scores(0)
    for qi in range(nq):
        if qi + 1 < nq:
            scores(qi + 1)
        finish(qi)


def _attn_prompt(q, kb, vb, slopes, lam, sg):
    batch, t_len, width = q.shape
    tq = min(ATTN_TILE, t_len)
    nq = t_len // tq
    pos = jnp.arange(t_len, dtype=jnp.int32)
    ka = jnp.zeros((t_len, LANES), F32)
    ka = ka.at[:, 0:2].set(1.0).at[:, 2].set(((pos >> 8) << 8).astype(F32)).at[:, 3].set((pos & 255).astype(F32))
    smem = pl.BlockSpec(memory_space=pltpu.SMEM)
    seq = pl.BlockSpec((1, t_len, V_DIM), lambda b, h: (b, 0, h))
    return pl.pallas_call(
        functools.partial(_attn_body, tq=tq, nq=nq),
        grid=(batch, N_HEADS),
        in_specs=[
            smem, smem, seq, seq, seq,
            pl.BlockSpec((t_len, LANES), lambda b, h: (0, 0)),
            pl.BlockSpec((1, V_DIM), lambda b, h: (0, 0)),
        ],
        out_specs=seq,
        out_shape=jax.ShapeDtypeStruct((batch, t_len, width), BF16),
        scratch_shapes=[
            pltpu.VMEM((t_len, V_DIM + LANES), BF16),
            pltpu.VMEM((t_len, V_DIM + LANES), BF16),
            pltpu.VMEM((2, 2, tq, t_len), F32),
            pltpu.VMEM((2, tq, t_len), BF16),
        ],
        compiler_params=_cparams(("arbitrary", "arbitrary")),
        name="attn_prompt",
    )(slopes, lam, q, kb, vb, ka.astype(BF16), sg)


def _attn_dec_body(slope_ref, lam_ref, q_ref, kc_ref, vc_ref, kn_ref, vn_ref, sg_ref, o_ref, *, past):
    lam = lam_ref[0]
    tq = q_ref.shape[1]
    qpos_a = lax.broadcasted_iota(jnp.int32, (tq, past), 0) + past
    kpos_a = lax.broadcasted_iota(jnp.int32, (tq, past), 1)
    qpos_b = lax.broadcasted_iota(jnp.int32, (tq, tq), 0) + past
    kpos_b = lax.broadcasted_iota(jnp.int32, (tq, tq), 1) + past

    def dist_and_vis(qpos, kpos):
        vis = lax.shift_right_logical(kpos, 6) <= lax.shift_right_logical(qpos, 6)
        return jnp.abs(qpos - kpos).astype(F32), vis

    dist_a, vis_a = dist_and_vis(qpos_a, kpos_a)
    dist_b, vis_b = dist_and_vis(qpos_b, kpos_b)

    for h in range(N_HEADS):
        cs = slice(h * V_DIM, (h + 1) * V_DIM)
        slope = slope_ref[h]
        bias_a = jnp.where(vis_a, -slope * dist_a, NEG_INF)
        bias_b = jnp.where(vis_b, -slope * dist_b, NEG_INF)
        q1z, q2z = _half_masks(q_ref[0, :, cs])
        kc = kc_ref[0, pl.ds(h, past, stride=N_HEADS), :].astype(BF16)
        vc = vc_ref[0, pl.ds(h, past, stride=N_HEADS), :].astype(BF16)
        kn = kn_ref[0, :, cs]

        def softmax_parts(qz):
            sa = _qk(qz, kc) + bias_a
            sb = _qk(qz, kn) + bias_b
            m = jnp.maximum(jnp.max(sa, axis=-1, keepdims=True), jnp.max(sb, axis=-1, keepdims=True))
            pa = jnp.exp(sa - m)
            pb = jnp.exp(sb - m)
            l = jnp.sum(pa, axis=-1, keepdims=True) + jnp.sum(pb, axis=-1, keepdims=True)
            return pa, pb, l

        pa1, pb1, l1 = softmax_parts(q1z)
        pa2, pb2, l2 = softmax_parts(q2z)
        c1 = 1.0 / l1
        c2 = lam / l2
        wa = (pa1 * c1 - pa2 * c2).astype(BF16)
        wb = (pb1 * c1 - pb2 * c2).astype(BF16)
        o = (jnp.dot(wa, vc, preferred_element_type=F32)
             + jnp.dot(wb, vn_ref[0, :, cs], preferred_element_type=F32))
        o_ref[0, :, cs] = _subln(o, sg_ref).astype(BF16)


def _attn_sample(q, kn, vn, cache_k, cache_v, slopes, lam, sg):
    batch, tq, width = q.shape
    past = cache_k.shape[1] // N_HEADS
    smem = pl.BlockSpec(memory_space=pltpu.SMEM)
    new = pl.BlockSpec((1, tq, width), lambda b: (b, 0, 0))
    old = pl.BlockSpec((1, past * N_HEADS, V_DIM), lambda b: (b, 0, 0))
    return pl.pallas_call(
        functools.partial(_attn_dec_body, past=past),
        grid=(batch,),
        in_specs=[smem, smem, new, old, old, new, new,
                  pl.BlockSpec((1, V_DIM), lambda b: (0, 0))],
        out_specs=new,
        out_shape=jax.ShapeDtypeStruct((batch, tq, width), BF16),
        compiler_params=_cparams(("arbitrary",)),
        name="attn_sample",
    )(slopes, lam, q, cache_k, cache_v, kn, vn, sg)


def _outproj_body(at_ref, sa_ref, gp_ref, x_ref, wo_ref, g_ref, rw_ref, rb_ref,
                  h_ref, ti_ref, tg_ref, cnt_ref):
    i = pl.program_id(0)
    rows = x_ref.shape[0]
    n_sub = 2 if rows % (2 * LANES) == 0 else 1
    sub = rows // n_sub

    def project(j):
        rs = slice(j * sub, (j + 1) * sub)
        merged = (sa_ref[rs, :].astype(F32) * at_ref[rs, :].astype(F32) + gp_ref[rs, :].astype(F32)).astype(BF16)
        return x_ref[rs, :] + jnp.dot(merged, wo_ref[...], preferred_element_type=F32)

    def route(hh, j):
        for c in range(ROW_SUB):
            h_ref[pl.ds(j * sub * ROW_SUB + c, sub, stride=ROW_SUB), :] = hh[:, c * LANES:(c + 1) * LANES]
        ms = jnp.mean(hh * hh, axis=-1, keepdims=True)
        hn = (hh * lax.rsqrt(ms + RMS_EPS) * g_ref[...]).astype(BF16)
        logits = jnp.dot(hn, rw_ref[...], preferred_element_type=F32) + rb_ref[...]
        lt = jnp.transpose(logits)[0:N_EXPERTS, :]
        e_iota = lax.broadcasted_iota(jnp.int32, (N_EXPERTS, sub), 0)
        vals, idxs, hots = [], [], []
        cur = lt
        for _ in range(TOP_K):
            m = jnp.max(cur, axis=0, keepdims=True)
            idx = jnp.min(jnp.where(cur == m, e_iota, N_EXPERTS), axis=0, keepdims=True)
            hit = e_iota == idx
            vals.append(m)
            idxs.append(idx)
            hots.append(hit)
            cur = jnp.where(hit, -jnp.inf, cur)
        ex = [jnp.exp(v - vals[0]) for v in vals]
        inv = 1.0 / (ex[0] + ex[1] + ex[2] + ex[3])
        zi = jnp.zeros((8 - TOP_K, sub), jnp.int32)
        zf = jnp.zeros((8 - TOP_K, sub), F32)
        cs = slice(j * sub, (j + 1) * sub)
        ti_ref[:, cs] = jnp.concatenate(idxs + [zi], axis=0)
        tg_ref[:, cs] = jnp.concatenate([e * inv for e in ex] + [zf], axis=0)
        hot = jnp.concatenate([hh_.astype(F32) for hh_ in hots], axis=0)
        return jnp.sum(hot, axis=1, keepdims=True)

    hs = [project(j) for j in range(n_sub)]
    csum = route(hs[0], 0)
    for j in range(1, n_sub):
        csum = csum + route(hs[j], j)

    @pl.when(i == 0)
    def _():
        cnt_ref[...] = jnp.zeros_like(cnt_ref)

    cnt_ref[...] += jnp.broadcast_to(csum, cnt_ref.shape)


def _outproj(attn, sa, gp, x, w_out, g, rw, rb):
    n_tok, d_model = x.shape
    tm = min(2 * ROW_TILE, n_tok)
    row = lambda i: (i, 0)
    fixed = lambda i: (0, 0)
    colb = lambda i: (0, i)
    return pl.pallas_call(
        _outproj_body,
        grid=(n_tok // tm,),
        in_specs=[
            pl.BlockSpec((tm, d_model), row),
            pl.BlockSpec((tm, d_model), row),
            pl.BlockSpec((tm, d_model), row),
            pl.BlockSpec((tm, d_model), row),
            pl.BlockSpec((d_model, d_model), fixed),
            pl.BlockSpec((1, d_model), fixed),
            pl.BlockSpec((d_model, LANES), fixed),
            pl.BlockSpec((1, LANES), fixed),
        ],
        out_specs=[
            pl.BlockSpec((tm * ROW_SUB, LANES), row),
            pl.BlockSpec((8, tm), colb),
            pl.BlockSpec((8, tm), colb),
            pl.BlockSpec((TOP_K * N_EXPERTS, LANES), fixed),
        ],
        out_shape=[
            jax.ShapeDtypeStruct((n_tok * ROW_SUB, LANES), F32),
            jax.ShapeDtypeStruct((8, n_tok), jnp.int32),
            jax.ShapeDtypeStruct((8, n_tok), F32),
            jax.ShapeDtypeStruct((TOP_K * N_EXPERTS, LANES), F32),
        ],
        compiler_params=_cparams(("arbitrary",)),
        name="outproj",
    )(attn, sa, gp, x, w_out, g, rw, rb)


def _rank_body(ti_ref, base_ref, tri_ref, dest_ref, carry_ref):
    i = pl.program_id(0)

    @pl.when(i == 0)
    def _():
        carry_ref[...] = jnp.zeros_like(carry_ref)

    tt = ti_ref.shape[1]
    e_iota = lax.broadcasted_iota(jnp.int32, (N_EXPERTS, tt), 0)
    hot = jnp.concatenate([(ti_ref[k:k + 1, :] == e_iota).astype(F32) for k in range(TOP_K)], axis=0)
    incl = jnp.dot(hot.astype(BF16), tri_ref[...], preferred_element_type=F32)
    slot = base_ref[:, 0:1] + carry_ref[:, 0:1] + incl - 1.0
    picked = hot * slot
    rows = [jnp.sum(picked[k * N_EXPERTS:(k + 1) * N_EXPERTS, :], axis=0, keepdims=True)
            for k in range(TOP_K)]
    rows.append(jnp.zeros((8 - TOP_K, tt), F32))
    dest_ref[...] = jnp.concatenate(rows, axis=0).astype(jnp.int32)
    carry_ref[...] += jnp.broadcast_to(jnp.sum(hot, axis=1, keepdims=True), carry_ref.shape)


def _rank(topi, base, tri):
    n_tok = topi.shape[1]
    tt = tri.shape[0]
    return pl.pallas_call(
        _rank_body,
        grid=(n_tok // tt,),
        in_specs=[
            pl.BlockSpec((8, tt), lambda i: (0, i)),
            pl.BlockSpec(base.shape, lambda i: (0, 0)),
            pl.BlockSpec((tt, tt), lambda i: (0, 0)),
        ],
        out_specs=pl.BlockSpec((8, tt), lambda i: (0, i)),
        out_shape=jax.ShapeDtypeStruct((8, n_tok), jnp.int32),
        scratch_shapes=[pltpu.VMEM((TOP_K * N_EXPERTS, LANES), F32)],
        compiler_params=_cparams(("arbitrary",)),
        name="rank",
    )(topi, base, tri)


DISPATCH_BUFS = 3


def _slab_copy(src, src_row, dst, dst_row, sem):
    def first_sublane(row):
        return row * ROW_SUB if isinstance(row, int) else pl.multiple_of(row * ROW_SUB, ROW_SUB)

    return pltpu.make_async_copy(src.at[pl.ds(first_sublane(src_row), ROW_SUB)],
                                 dst.at[pl.ds(first_sublane(dst_row), ROW_SUB)], sem)


def _dispatch_body(zs_ref, zl_ref, tz_ref, dest_hbm, hp_hbm, hs_hbm, xs_hbm,
                   idx0, idx1, idx2, hb0, hb1, hb2, lsem, dsem, zsem,
                   *, tt, n_prompt_tiles, n_out_tiles):
    i = pl.program_id(0)
    n = pl.num_programs(0)
    idx = (idx0, idx1, idx2)
    hbuf = (hb0, hb1, hb2)
    nbuf = DISPATCH_BUFS
    tile_sub = tt * ROW_SUB

    def load_wait(s):
        pltpu.make_async_copy(dest_hbm.at[0], idx[s], lsem.at[s]).wait()
        pltpu.make_async_copy(hp_hbm.at[pl.ds(0, tile_sub)], hbuf[s], lsem.at[s]).wait()

    def load_start(tile, s):
        pltpu.make_async_copy(dest_hbm.at[tile], idx[s], lsem.at[s]).start()

        @pl.when(tile < n_prompt_tiles)
        def _():
            r0 = pl.multiple_of(tile * tile_sub, tile_sub)
            pltpu.make_async_copy(hp_hbm.at[pl.ds(r0, tile_sub)], hbuf[s], lsem.at[s]).start()

        @pl.when(tile >= n_prompt_tiles)
        def _():
            r0 = pl.multiple_of((tile - n_prompt_tiles) * tile_sub, tile_sub)
            pltpu.make_async_copy(hs_hbm.at[pl.ds(r0, tile_sub)], hbuf[s], lsem.at[s]).start()

    def scatter_start(s):
        for k in range(TOP_K):
            for r in range(tt):
                _slab_copy(hbuf[s], r, xs_hbm, idx[s][k * tt + r], dsem.at[s]).start(priority=r % 2)

    def scatter_wait(s):
        for _ in range(TOP_K):
            pltpu.make_async_copy(hbuf[s], xs_hbm.at[pl.ds(0, tile_sub)], dsem.at[s]).wait()

    @pl.when(i == 0)
    def _():
        zero = hbuf[nbuf - 1]
        zero[...] = jnp.zeros_like(zero)

        def per_expert(e, c):
            def one(r, c2):
                _slab_copy(zero, 0, xs_hbm, zs_ref[e] + r, zsem).start()
                return c2
            lax.fori_loop(0, zl_ref[e], one, 0)

            def one_wait(r, c2):
                _slab_copy(zero, 0, xs_hbm, zs_ref[e] + r, zsem).wait()
                return c2
            lax.fori_loop(0, zl_ref[e], one_wait, 0)
            return c
        lax.fori_loop(0, N_EXPERTS, per_expert, 0)

        def tail_copy(j):
            r0 = pl.multiple_of(j * tile_sub, tile_sub)
            return pltpu.make_async_copy(zero, xs_hbm.at[pl.ds(r0, tile_sub)], zsem)

        def tail_one(j, c):
            tail_copy(j).start()
            tail_copy(j).wait()
            return c
        lax.fori_loop(tz_ref[0], n_out_tiles, tail_one, 0)

        load_start(0, 0)

        @pl.when(n > 1)
        def _():
            load_start(1, 1)

    for s in range(nbuf):
        @pl.when(i % nbuf == s)
        def _(s=s):
            prev = (s + nbuf - 1) % nbuf
            load_wait(s)
            scatter_start(s)

            @pl.when(i > 0)
            def _():
                scatter_wait(prev)

            @pl.when(i + 2 < n)
            def _():
                load_start(i + 2, prev)

            @pl.when(i == n - 1)
            def _():
                scatter_wait(s)


def _dispatch(zero_start, zero_len, tail_tile, dest_tiles, hp, hs, n_rows):
    n_tiles, width = dest_tiles.shape
    tt = width // TOP_K
    n_prompt_tiles = hp.shape[0] // (tt * ROW_SUB)
    any_spec = pl.BlockSpec(memory_space=pl.ANY)
    grid_spec = pltpu.PrefetchScalarGridSpec(
        num_scalar_prefetch=3,
        grid=(n_tiles,),
        in_specs=[any_spec, any_spec, any_spec],
        out_specs=any_spec,
        scratch_shapes=[pltpu.SMEM((width,), jnp.int32)] * DISPATCH_BUFS
        + [pltpu.VMEM((tt * ROW_SUB, LANES), F32)] * DISPATCH_BUFS
        + [
            pltpu.SemaphoreType.DMA((DISPATCH_BUFS,)),
            pltpu.SemaphoreType.DMA((DISPATCH_BUFS,)),
            pltpu.SemaphoreType.DMA,
        ],
    )
    return pl.pallas_call(
        functools.partial(_dispatch_body, tt=tt, n_prompt_tiles=n_prompt_tiles, n_out_tiles=n_rows // tt),
        grid_spec=grid_spec,
        out_shape=jax.ShapeDtypeStruct((n_rows * ROW_SUB, LANES), F32),
        compiler_params=_cparams(("arbitrary",)),
        name="dispatch",
    )(zero_start, zero_len, tail_tile, dest_tiles, hp, hs)


def _moe_body(te_ref, nu_ref, nx_ref, x_ref, g_ref, wg_hbm, bg_ref, wu_hbm, bu_ref, wd_hbm, bd_ref,
              y_ref, wgs, wus, wds, wgb, wub, wdb, wsem, *, tm):
    i = pl.program_id(0)
    n_used = nu_ref[0]

    def weight_copies(e):
        return (pltpu.make_async_copy(wg_hbm.at[e], wgs, wsem),
                pltpu.make_async_copy(wu_hbm.at[e], wus, wsem),
                pltpu.make_async_copy(wd_hbm.at[e], wds, wsem))

    @pl.when(i >= n_used)
    def _():
        y_ref[...] = jnp.zeros_like(y_ref)

    @pl.when(i < n_used)
    def _():
        @pl.when(i == 0)
        def _():
            for cp in weight_copies(te_ref[0]):
                cp.start()

        prev = te_ref[jnp.maximum(i - 1, 0)]
        changed = jnp.logical_or(i == 0, te_ref[i] != prev)

        @pl.when(changed)
        def _():
            for cp in weight_copies(te_ref[i]):
                cp.wait()
            wgb[...] = wgs[...].astype(BF16)
            wub[...] = wus[...].astype(BF16)
            wdb[...] = wds[...].astype(BF16)

            @pl.when(nx_ref[i] >= 0)
            def _():
                for cp in weight_copies(nx_ref[i]):
                    cp.start()

        half = tm // 2
        d_model = ROW_SUB * LANES

        def normed_half(r0):
            cols = [x_ref[pl.ds(r0 * ROW_SUB + c, half, stride=ROW_SUB), :] for c in range(ROW_SUB)]
            ssq = cols[0] * cols[0]
            for xc in cols[1:]:
                ssq = ssq + xc * xc
            r = lax.rsqrt(jnp.sum(ssq, axis=-1, keepdims=True) * (1.0 / d_model) + RMS_EPS)
            return jnp.concatenate(
                [(xc * r * g_ref[:, c * LANES:(c + 1) * LANES]).astype(BF16) for c, xc in enumerate(cols)], axis=1)

        def gate_up(xb):
            gt = jnp.dot(xb, wgb[...], preferred_element_type=F32) + bg_ref[0]
            up = jnp.dot(xb, wub[...], preferred_element_type=F32) + bu_ref[0]
            return gt, up

        def down(gt, up, r0):
            gt = jnp.minimum(gt, SWIGLU_LIMIT)
            up = jnp.clip(up, -SWIGLU_LIMIT, SWIGLU_LIMIT)
            hdn = (up + 1.0) * (gt * _sigmoid(SWIGLU_ALPHA * gt))
            y = jnp.dot(hdn.astype(BF16), wdb[...], preferred_element_type=F32) + bd_ref[0]
            for c in range(ROW_SUB):
                y_ref[pl.ds(r0 * ROW_SUB + c, half, stride=ROW_SUB), :] = y[:, c * LANES:(c + 1) * LANES]

        ga, ua = gate_up(normed_half(0))
        gb, ub = gate_up(normed_half(half))
        down(ga, ua, 0)
        down(gb, ub, half)


def _moe(tile_expert, n_used, next_expert, x_rows, g, w_gate, b_gate, w_up, b_up, w_down, b_down, *, tm):
    n_tiles = x_rows.shape[0] // (tm * ROW_SUB)
    d_model = ROW_SUB * LANES
    d_ff = w_gate.shape[2]
    assert w_gate.shape[1] == d_model
    bspec = lambda width: pl.BlockSpec((1, 1, width), lambda i, te, nu, nx: (te[i], 0, 0))
    any_spec = pl.BlockSpec(memory_space=pl.ANY)
    grid_spec = pltpu.PrefetchScalarGridSpec(
        num_scalar_prefetch=3,
        grid=(n_tiles,),
        in_specs=[
            pl.BlockSpec((tm * ROW_SUB, LANES), lambda i, te, nu, nx: (jnp.minimum(i, nu[0] - 1), 0)),
            pl.BlockSpec((1, d_model), lambda i, te, nu, nx: (0, 0)),
            any_spec, bspec(d_ff),
            any_spec, bspec(d_ff),
            any_spec, bspec(d_model),
        ],
        out_specs=pl.BlockSpec((tm * ROW_SUB, LANES), lambda i, te, nu, nx: (i, 0)),
        scratch_shapes=[
            pltpu.VMEM((d_model, d_ff), F32),
            pltpu.VMEM((d_model, d_ff), F32),
            pltpu.VMEM((d_ff, d_model), F32),
            pltpu.VMEM((d_model, d_ff), BF16),
            pltpu.VMEM((d_model, d_ff), BF16),
            pltpu.VMEM((d_ff, d_model), BF16),
            pltpu.SemaphoreType.DMA,
        ],
    )
    return pl.pallas_call(
        functools.partial(_moe_body, tm=tm),
        grid_spec=grid_spec,
        out_shape=jax.ShapeDtypeStruct((n_tiles * tm * ROW_SUB, LANES), F32),
        compiler_params=_cparams(("arbitrary",)),
        name="moe",
    )(tile_expert, n_used, next_expert, x_rows, g,
      w_gate, b_gate.reshape(N_EXPERTS, 1, d_ff), w_up, b_up.reshape(N_EXPERTS, 1, d_ff),
      w_down, b_down.reshape(N_EXPERTS, 1, d_model))


def _combine_body(dest_hbm, y_hbm, h_ref, gt_ref, o_ref, idx0, idx1, yb0, yb1, isem, gsem, *, tt):
    i = pl.program_id(0)
    n = pl.num_programs(0)
    idx = (idx0, idx1)
    ybuf = (yb0, yb1)

    def idx_copy(tile, s):
        return pltpu.make_async_copy(dest_hbm.at[tile], idx[s], isem.at[s])

    def issue_gather(s):
        for k in range(TOP_K):
            for r in range(tt):
                src = pl.multiple_of(idx[s][k * tt + r] * ROW_SUB, ROW_SUB)
                pltpu.make_async_copy(y_hbm.at[pl.ds(src, ROW_SUB)],
                                      ybuf[s].at[k, pl.ds(r * ROW_SUB, ROW_SUB)], gsem.at[s]).start(priority=r % 2)

    def wait_gather(s):
        for k in range(TOP_K):
            pltpu.make_async_copy(y_hbm.at[pl.ds(0, tt * ROW_SUB)], ybuf[s].at[k], gsem.at[s]).wait()

    @pl.when(i == 0)
    def _():
        idx_copy(0, 0).start()
        idx_copy(0, 0).wait()
        issue_gather(0)
        idx_copy(1, 1).start()

    for s in range(2):
        @pl.when(i % 2 == s)
        def _(s=s):
            o = 1 - s
            idx_copy(i + 1, o).wait()
            issue_gather(o)
            idx_copy(i + 2, s).start()
            wait_gather(s)
            gts = gt_ref[...]
            for c in range(ROW_SUB):
                cs = slice(c * LANES, (c + 1) * LANES)
                acc = _load_slab_cols(h_ref, tt, c)
                for k in range(TOP_K):
                    acc = acc + gts[:, k:k + 1] * ybuf[s][k, pl.ds(c, tt, stride=ROW_SUB), :]
                o_ref[:, cs] = acc

            @pl.when(i == n - 1)
            def _():
                wait_gather(o)
                idx_copy(i + 2, s).wait()


def _combine(dest_tiles, y_rows, h, gates_tok):
    n_tiles, width = dest_tiles.shape[0] - 2, dest_tiles.shape[1]
    tt = width // TOP_K
    n_tok, d_model = h.shape[0] // ROW_SUB, ROW_SUB * LANES
    any_spec = pl.BlockSpec(memory_space=pl.ANY)
    return pl.pallas_call(
        functools.partial(_combine_body, tt=tt),
        grid=(n_tiles,),
        in_specs=[any_spec, any_spec,
                  pl.BlockSpec((tt * ROW_SUB, LANES), lambda i: (i, 0)),
                  pl.BlockSpec((tt, 8), lambda i: (i, 0))],
        out_specs=pl.BlockSpec((tt, d_model), lambda i: (i, 0)),
        out_shape=jax.ShapeDtypeStruct((n_tok, d_model), F32),
        scratch_shapes=[
            pltpu.SMEM((TOP_K * tt,), jnp.int32),
            pltpu.SMEM((TOP_K * tt,), jnp.int32),
            pltpu.VMEM((TOP_K, tt * ROW_SUB, LANES), F32),
            pltpu.VMEM((TOP_K, tt * ROW_SUB, LANES), F32),
            pltpu.SemaphoreType.DMA((2,)),
            pltpu.SemaphoreType.DMA((2,)),
        ],
        compiler_params=_cparams(("arbitrary",)),
        name="combine",
    )(dest_tiles, y_rows, h, gates_tok)


def _block_diag_ones():
    r = lax.broadcasted_iota(jnp.int32, (NORM_BLOCK, NORM_BLOCK), 0) // HEAD_DIM
    c = lax.broadcasted_iota(jnp.int32, (NORM_BLOCK, NORM_BLOCK), 1) // HEAD_DIM
    return (r == c).astype(BF16)


def _dest_tiles(dest, tt):
    n_tok = dest.shape[1]
    tiles = dest[:TOP_K].reshape(TOP_K, n_tok // tt, tt).transpose(1, 0, 2).reshape(n_tok // tt, TOP_K * tt)
    return jnp.pad(tiles, ((0, 2), (0, 0)))


def _layer(xp, xs, ck, cv, sp, lam, lam_init, ng, w_in, qng, kng, slg, pw, ps, w_out, fg,
           rw, rb, w_gate, b_gate, w_up, b_up, w_down, b_down):
    bp, t_len, d_model = xp.shape
    bs, ts, _ = xs.shape
    past = ck.shape[1]
    pool_w = sp.shape[-1]
    n_p, n_s = bp * t_len, bs * ts

    w_in_b = w_in.astype(BF16)
    w_out_b = w_out.astype(BF16)
    pw_b = pw.astype(BF16)
    reps = NORM_BLOCK // HEAD_DIM
    qg = (jnp.tile(qng.astype(F32), reps) * (HEAD_DIM ** -0.5)).reshape(1, NORM_BLOCK)
    kg = jnp.tile(kng.astype(F32), reps).reshape(1, NORM_BLOCK)
    bd = _block_diag_ones()
    ng2 = ng.astype(F32).reshape(1, d_model)
    ps2 = ps.astype(F32).reshape(1, d_model)
    sg = (slg.astype(F32) * (1.0 - lam_init)).reshape(1, V_DIM)
    slopes = jnp.exp2(-(8.0 / N_HEADS) * jnp.arange(1, N_HEADS + 1, dtype=F32))
    lam1 = lam.reshape(1).astype(F32)
    fg2 = fg.astype(F32).reshape(1, d_model)
    rw_b = jnp.zeros((d_model, LANES), BF16).at[:, :N_EXPERTS].set(rw.astype(BF16))
    rb2 = jnp.zeros((1, LANES), F32).at[0, :N_EXPERTS].set(rb.astype(F32))

    tm = min(ROW_TILE, t_len)
    zero_pre = jnp.zeros((bp, POOL_HALO, pool_w), F32)
    qp, kp, kpb, vp, vpb, sap, gpp, utp = _inproj(
        xp, zero_pre, ng2, w_in_b, qg, kg, bd, pw_b, ps2, nseg=1, seg_len=tm, start_pos=0, carry=True)
    atp = _attn_prompt(qp, kpb, vpb, slopes, lam1, sg)
    hp, tip, tgp, cntp = _outproj(atp.reshape(n_p, d_model), sap.reshape(n_p, d_model),
                                  gpp.reshape(n_p, d_model), xp.reshape(n_p, d_model),
                                  w_out_b, fg2, rw_b, rb2)

    pre_s = jnp.concatenate([jnp.zeros((bs, POOL_HALO - sp.shape[1], pool_w), F32), sp.astype(F32)], axis=1)
    qs, ks, ksb, vs, vsb, sas, gps, uts = _inproj(
        xs.reshape(1, n_s, d_model), pre_s, ng2, w_in_b, qg, kg, bd, pw_b, ps2,
        nseg=bs, seg_len=ts, start_pos=past, carry=False)
    ats = _attn_sample(qs.reshape(bs, ts, d_model), ksb.reshape(bs, ts, d_model), vsb.reshape(bs, ts, d_model),
                       ck.reshape(bs, past * N_HEADS, V_DIM), cv.reshape(bs, past * N_HEADS, V_DIM),
                       slopes, lam1, sg)
    hs, tis, tgs, cnts = _outproj(ats.reshape(n_s, d_model), sas.reshape(n_s, d_model),
                                  gps.reshape(n_s, d_model), xs.reshape(n_s, d_model),
                                  w_out_b, fg2, rw_b, rb2)

    n_tok = n_p + n_s
    topi = jnp.concatenate([tip, tis], axis=1)
    gates = jnp.concatenate([tgp, tgs], axis=1)
    cnt = (cntp[:, 0] + cnts[:, 0]).astype(jnp.int32).reshape(TOP_K, N_EXPERTS)
    per_expert = jnp.sum(cnt, axis=0)
    padded = (per_expert + MOE_TILE - 1) // MOE_TILE * MOE_TILE
    pad_end = jnp.cumsum(padded)
    pad_start = pad_end - padded
    base = pad_start[None, :] + jnp.cumsum(cnt, axis=0) - cnt
    base_f = jnp.broadcast_to(base.reshape(-1, 1).astype(F32), (TOP_K * N_EXPERTS, LANES))
    tt = TOK_TILE
    rt = max(m * tt for m in range(1, RANK_TILE_MAX // tt + 1) if n_tok % (m * tt) == 0)
    tri = (lax.broadcasted_iota(jnp.int32, (rt, rt), 0) <= lax.broadcasted_iota(jnp.int32, (rt, rt), 1)).astype(BF16)
    dest = _rank(topi, base_f, tri)

    n_tiles = -(-(n_tok * TOP_K) // MOE_TILE) + N_EXPERTS
    n_used = (pad_end[-1] // MOE_TILE).astype(jnp.int32)
    tile_start = jnp.arange(n_tiles, dtype=jnp.int32) * MOE_TILE
    last_start = jnp.maximum(pad_end[-1] - MOE_TILE, 0)
    tile_expert = jnp.minimum(
        jnp.sum((jnp.minimum(tile_start, last_start)[:, None] >= pad_end[None, :]).astype(jnp.int32), axis=1),
        N_EXPERTS - 1)
    dest_p = _dest_tiles(dest[:, :n_p], tt)
    dest_s = _dest_tiles(dest[:, n_p:], tt)
    dest_all = jnp.concatenate([dest_p[:-2], dest_s[:-2]], axis=0)
    x_rows = _dispatch(pad_start + per_expert, padded - per_expert, (pad_end[-1:] // tt).astype(jnp.int32),
                       dest_all, hp, hs, n_tiles * MOE_TILE)
    e_ids = jnp.arange(N_EXPERTS, dtype=jnp.int32)
    later_present = jnp.logical_and(e_ids[None, :] > e_ids[:, None], (padded > 0)[None, :])
    next_of = jnp.min(jnp.where(later_present, e_ids[None, :], N_EXPERTS), axis=1)
    next_of = jnp.where(next_of == N_EXPERTS, -1, next_of).astype(jnp.int32)
    next_expert = jnp.sum(jnp.where(tile_expert[:, None] == e_ids[None, :], next_of[None, :], 0), axis=1)
    y_rows = _moe(tile_expert, n_used.reshape(1), next_expert.astype(jnp.int32), x_rows, fg2,
                  w_gate, b_gate, w_up, b_up, w_down, b_down, tm=MOE_TILE)

    gates_tok = jnp.transpose(gates)
    yp = _combine(dest_p, y_rows, hp, gates_tok[:n_p])
    ys = _combine(dest_s, y_rows, hs, gates_tok[n_p:])

    heads = (N_HEADS, V_DIM)
    return (yp.reshape(bp, t_len, d_model), ys.reshape(bs, ts, d_model),
            kp.reshape(bp, t_len, *heads), vp.reshape(bp, t_len, *heads), utp[:, 1:],
            ks.reshape(bs, ts, *heads), vs.reshape(bs, ts, *heads), uts[:, 1:])


def kernel(x_prompt, x_sample, cache_k, cache_v, state_pool, norm_mix_g, w_in, q_norm_g, k_norm_g,
           lambda_q1, lambda_k1, lambda_q2, lambda_k2, subln_g, pool_w, pool_scale, w_out, norm_ffn_g,
           router_w, router_b, w_gate, b_gate, w_up, b_up, w_down, b_down):
    depth = w_in.shape[0]
    hp, hs = x_prompt, x_sample
    outs = [[] for _ in range(6)]
    for layer in range(depth):
        lam_init = 0.8 - 0.6 * math.exp(-0.3 * layer)
        lam = (jnp.exp(jnp.sum(lambda_q1[layer].astype(F32) * lambda_k1[layer].astype(F32)))
               - jnp.exp(jnp.sum(lambda_q2[layer].astype(F32) * lambda_k2[layer].astype(F32)))
               + lam_init)
        hp, hs, kp, vp, up, ks, vs, us = _layer(
            hp, hs, cache_k[layer], cache_v[layer], state_pool[layer], lam, lam_init,
            norm_mix_g[layer], w_in[layer], q_norm_g[layer], k_norm_g[layer], subln_g[layer],
            pool_w[layer], pool_scale[layer], w_out[layer], norm_ffn_g[layer],
            router_w[layer], router_b[layer], w_gate[layer], b_gate[layer], w_up[layer], b_up[layer],
            w_down[layer], b_down[layer])
        for lst, val in zip(outs, (kp, vp, up, ks, vs, us)):
            lst.append(val)
    return (hp, hs) + tuple(jnp.stack(o) for o in outs)
```

```python
import functools
import math

import jax
import jax.numpy as jnp
from jax import lax
from jax.experimental import pallas as pl
from jax.experimental.pallas import tpu as pltpu

F32 = jnp.float32
BF16 = jnp.bfloat16

CHUNK = 64
N_HEADS = 8
HEAD_DIM = 64
V_DIM = 2 * HEAD_DIM
POOL_WINDOWS = (2, 4, 8, 16)
POOL_GROUP_DIM = 128
POOL_OUT_DIM = 256
POOL_HALO = 16
N_EXPERTS = 32
TOP_K = 4
SWIGLU_LIMIT = 7.0
SWIGLU_ALPHA = 1.702
RMS_EPS = 1e-6
NEG_INF = -1e30

LANES = 128
ROW_SUB = 8
NORM_BLOCK = 256
VMEM_LIMIT = 56 * 1024 * 1024

ROW_TILE = 512
ATTN_TILE = 256
MOE_TILE = 512
TOK_TILE = 256
RANK_TILE_MAX = 1024


def _sigmoid(x):
    return 1.0 / (1.0 + jnp.exp(-x))


def _cparams(sem):
    return pltpu.CompilerParams(dimension_semantics=sem, vmem_limit_bytes=VMEM_LIMIT)


def _store_slabs(ref, val):
    rows = val.shape[0]
    for c in range(ROW_SUB):
        ref[pl.ds(c, rows, stride=ROW_SUB), :] = val[:, c * LANES:(c + 1) * LANES]


def _load_slab_cols(ref, rows, c):
    return ref[pl.ds(c, rows, stride=ROW_SUB), :]


def _inproj_body(x_ref, pre_ref, ng_ref, w_ref, qg_ref, kg_ref, bd_ref, pw_ref, ps_ref,
                 q_ref, k_ref, kb_ref, v_ref, vb_ref, sa_ref, gp_ref, ut_ref, ext_ref,
                 *, nseg, seg_len, start_pos, carry, attn_w, pool_w):
    rows = nseg * seg_len
    t = pl.program_id(1)
    x = x_ref[0]
    ms = jnp.mean(x * x, axis=-1, keepdims=True)
    xn = (x * lax.rsqrt(ms + RMS_EPS) * ng_ref[...]).astype(BF16)

    def proj(c0, width):
        return jnp.dot(xn, w_ref[:, c0:c0 + width], preferred_element_type=F32)

    bd = bd_ref[...]

    def group_norm(p, g_ref):
        ss = jnp.dot((p * p).astype(BF16), bd, preferred_element_type=F32)
        return p * lax.rsqrt(ss * (1.0 / HEAD_DIM) + RMS_EPS) * g_ref[...]

    nb = NORM_BLOCK
    heads_per_block = nb // V_DIM

    def store_heads(ref, val, c):
        for j in range(heads_per_block):
            head = c * heads_per_block + j
            ref[0, pl.ds(head, rows, stride=N_HEADS), :] = val[:, j * V_DIM:(j + 1) * V_DIM]

    def sink_q(val, c):
        q_ref[0, :, c * nb:(c + 1) * nb] = group_norm(val, qg_ref).astype(BF16)

    def sink_k(val, c):
        kn = group_norm(val, kg_ref)
        store_heads(k_ref, kn, c)
        kb_ref[0, :, c * nb:(c + 1) * nb] = kn.astype(BF16)

    def sink_v(val, c):
        store_heads(v_ref, val, c)
        vb_ref[0, :, c * nb:(c + 1) * nb] = val.astype(BF16)

    n_chunks = attn_w // nb
    work = [(part * attn_w + c * nb, sink, c)
            for part, sink in enumerate((sink_q, sink_k, sink_v)) for c in range(n_chunks)]
    pending = proj(work[0][0], nb)
    for j, (_, sink, c) in enumerate(work):
        cur = pending
        if j + 1 < len(work):
            pending = proj(work[j + 1][0], nb)
        sink(cur, c)

    if carry:
        @pl.when(t == 0)
        def _():
            ext_ref[:, 0:POOL_HALO, :] = pre_ref[...]
    else:
        ext_ref[:, 0:POOL_HALO, :] = pre_ref[...]
    for c in range(pool_w // nb):
        cs = slice(c * nb, (c + 1) * nb)
        u = proj(3 * attn_w + c * nb, nb)
        ext_ref[:, POOL_HALO:POOL_HALO + seg_len, cs] = u.reshape(nseg, seg_len, nb)

    ga0 = 3 * attn_w + pool_w
    d_model = attn_w
    for c in range(d_model // nb):
        cs = slice(c * nb, (c + 1) * nb)
        sa_ref[0, :, cs] = _sigmoid(proj(ga0 + c * nb, nb)).astype(BF16)

    gb0 = ga0 + d_model
    row = lax.broadcasted_iota(jnp.int32, (1, seg_len, 1), 1)
    pos = row + start_pos
    if carry:
        pos = pos + t * seg_len
    for g, w in enumerate(POOL_WINDOWS):
        cs = slice(g * POOL_GROUP_DIM, (g + 1) * POOL_GROUP_DIM)
        own = ext_ref[:, POOL_HALO:POOL_HALO + seg_len, cs]
        acc = own
        for i in range(1, w):
            acc = acc + ext_ref[:, POOL_HALO - i:POOL_HALO - i + seg_len, cs]
        inv = 1.0 / jnp.minimum(w, pos + 1).astype(F32)
        z = (acc * inv - own).reshape(rows, POOL_GROUP_DIM)
        os_ = slice(g * POOL_OUT_DIM, (g + 1) * POOL_OUT_DIM)
        yp = jnp.dot(z.astype(BF16), pw_ref[g], preferred_element_type=F32) * ps_ref[:, os_]
        gb = proj(gb0 + g * POOL_OUT_DIM, POOL_OUT_DIM)
        gp_ref[0, :, os_] = (_sigmoid(gb) * yp).astype(BF16)

    tail = ext_ref[:, seg_len:seg_len + POOL_HALO, :]
    ut_ref[...] = tail
    if carry:
        ext_ref[:, 0:POOL_HALO, :] = tail


def _inproj(x3, prefix, ng, w_in, qg, kg, bd, pw, ps, *, nseg, seg_len, start_pos, carry):
    groups, t_len, d_model = x3.shape
    rows = nseg * seg_len
    steps = t_len // rows
    in_cols = w_in.shape[1]
    pool_w = prefix.shape[-1]
    attn_w = d_model
    assert in_cols == 3 * attn_w + pool_w + 2 * d_model
    tok = lambda b, t: (b, t, 0)
    fixed2 = lambda b, t: (0, 0)
    act = lambda dt: jax.ShapeDtypeStruct((groups, t_len, d_model), dt)
    by_head = jax.ShapeDtypeStruct((groups, t_len * N_HEADS, V_DIM), F32)
    tok_spec = pl.BlockSpec((1, rows, d_model), tok)
    head_spec = pl.BlockSpec((1, rows * N_HEADS, V_DIM), tok)
    body = functools.partial(_inproj_body, nseg=nseg, seg_len=seg_len, start_pos=start_pos,
                             carry=carry, attn_w=attn_w, pool_w=pool_w)
    return pl.pallas_call(
        body,
        grid=(groups, steps),
        in_specs=[
            pl.BlockSpec((1, rows, d_model), tok),
            pl.BlockSpec((nseg, POOL_HALO, pool_w), lambda b, t: (b, 0, 0)),
            pl.BlockSpec((1, d_model), fixed2),
            pl.BlockSpec((d_model, in_cols), fixed2, pipeline_mode=pl.Buffered(1)),
            pl.BlockSpec((1, NORM_BLOCK), fixed2),
            pl.BlockSpec((1, NORM_BLOCK), fixed2),
            pl.BlockSpec((NORM_BLOCK, NORM_BLOCK), fixed2),
            pl.BlockSpec(pw.shape, lambda b, t: (0, 0, 0)),
            pl.BlockSpec((1, d_model), fixed2),
        ],
        out_specs=[tok_spec, head_spec, tok_spec, head_spec, tok_spec, tok_spec, tok_spec,
                   pl.BlockSpec((nseg, POOL_HALO, pool_w), lambda b, t: (b, 0, 0))],
        out_shape=[act(BF16), by_head, act(BF16), by_head, act(BF16), act(BF16), act(BF16),
                   jax.ShapeDtypeStruct(prefix.shape, F32)],
        scratch_shapes=[pltpu.VMEM((nseg, POOL_HALO + seg_len, pool_w), F32)],
        compiler_params=_cparams(("arbitrary", "arbitrary")),
        name="inproj",
    )(x3, prefix, ng, w_in, qg, kg, bd, pw, ps)


def _half_masks(q):
    lane = lax.broadcasted_iota(jnp.int32, q.shape, 1)
    zero = jnp.zeros_like(q)
    return jnp.where(lane < HEAD_DIM, q, zero), jnp.where(lane >= HEAD_DIM, q, zero)


def _qk(qz, kblk):
    return lax.dot_general(qz, kblk, (((1,), (1,)), ((), ())), preferred_element_type=F32)


def _subln(o, sg_ref):
    ms = jnp.mean(o * o, axis=-1, keepdims=True)
    return o * lax.rsqrt(ms + RMS_EPS) * sg_ref[...]


def _attn_body(slope_ref, lam_ref, q_ref, k_ref, v_ref, ka_ref, sg_ref, o_ref,
               kf_ref, vf_ref, s_ref, p_ref, *, tq, nq):
    h = pl.program_id(1)
    slope = slope_ref[h]
    lam = lam_ref[0]
    lane = lax.broadcasted_iota(jnp.int32, (tq, LANES), 1)
    kf_ref[:, 0:V_DIM] = k_ref[0]
    kf_ref[:, V_DIM:] = ka_ref[...]
    vf_ref[:, 0:V_DIM] = v_ref[0]
    t_len = vf_ref.shape[0]
    vf_ref[:, V_DIM:] = (lax.broadcasted_iota(jnp.int32, (t_len, LANES), 1) == 0).astype(BF16)

    row = lax.broadcasted_iota(jnp.int32, (tq, tq), 0)
    col = lax.broadcasted_iota(jnp.int32, (tq, tq), 1)
    rc = (row - col).astype(F32)
    vis = lax.shift_right_logical(col, 6) <= lax.shift_right_logical(row, 6)
    corr = jnp.where(vis, jnp.minimum(rc, 0.0) * (2.0 * slope), NEG_INF)

    def scores(qi, m):
        nk = (qi + 1) * tq
        qz = _half_masks(q_ref[0, qi * tq:(qi + 1) * tq, :])[m]
        t = lax.broadcasted_iota(jnp.int32, (tq, LANES), 0) + qi * tq
        t_hi = lax.shift_left(lax.shift_right_logical(t, 8), 8).astype(F32)
        t_lo = jnp.bitwise_and(t, 255).astype(F32)
        qaug = jnp.where(lane == 0, -slope * t_hi,
                         jnp.where(lane == 1, -slope * t_lo,
                                   jnp.where(lane < 4, slope, 0.0))).astype(BF16)
        qa = jnp.concatenate([qz, qaug], axis=1)
        s_ref[qi % 2, m, :, 0:nk] = _qk(qa, kf_ref[0:nk, :])

    def softmax_pv(qi, m):
        nk = (qi + 1) * tq
        b = qi % 2
        s_ref[b, m, :, nk - tq:nk] += corr
        mx = jnp.max(s_ref[b, m, :, 0:nk], axis=-1, keepdims=True)
        p_ref[m, :, 0:nk] = jnp.exp(s_ref[b, m, :, 0:nk] - mx).astype(BF16)
        return jnp.dot(p_ref[m, :, 0:nk], vf_ref[0:nk, :], preferred_element_type=F32)

    scores(0, 0)
    scores(0, 1)
    for qi in range(nq):
        if qi + 1 < nq:
            scores(qi + 1, 0)
            scores(qi + 1, 1)
        o1, o2 = softmax_pv(qi, 0), softmax_pv(qi, 1)
        c1 = 1.0 / o1[:, V_DIM:V_DIM + 1]
        c2 = lam / o2[:, V_DIM:V_DIM + 1]
        o = o1[:, 0:V_DIM] * c1 - o2[:, 0:V_DIM] * c2
        o_ref[0, qi * tq:(qi + 1) * tq, :] = _subln(o, sg_ref).astype(BF16)


def _attn_prompt(q, kb, vb, slopes, lam, sg):
    batch, t_len, width = q.shape
    tq = min(ATTN_TILE, t_len)
    nq = t_len // tq
    pos = jnp.arange(t_len, dtype=jnp.int32)
    ka = jnp.zeros((t_len, LANES), F32)
    ka = ka.at[:, 0:2].set(1.0).at[:, 2].set(((pos >> 8) << 8).astype(F32)).at[:, 3].set((pos & 255).astype(F32))
    smem = pl.BlockSpec(memory_space=pltpu.SMEM)
    seq = pl.BlockSpec((1, t_len, V_DIM), lambda b, h: (b, 0, h))
    return pl.pallas_call(
        functools.partial(_attn_body, tq=tq, nq=nq),
        grid=(batch, N_HEADS),
        in_specs=[
            smem, smem, seq, seq, seq,
            pl.BlockSpec((t_len, LANES), lambda b, h: (0, 0)),
            pl.BlockSpec((1, V_DIM), lambda b, h: (0, 0)),
        ],
        out_specs=seq,
        out_shape=jax.ShapeDtypeStruct((batch, t_len, width), BF16),
        scratch_shapes=[
            pltpu.VMEM((t_len, V_DIM + LANES), BF16),
            pltpu.VMEM((t_len, V_DIM + LANES), BF16),
            pltpu.VMEM((2, 2, tq, t_len), F32),
            pltpu.VMEM((2, tq, t_len), BF16),
        ],
        compiler_params=_cparams(("arbitrary", "arbitrary")),
        name="attn_prompt",
    )(slopes, lam, q, kb, vb, ka.astype(BF16), sg)


def _attn_dec_body(slope_ref, lam_ref, q_ref, kc_ref, vc_ref, kn_ref, vn_ref, sg_ref, o_ref, *, past):
    lam = lam_ref[0]
    tq = q_ref.shape[1]
    qpos_a = lax.broadcasted_iota(jnp.int32, (tq, past), 0) + past
    kpos_a = lax.broadcasted_iota(jnp.int32, (tq, past), 1)
    qpos_b = lax.broadcasted_iota(jnp.int32, (tq, tq), 0) + past
    kpos_b = lax.broadcasted_iota(jnp.int32, (tq, tq), 1) + past

    def dist_and_vis(qpos, kpos):
        vis = lax.shift_right_logical(kpos, 6) <= lax.shift_right_logical(qpos, 6)
        return jnp.abs(qpos - kpos).astype(F32), vis

    dist_a, vis_a = dist_and_vis(qpos_a, kpos_a)
    dist_b, vis_b = dist_and_vis(qpos_b, kpos_b)

    for h in range(N_HEADS):
        cs = slice(h * V_DIM, (h + 1) * V_DIM)
        slope = slope_ref[h]
        bias_a = jnp.where(vis_a, -slope * dist_a, NEG_INF)
        bias_b = jnp.where(vis_b, -slope * dist_b, NEG_INF)
        q1z, q2z = _half_masks(q_ref[0, :, cs])
        kc = kc_ref[0, pl.ds(h, past, stride=N_HEADS), :].astype(BF16)
        vc = vc_ref[0, pl.ds(h, past, stride=N_HEADS), :].astype(BF16)
        kn = kn_ref[0, :, cs]

        qcat = jnp.concatenate([q1z, q2z], axis=0)
        sa_both = _qk(qcat, kc)
        sb_both = _qk(qcat, kn)

        def softmax_parts(m_idx):
            rs = slice(m_idx * tq, (m_idx + 1) * tq)
            sa = sa_both[rs, :] + bias_a
            sb = sb_both[rs, :] + bias_b
            m = jnp.maximum(jnp.max(sa, axis=-1, keepdims=True), jnp.max(sb, axis=-1, keepdims=True))
            pa = jnp.exp(sa - m)
            pb = jnp.exp(sb - m)
            l = jnp.sum(pa, axis=-1, keepdims=True) + jnp.sum(pb, axis=-1, keepdims=True)
            return pa, pb, l

        pa1, pb1, l1 = softmax_parts(0)
        pa2, pb2, l2 = softmax_parts(1)
        c1 = 1.0 / l1
        c2 = lam / l2
        wa = (pa1 * c1 - pa2 * c2).astype(BF16)
        wb = (pb1 * c1 - pb2 * c2).astype(BF16)
        o = (jnp.dot(wa, vc, preferred_element_type=F32)
             + jnp.dot(wb, vn_ref[0, :, cs], preferred_element_type=F32))
        o_ref[0, :, cs] = _subln(o, sg_ref).astype(BF16)


def _attn_sample(q, kn, vn, cache_k, cache_v, slopes, lam, sg):
    batch, tq, width = q.shape
    past = cache_k.shape[1] // N_HEADS
    smem = pl.BlockSpec(memory_space=pltpu.SMEM)
    new = pl.BlockSpec((1, tq, width), lambda b: (b, 0, 0))
    old = pl.BlockSpec((1, past * N_HEADS, V_DIM), lambda b: (b, 0, 0))
    return pl.pallas_call(
        functools.partial(_attn_dec_body, past=past),
        grid=(batch,),
        in_specs=[smem, smem, new, old, old, new, new,
                  pl.BlockSpec((1, V_DIM), lambda b: (0, 0))],
        out_specs=new,
        out_shape=jax.ShapeDtypeStruct((batch, tq, width), BF16),
        compiler_params=_cparams(("arbitrary",)),
        name="attn_sample",
    )(slopes, lam, q, cache_k, cache_v, kn, vn, sg)


def _outproj_body(at_ref, sa_ref, gp_ref, x_ref, wo_ref, g_ref, rw_ref, rb_ref,
                  h_ref, ti_ref, tg_ref, cnt_ref):
    i = pl.program_id(0)
    rows = x_ref.shape[0]
    n_sub = 2 if rows % (2 * LANES) == 0 else 1
    sub = rows // n_sub

    def project(j):
        rs = slice(j * sub, (j + 1) * sub)
        merged = (sa_ref[rs, :].astype(F32) * at_ref[rs, :].astype(F32) + gp_ref[rs, :].astype(F32)).astype(BF16)
        return x_ref[rs, :] + jnp.dot(merged, wo_ref[...], preferred_element_type=F32)

    def route(hh, j):
        for c in range(ROW_SUB):
            h_ref[pl.ds(j * sub * ROW_SUB + c, sub, stride=ROW_SUB), :] = hh[:, c * LANES:(c + 1) * LANES]
        ms = jnp.mean(hh * hh, axis=-1, keepdims=True)
        hn = (hh * lax.rsqrt(ms + RMS_EPS) * g_ref[...]).astype(BF16)
        logits = jnp.dot(hn, rw_ref[...], preferred_element_type=F32) + rb_ref[...]
        lt = jnp.transpose(logits)[0:N_EXPERTS, :]
        e_iota = lax.broadcasted_iota(jnp.int32, (N_EXPERTS, sub), 0)
        vals, idxs, hots = [], [], []
        cur = lt
        for _ in range(TOP_K):
            m = jnp.max(cur, axis=0, keepdims=True)
            idx = jnp.min(jnp.where(cur == m, e_iota, N_EXPERTS), axis=0, keepdims=True)
            hit = e_iota == idx
            vals.append(m)
            idxs.append(idx)
            hots.append(hit)
            cur = jnp.where(hit, -jnp.inf, cur)
        ex = [jnp.exp(v - vals[0]) for v in vals]
        inv = 1.0 / (ex[0] + ex[1] + ex[2] + ex[3])
        zi = jnp.zeros((8 - TOP_K, sub), jnp.int32)
        zf = jnp.zeros((8 - TOP_K, sub), F32)
        cs = slice(j * sub, (j + 1) * sub)
        ti_ref[:, cs] = jnp.concatenate(idxs + [zi], axis=0)
        tg_ref[:, cs] = jnp.concatenate([e * inv for e in ex] + [zf], axis=0)
        hot = jnp.concatenate([hh_.astype(F32) for hh_ in hots], axis=0)
        return jnp.sum(hot, axis=1, keepdims=True)

    hs = [project(j) for j in range(n_sub)]
    csum = route(hs[0], 0)
    for j in range(1, n_sub):
        csum = csum + route(hs[j], j)

    @pl.when(i == 0)
    def _():
        cnt_ref[...] = jnp.zeros_like(cnt_ref)

    cnt_ref[...] += jnp.broadcast_to(csum, cnt_ref.shape)


def _outproj(attn, sa, gp, x, w_out, g, rw, rb):
    n_tok, d_model = x.shape
    tm = min(2 * ROW_TILE, n_tok)
    row = lambda i: (i, 0)
    fixed = lambda i: (0, 0)
    colb = lambda i: (0, i)
    return pl.pallas_call(
        _outproj_body,
        grid=(n_tok // tm,),
        in_specs=[
            pl.BlockSpec((tm, d_model), row),
            pl.BlockSpec((tm, d_model), row),
            pl.BlockSpec((tm, d_model), row),
            pl.BlockSpec((tm, d_model), row),
            pl.BlockSpec((d_model, d_model), fixed),
            pl.BlockSpec((1, d_model), fixed),
            pl.BlockSpec((d_model, LANES), fixed),
            pl.BlockSpec((1, LANES), fixed),
        ],
        out_specs=[
            pl.BlockSpec((tm * ROW_SUB, LANES), row),
            pl.BlockSpec((8, tm), colb),
            pl.BlockSpec((8, tm), colb),
            pl.BlockSpec((TOP_K * N_EXPERTS, LANES), fixed),
        ],
        out_shape=[
            jax.ShapeDtypeStruct((n_tok * ROW_SUB, LANES), F32),
            jax.ShapeDtypeStruct((8, n_tok), jnp.int32),
            jax.ShapeDtypeStruct((8, n_tok), F32),
            jax.ShapeDtypeStruct((TOP_K * N_EXPERTS, LANES), F32),
        ],
        compiler_params=_cparams(("arbitrary",)),
        name="outproj",
    )(attn, sa, gp, x, w_out, g, rw, rb)


def _rank_body(ti_ref, base_ref, tri_ref, dest_ref, carry_ref):
    i = pl.program_id(0)

    @pl.when(i == 0)
    def _():
        carry_ref[...] = jnp.zeros_like(carry_ref)

    tt = ti_ref.shape[1]
    e_iota = lax.broadcasted_iota(jnp.int32, (N_EXPERTS, tt), 0)
    hot = jnp.concatenate([(ti_ref[k:k + 1, :] == e_iota).astype(F32) for k in range(TOP_K)], axis=0)
    incl = jnp.dot(hot.astype(BF16), tri_ref[...], preferred_element_type=F32)
    slot = base_ref[:, 0:1] + carry_ref[:, 0:1] + incl - 1.0
    picked = hot * slot
    rows = [jnp.sum(picked[k * N_EXPERTS:(k + 1) * N_EXPERTS, :], axis=0, keepdims=True)
            for k in range(TOP_K)]
    rows.append(jnp.zeros((8 - TOP_K, tt), F32))
    dest_ref[...] = jnp.concatenate(rows, axis=0).astype(jnp.int32)
    carry_ref[...] += jnp.broadcast_to(jnp.sum(hot, axis=1, keepdims=True), carry_ref.shape)


def _rank(topi, base, tri):
    n_tok = topi.shape[1]
    tt = tri.shape[0]
    return pl.pallas_call(
        _rank_body,
        grid=(n_tok // tt,),
        in_specs=[
            pl.BlockSpec((8, tt), lambda i: (0, i)),
            pl.BlockSpec(base.shape, lambda i: (0, 0)),
            pl.BlockSpec((tt, tt), lambda i: (0, 0)),
        ],
        out_specs=pl.BlockSpec((8, tt), lambda i: (0, i)),
        out_shape=jax.ShapeDtypeStruct((8, n_tok), jnp.int32),
        scratch_shapes=[pltpu.VMEM((TOP_K * N_EXPERTS, LANES), F32)],
        compiler_params=_cparams(("arbitrary",)),
        name="rank",
    )(topi, base, tri)


DISPATCH_BUFS = 3


def _slab_copy(src, src_row, dst, dst_row, sem):
    def first_sublane(row):
        return row * ROW_SUB if isinstance(row, int) else pl.multiple_of(row * ROW_SUB, ROW_SUB)

    return pltpu.make_async_copy(src.at[pl.ds(first_sublane(src_row), ROW_SUB)],
                                 dst.at[pl.ds(first_sublane(dst_row), ROW_SUB)], sem)


def _dispatch_body(zs_ref, zl_ref, tz_ref, dest_hbm, hp_hbm, hs_hbm, xs_hbm,
                   idx0, idx1, idx2, hb0, hb1, hb2, lsem, dsem, zsem,
                   *, tt, n_prompt_tiles, n_out_tiles):
    i = pl.program_id(0)
    n = pl.num_programs(0)
    idx = (idx0, idx1, idx2)
    hbuf = (hb0, hb1, hb2)
    nbuf = DISPATCH_BUFS
    tile_sub = tt * ROW_SUB

    def load_wait(s):
        pltpu.make_async_copy(dest_hbm.at[0], idx[s], lsem.at[s]).wait()
        pltpu.make_async_copy(hp_hbm.at[pl.ds(0, tile_sub)], hbuf[s], lsem.at[s]).wait()

    def load_start(tile, s):
        pltpu.make_async_copy(dest_hbm.at[tile], idx[s], lsem.at[s]).start()

        @pl.when(tile < n_prompt_tiles)
        def _():
            r0 = pl.multiple_of(tile * tile_sub, tile_sub)
            pltpu.make_async_copy(hp_hbm.at[pl.ds(r0, tile_sub)], hbuf[s], lsem.at[s]).start()

        @pl.when(tile >= n_prompt_tiles)
        def _():
            r0 = pl.multiple_of((tile - n_prompt_tiles) * tile_sub, tile_sub)
            pltpu.make_async_copy(hs_hbm.at[pl.ds(r0, tile_sub)], hbuf[s], lsem.at[s]).start()

    def scatter_start(s):
        for k in range(TOP_K):
            for r in range(tt):
                _slab_copy(hbuf[s], r, xs_hbm, idx[s][k * tt + r], dsem.at[s]).start(priority=r % 2)

    def scatter_wait(s):
        for _ in range(TOP_K):
            pltpu.make_async_copy(hbuf[s], xs_hbm.at[pl.ds(0, tile_sub)], dsem.at[s]).wait()

    @pl.when(i == 0)
    def _():
        zero = hbuf[nbuf - 1]
        zero[...] = jnp.zeros_like(zero)

        def pad_pieces(e, act):
            length = zl_ref[e]
            for shift in range(tt.bit_length() - 1, -1, -1):
                piece = 1 << shift
                first = zs_ref[e] + jnp.bitwise_and(length, ~(2 * piece - 1))

                @pl.when(jnp.bitwise_and(length, piece) != 0)
                def _(piece=piece, first=first):
                    dst0 = pl.multiple_of(first * ROW_SUB, ROW_SUB)
                    act(pltpu.make_async_copy(zero.at[pl.ds(0, piece * ROW_SUB)],
                                              xs_hbm.at[pl.ds(dst0, piece * ROW_SUB)], zsem))

        def per_expert(e, c):
            pad_pieces(e, lambda cp: cp.start())
            pad_pieces(e, lambda cp: cp.wait())
            return c
        lax.fori_loop(0, N_EXPERTS, per_expert, 0)

        def tail_copy(j):
            r0 = pl.multiple_of(j * tile_sub, tile_sub)
            return pltpu.make_async_copy(zero, xs_hbm.at[pl.ds(r0, tile_sub)], zsem)

        def tail_one(j, c):
            tail_copy(j).start()
            tail_copy(j).wait()
            return c
        lax.fori_loop(tz_ref[0], n_out_tiles, tail_one, 0)

        load_start(0, 0)

        @pl.when(n > 1)
        def _():
            load_start(1, 1)

    for s in range(nbuf):
        @pl.when(i % nbuf == s)
        def _(s=s):
            prev = (s + nbuf - 1) % nbuf
            load_wait(s)
            scatter_start(s)

            @pl.when(i > 0)
            def _():
                scatter_wait(prev)

            @pl.when(i + 2 < n)
            def _():
                load_start(i + 2, prev)

            @pl.when(i == n - 1)
            def _():
                scatter_wait(s)


def _dispatch(zero_start, zero_len, tail_tile, dest_tiles, hp, hs, n_rows):
    n_tiles, width = dest_tiles.shape
    tt = width // TOP_K
    n_prompt_tiles = hp.shape[0] // (tt * ROW_SUB)
    any_spec = pl.BlockSpec(memory_space=pl.ANY)
    grid_spec = pltpu.PrefetchScalarGridSpec(
        num_scalar_prefetch=3,
        grid=(n_tiles,),
        in_specs=[any_spec, any_spec, any_spec],
        out_specs=any_spec,
        scratch_shapes=[pltpu.SMEM((width,), jnp.int32)] * DISPATCH_BUFS
        + [pltpu.VMEM((tt * ROW_SUB, LANES), F32)] * DISPATCH_BUFS
        + [
            pltpu.SemaphoreType.DMA((DISPATCH_BUFS,)),
            pltpu.SemaphoreType.DMA((DISPATCH_BUFS,)),
            pltpu.SemaphoreType.DMA,
        ],
    )
    return pl.pallas_call(
        functools.partial(_dispatch_body, tt=tt, n_prompt_tiles=n_prompt_tiles, n_out_tiles=n_rows // tt),
        grid_spec=grid_spec,
        out_shape=jax.ShapeDtypeStruct((n_rows * ROW_SUB, LANES), F32),
        compiler_params=_cparams(("arbitrary",)),
        name="dispatch",
    )(zero_start, zero_len, tail_tile, dest_tiles, hp, hs)


def _moe_body(te_ref, nu_ref, nx_ref, x_ref, g_ref, wg_hbm, bg_ref, wu_hbm, bu_ref, wd_hbm, bd_ref,
              y_ref, wgs, wus, wds, wgb, wub, wdb, wsem, *, tm):
    i = pl.program_id(0)
    n_used = nu_ref[0]

    def weight_copies(e):
        return (pltpu.make_async_copy(wg_hbm.at[e], wgs, wsem),
                pltpu.make_async_copy(wu_hbm.at[e], wus, wsem),
                pltpu.make_async_copy(wd_hbm.at[e], wds, wsem))

    @pl.when(i >= n_used)
    def _():
        y_ref[...] = jnp.zeros_like(y_ref)

    @pl.when(i < n_used)
    def _():
        @pl.when(i == 0)
        def _():
            for cp in weight_copies(te_ref[0]):
                cp.start()

        prev = te_ref[jnp.maximum(i - 1, 0)]
        changed = jnp.logical_or(i == 0, te_ref[i] != prev)

        @pl.when(changed)
        def _():
            for cp in weight_copies(te_ref[i]):
                cp.wait()
            wgb[...] = wgs[...].astype(BF16)
            wub[...] = wus[...].astype(BF16)
            wdb[...] = wds[...].astype(BF16)

            @pl.when(nx_ref[i] >= 0)
            def _():
                for cp in weight_copies(nx_ref[i]):
                    cp.start()

        half = tm // 2
        d_model = ROW_SUB * LANES

        def normed_half(r0):
            cols = [x_ref[pl.ds(r0 * ROW_SUB + c, half, stride=ROW_SUB), :] for c in range(ROW_SUB)]
            ssq = cols[0] * cols[0]
            for xc in cols[1:]:
                ssq = ssq + xc * xc
            r = lax.rsqrt(jnp.sum(ssq, axis=-1, keepdims=True) * (1.0 / d_model) + RMS_EPS)
            return jnp.concatenate(
                [(xc * r * g_ref[:, c * LANES:(c + 1) * LANES]).astype(BF16) for c, xc in enumerate(cols)], axis=1)

        def gate_up(xb):
            gt = jnp.dot(xb, wgb[...], preferred_element_type=F32) + bg_ref[0]
            up = jnp.dot(xb, wub[...], preferred_element_type=F32) + bu_ref[0]
            return gt, up

        def down(gt, up, r0):
            gt = jnp.minimum(gt, SWIGLU_LIMIT)
            up = jnp.clip(up, -SWIGLU_LIMIT, SWIGLU_LIMIT)
            hdn = (up + 1.0) * (gt * _sigmoid(SWIGLU_ALPHA * gt))
            y = jnp.dot(hdn.astype(BF16), wdb[...], preferred_element_type=F32) + bd_ref[0]
            for c in range(ROW_SUB):
                y_ref[pl.ds(r0 * ROW_SUB + c, half, stride=ROW_SUB), :] = y[:, c * LANES:(c + 1) * LANES]

        ga, ua = gate_up(normed_half(0))
        gb, ub = gate_up(normed_half(half))
        down(ga, ua, 0)
        down(gb, ub, half)


def _moe(tile_expert, n_used, next_expert, x_rows, g, w_gate, b_gate, w_up, b_up, w_down, b_down, *, tm):
    n_tiles = x_rows.shape[0] // (tm * ROW_SUB)
    d_model = ROW_SUB * LANES
    d_ff = w_gate.shape[2]
    assert w_gate.shape[1] == d_model
    bspec = lambda width: pl.BlockSpec((1, 1, width), lambda i, te, nu, nx: (te[i], 0, 0))
    any_spec = pl.BlockSpec(memory_space=pl.ANY)
    grid_spec = pltpu.PrefetchScalarGridSpec(
        num_scalar_prefetch=3,
        grid=(n_tiles,),
        in_specs=[
            pl.BlockSpec((tm * ROW_SUB, LANES), lambda i, te, nu, nx: (jnp.minimum(i, nu[0] - 1), 0)),
            pl.BlockSpec((1, d_model), lambda i, te, nu, nx: (0, 0)),
            any_spec, bspec(d_ff),
            any_spec, bspec(d_ff),
            any_spec, bspec(d_model),
        ],
        out_specs=pl.BlockSpec((tm * ROW_SUB, LANES), lambda i, te, nu, nx: (i, 0)),
        scratch_shapes=[
            pltpu.VMEM((d_model, d_ff), F32),
            pltpu.VMEM((d_model, d_ff), F32),
            pltpu.VMEM((d_ff, d_model), F32),
            pltpu.VMEM((d_model, d_ff), BF16),
            pltpu.VMEM((d_model, d_ff), BF16),
            pltpu.VMEM((d_ff, d_model), BF16),
            pltpu.SemaphoreType.DMA,
        ],
    )
    return pl.pallas_call(
        functools.partial(_moe_body, tm=tm),
        grid_spec=grid_spec,
        out_shape=jax.ShapeDtypeStruct((n_tiles * tm * ROW_SUB, LANES), F32),
        compiler_params=_cparams(("arbitrary",)),
        name="moe",
    )(tile_expert, n_used, next_expert, x_rows, g,
      w_gate, b_gate.reshape(N_EXPERTS, 1, d_ff), w_up, b_up.reshape(N_EXPERTS, 1, d_ff),
      w_down, b_down.reshape(N_EXPERTS, 1, d_model))


def _combine_body(dest_hbm, y_hbm, h_ref, gt_ref, o_ref, idx0, idx1, yb0, yb1, isem, gsem, *, tt):
    i = pl.program_id(0)
    n = pl.num_programs(0)
    idx = (idx0, idx1)
    ybuf = (yb0, yb1)

    def idx_copy(tile, s):
        return pltpu.make_async_copy(dest_hbm.at[tile], idx[s], isem.at[s])

    def issue_gather(s):
        for k in range(TOP_K):
            for r in range(tt):
                src = pl.multiple_of(idx[s][k * tt + r] * ROW_SUB, ROW_SUB)
                pltpu.make_async_copy(y_hbm.at[pl.ds(src, ROW_SUB)],
                                      ybuf[s].at[k, pl.ds(r * ROW_SUB, ROW_SUB)], gsem.at[s]).start(priority=r % 2)

    def wait_gather(s):
        for k in range(TOP_K):
            pltpu.make_async_copy(y_hbm.at[pl.ds(0, tt * ROW_SUB)], ybuf[s].at[k], gsem.at[s]).wait()

    @pl.when(i == 0)
    def _():
        idx_copy(0, 0).start()
        idx_copy(0, 0).wait()
        issue_gather(0)

        @pl.when(n > 1)
        def _():
            idx_copy(1, 1).start()

    for s in range(2):
        @pl.when(i % 2 == s)
        def _(s=s):
            o = 1 - s

            @pl.when(i + 1 < n)
            def _():
                idx_copy(i + 1, o).wait()
                issue_gather(o)

            @pl.when(i + 2 < n)
            def _():
                idx_copy(i + 2, s).start()

            wait_gather(s)
            gts = gt_ref[...]
            for c in range(ROW_SUB):
                cs = slice(c * LANES, (c + 1) * LANES)
                acc = _load_slab_cols(h_ref, tt, c)
                for k in range(TOP_K):
                    acc = acc + gts[:, k:k + 1] * ybuf[s][k, pl.ds(c, tt, stride=ROW_SUB), :]
                o_ref[:, cs] = acc


def _combine(dest_tiles, y_rows, h, gates_tok):
    n_tiles, width = dest_tiles.shape
    tt = width // TOP_K
    n_tok, d_model = h.shape[0] // ROW_SUB, ROW_SUB * LANES
    any_spec = pl.BlockSpec(memory_space=pl.ANY)
    return pl.pallas_call(
        functools.partial(_combine_body, tt=tt),
        grid=(n_tiles,),
        in_specs=[any_spec, any_spec,
                  pl.BlockSpec((tt * ROW_SUB, LANES), lambda i: (i, 0)),
                  pl.BlockSpec((tt, 8), lambda i: (i, 0))],
        out_specs=pl.BlockSpec((tt, d_model), lambda i: (i, 0)),
        out_shape=jax.ShapeDtypeStruct((n_tok, d_model), F32),
        scratch_shapes=[
            pltpu.SMEM((TOP_K * tt,), jnp.int32),
            pltpu.SMEM((TOP_K * tt,), jnp.int32),
            pltpu.VMEM((TOP_K, tt * ROW_SUB, LANES), F32),
            pltpu.VMEM((TOP_K, tt * ROW_SUB, LANES), F32),
            pltpu.SemaphoreType.DMA((2,)),
            pltpu.SemaphoreType.DMA((2,)),
        ],
        compiler_params=_cparams(("arbitrary",)),
        name="combine",
    )(dest_tiles, y_rows, h, gates_tok)


def _block_diag_ones():
    r = lax.broadcasted_iota(jnp.int32, (NORM_BLOCK, NORM_BLOCK), 0) // HEAD_DIM
    c = lax.broadcasted_iota(jnp.int32, (NORM_BLOCK, NORM_BLOCK), 1) // HEAD_DIM
    return (r == c).astype(BF16)


def _dest_tiles(dest, tt):
    n_tok = dest.shape[1]
    return dest[:TOP_K].reshape(TOP_K, n_tok // tt, tt).transpose(1, 0, 2).reshape(n_tok // tt, TOP_K * tt)


def _layer(xp, xs, ck, cv, sp, lam, lam_init, ng, w_in, qng, kng, slg, pw, ps, w_out, fg,
           rw, rb, w_gate, b_gate, w_up, b_up, w_down, b_down):
    bp, t_len, d_model = xp.shape
    bs, ts, _ = xs.shape
    past = ck.shape[1]
    pool_w = sp.shape[-1]
    n_p, n_s = bp * t_len, bs * ts

    w_in_b = w_in.astype(BF16)
    w_out_b = w_out.astype(BF16)
    pw_b = pw.astype(BF16)
    reps = NORM_BLOCK // HEAD_DIM
    qg = (jnp.tile(qng.astype(F32), reps) * (HEAD_DIM ** -0.5)).reshape(1, NORM_BLOCK)
    kg = jnp.tile(kng.astype(F32), reps).reshape(1, NORM_BLOCK)
    bd = _block_diag_ones()
    ng2 = ng.astype(F32).reshape(1, d_model)
    ps2 = ps.astype(F32).reshape(1, d_model)
    sg = (slg.astype(F32) * (1.0 - lam_init)).reshape(1, V_DIM)
    slopes = jnp.exp2(-(8.0 / N_HEADS) * jnp.arange(1, N_HEADS + 1, dtype=F32))
    lam1 = lam.reshape(1).astype(F32)
    fg2 = fg.astype(F32).reshape(1, d_model)
    rw_b = jnp.zeros((d_model, LANES), BF16).at[:, :N_EXPERTS].set(rw.astype(BF16))
    rb2 = jnp.zeros((1, LANES), F32).at[0, :N_EXPERTS].set(rb.astype(F32))

    tm = min(ROW_TILE, t_len)
    zero_pre = jnp.zeros((bp, POOL_HALO, pool_w), F32)
    qp, kp, kpb, vp, vpb, sap, gpp, utp = _inproj(
        xp, zero_pre, ng2, w_in_b, qg, kg, bd, pw_b, ps2, nseg=1, seg_len=tm, start_pos=0, carry=True)
    atp = _attn_prompt(qp, kpb, vpb, slopes, lam1, sg)
    hp, tip, tgp, cntp = _outproj(atp.reshape(n_p, d_model), sap.reshape(n_p, d_model),
                                  gpp.reshape(n_p, d_model), xp.reshape(n_p, d_model),
                                  w_out_b, fg2, rw_b, rb2)

    pre_s = jnp.concatenate([jnp.zeros((bs, POOL_HALO - sp.shape[1], pool_w), F32), sp.astype(F32)], axis=1)
    qs, ks, ksb, vs, vsb, sas, gps, uts = _inproj(
        xs.reshape(1, n_s, d_model), pre_s, ng2, w_in_b, qg, kg, bd, pw_b, ps2,
        nseg=bs, seg_len=ts, start_pos=past, carry=False)
    ats = _attn_sample(qs.reshape(bs, ts, d_model), ksb.reshape(bs, ts, d_model), vsb.reshape(bs, ts, d_model),
                       ck.reshape(bs, past * N_HEADS, V_DIM), cv.reshape(bs, past * N_HEADS, V_DIM),
                       slopes, lam1, sg)
    hs, tis, tgs, cnts = _outproj(ats.reshape(n_s, d_model), sas.reshape(n_s, d_model),
                                  gps.reshape(n_s, d_model), xs.reshape(n_s, d_model),
                                  w_out_b, fg2, rw_b, rb2)

    n_tok = n_p + n_s
    topi = jnp.concatenate([tip, tis], axis=1)
    gates = jnp.concatenate([tgp, tgs], axis=1)
    cnt = (cntp[:, 0] + cnts[:, 0]).astype(jnp.int32).reshape(TOP_K, N_EXPERTS)
    per_expert = jnp.sum(cnt, axis=0)
    padded = (per_expert + MOE_TILE - 1) // MOE_TILE * MOE_TILE
    pad_end = jnp.cumsum(padded)
    pad_start = pad_end - padded
    base = pad_start[None, :] + jnp.cumsum(cnt, axis=0) - cnt
    base_f = jnp.broadcast_to(base.reshape(-1, 1).astype(F32), (TOP_K * N_EXPERTS, LANES))
    tt = TOK_TILE
    rt = max(m * tt for m in range(1, RANK_TILE_MAX // tt + 1) if n_tok % (m * tt) == 0)
    tri = (lax.broadcasted_iota(jnp.int32, (rt, rt), 0) <= lax.broadcasted_iota(jnp.int32, (rt, rt), 1)).astype(BF16)
    dest = _rank(topi, base_f, tri)

    n_tiles = -(-(n_tok * TOP_K) // MOE_TILE) + N_EXPERTS
    n_used = (pad_end[-1] // MOE_TILE).astype(jnp.int32)
    tile_start = jnp.arange(n_tiles, dtype=jnp.int32) * MOE_TILE
    last_start = jnp.maximum(pad_end[-1] - MOE_TILE, 0)
    tile_expert = jnp.minimum(
        jnp.sum((jnp.minimum(tile_start, last_start)[:, None] >= pad_end[None, :]).astype(jnp.int32), axis=1),
        N_EXPERTS - 1)
    dest_p = _dest_tiles(dest[:, :n_p], tt)
    dest_s = _dest_tiles(dest[:, n_p:], tt)
    dest_all = jnp.concatenate([dest_p, dest_s], axis=0)
    assert MOE_TILE <= 2 * tt
    x_rows = _dispatch(pad_start + per_expert, padded - per_expert, (pad_end[-1:] // tt).astype(jnp.int32),
                       dest_all, hp, hs, n_tiles * MOE_TILE)
    e_ids = jnp.arange(N_EXPERTS, dtype=jnp.int32)
    later_present = jnp.logical_and(e_ids[None, :] > e_ids[:, None], (padded > 0)[None, :])
    next_of = jnp.min(jnp.where(later_present, e_ids[None, :], N_EXPERTS), axis=1)
    next_of = jnp.where(next_of == N_EXPERTS, -1, next_of).astype(jnp.int32)
    next_expert = jnp.sum(jnp.where(tile_expert[:, None] == e_ids[None, :], next_of[None, :], 0), axis=1)
    y_rows = _moe(tile_expert, n_used.reshape(1), next_expert.astype(jnp.int32), x_rows, fg2,
                  w_gate, b_gate, w_up, b_up, w_down, b_down, tm=MOE_TILE)

    gates_tok = jnp.transpose(gates)
    yp = _combine(dest_p, y_rows, hp, gates_tok[:n_p])
    ys = _combine(dest_s, y_rows, hs, gates_tok[n_p:])

    heads = (N_HEADS, V_DIM)
    return (yp.reshape(bp, t_len, d_model), ys.reshape(bs, ts, d_model),
            kp.reshape(bp, t_len, *heads), vp.reshape(bp, t_len, *heads), utp[:, 1:],
            ks.reshape(bs, ts, *heads), vs.reshape(bs, ts, *heads), uts[:, 1:])


def kernel(x_prompt, x_sample, cache_k, cache_v, state_pool, norm_mix_g, w_in, q_norm_g, k_norm_g,
           lambda_q1, lambda_k1, lambda_q2, lambda_k2, subln_g, pool_w, pool_scale, w_out, norm_ffn_g,
           router_w, router_b, w_gate, b_gate, w_up, b_up, w_down, b_down):
    depth = w_in.shape[0]
    hp, hs = x_prompt, x_sample
    outs = [[] for _ in range(6)]
    for layer in range(depth):
        lam_init = 0.8 - 0.6 * math.exp(-0.3 * layer)
        lam = (jnp.exp(jnp.sum(lambda_q1[layer].astype(F32) * lambda_k1[layer].astype(F32)))
               - jnp.exp(jnp.sum(lambda_q2[layer].astype(F32) * lambda_k2[layer].astype(F32)))
               + lam_init)
        hp, hs, kp, vp, up, ks, vs, us = _layer(
            hp, hs, cache_k[layer], cache_v[layer], state_pool[layer], lam, lam_init,
            norm_mix_g[layer], w_in[layer], q_norm_g[layer], k_norm_g[layer], subln_g[layer],
            pool_w[layer], pool_scale[layer], w_out[layer], norm_ffn_g[layer],
            router_w[layer], router_b[layer], w_gate[layer], b_gate[layer], w_up[layer], b_up[layer],
            w_down[layer], b_down[layer])
        for lst, val in zip(outs, (kp, vp, up, ks, vs, us)):
            lst.append(val)
    return (hp, hs) + tuple(jnp.stack(o) for o in outs)
```

```python
import functools
import math

import jax
import jax.numpy as jnp
from jax import lax
from jax.experimental import pallas as pl
from jax.experimental.pallas import tpu as pltpu

F32 = jnp.float32
BF16 = jnp.bfloat16

CHUNK = 64
N_HEADS = 8
HEAD_DIM = 64
V_DIM = 2 * HEAD_DIM
POOL_WINDOWS = (2, 4, 8, 16)
POOL_GROUP_DIM = 128
POOL_OUT_DIM = 256
POOL_HALO = 16
N_EXPERTS = 32
TOP_K = 4
SWIGLU_LIMIT = 7.0
SWIGLU_ALPHA = 1.702
RMS_EPS = 1e-6
NEG_INF = -1e30

LANES = 128
ROW_SUB = 8
NORM_BLOCK = 256
VMEM_LIMIT = 56 * 1024 * 1024

ROW_TILE = 512
ATTN_TILE = 256
KEY_CHUNK = 512
MOE_TILE = 512
MOE_SUBTILES = 2
GATE_LANES = LANES // TOP_K
TOK_TILE = 256
RANK_TILE_MAX = 1024


def _sigmoid(x):
    return 1.0 / (1.0 + jnp.exp(-x))


def _cparams(sem):
    return pltpu.CompilerParams(dimension_semantics=sem, vmem_limit_bytes=VMEM_LIMIT)


def _store_slabs(ref, val):
    rows = val.shape[0]
    for c in range(ROW_SUB):
        ref[pl.ds(c, rows, stride=ROW_SUB), :] = val[:, c * LANES:(c + 1) * LANES]


def _load_slab_cols(ref, rows, c):
    return ref[pl.ds(c, rows, stride=ROW_SUB), :]


def _inproj_body(x_ref, pre_ref, ng_ref, w_ref, qg_ref, kg_ref, bd_ref, pw_ref, ps_ref,
                 q_ref, k_ref, kb_ref, v_ref, vb_ref, sa_ref, gp_ref, ut_ref, ext_ref,
                 *, nseg, seg_len, start_pos, carry, attn_w, pool_w):
    rows = nseg * seg_len
    t = pl.program_id(1)
    x = x_ref[0]
    ms = jnp.mean(x * x, axis=-1, keepdims=True)
    xn = (x * lax.rsqrt(ms + RMS_EPS) * ng_ref[...]).astype(BF16)

    def proj(c0, width):
        return jnp.dot(xn, w_ref[:, c0:c0 + width], preferred_element_type=F32)

    bd = bd_ref[...]

    def group_norm(p, g_ref):
        ss = jnp.dot((p * p).astype(BF16), bd, preferred_element_type=F32)
        return p * lax.rsqrt(ss * (1.0 / HEAD_DIM) + RMS_EPS) * g_ref[...]

    nb = NORM_BLOCK
    heads_per_block = nb // V_DIM

    def store_heads(ref, val, c):
        for j in range(heads_per_block):
            head = c * heads_per_block + j
            ref[0, pl.ds(head, rows, stride=N_HEADS), :] = val[:, j * V_DIM:(j + 1) * V_DIM]

    def sink_q(val, c):
        q_ref[0, :, c * nb:(c + 1) * nb] = group_norm(val, qg_ref).astype(BF16)

    def sink_k(val, c):
        kn = group_norm(val, kg_ref)
        store_heads(k_ref, kn, c)
        kb_ref[0, :, c * nb:(c + 1) * nb] = kn.astype(BF16)

    def sink_v(val, c):
        store_heads(v_ref, val, c)
        vb_ref[0, :, c * nb:(c + 1) * nb] = val.astype(BF16)

    n_chunks = attn_w // nb
    work = [(part * attn_w + c * nb, sink, c)
            for part, sink in enumerate((sink_q, sink_k, sink_v)) for c in range(n_chunks)]
    pending = proj(work[0][0], nb)
    for j, (_, sink, c) in enumerate(work):
        cur = pending
        if j + 1 < len(work):
            pending = proj(work[j + 1][0], nb)
        sink(cur, c)

    if carry:
        @pl.when(t == 0)
        def _():
            ext_ref[:, 0:POOL_HALO, :] = pre_ref[...]
    else:
        ext_ref[:, 0:POOL_HALO, :] = pre_ref[...]
    for c in range(pool_w // nb):
        cs = slice(c * nb, (c + 1) * nb)
        u = proj(3 * attn_w + c * nb, nb)
        ext_ref[:, POOL_HALO:POOL_HALO + seg_len, cs] = u.reshape(nseg, seg_len, nb)

    ga0 = 3 * attn_w + pool_w
    d_model = attn_w
    for c in range(d_model // nb):
        cs = slice(c * nb, (c + 1) * nb)
        sa_ref[0, :, cs] = _sigmoid(proj(ga0 + c * nb, nb)).astype(BF16)

    gb0 = ga0 + d_model
    row = lax.broadcasted_iota(jnp.int32, (1, seg_len, 1), 1)
    pos = row + start_pos
    if carry:
        pos = pos + t * seg_len
    for g, w in enumerate(POOL_WINDOWS):
        cs = slice(g * POOL_GROUP_DIM, (g + 1) * POOL_GROUP_DIM)
        own = ext_ref[:, POOL_HALO:POOL_HALO + seg_len, cs]
        acc = own
        for i in range(1, w):
            acc = acc + ext_ref[:, POOL_HALO - i:POOL_HALO - i + seg_len, cs]
        inv = 1.0 / jnp.minimum(w, pos + 1).astype(F32)
        z = (acc * inv - own).reshape(rows, POOL_GROUP_DIM)
        os_ = slice(g * POOL_OUT_DIM, (g + 1) * POOL_OUT_DIM)
        yp = jnp.dot(z.astype(BF16), pw_ref[g], preferred_element_type=F32) * ps_ref[:, os_]
        gb = proj(gb0 + g * POOL_OUT_DIM, POOL_OUT_DIM)
        gp_ref[0, :, os_] = (_sigmoid(gb) * yp).astype(BF16)

    tail = ext_ref[:, seg_len:seg_len + POOL_HALO, :]
    ut_ref[...] = tail
    if carry:
        ext_ref[:, 0:POOL_HALO, :] = tail


def _inproj(x3, prefix, ng, w_in, qg, kg, bd, pw, ps, *, nseg, seg_len, start_pos, carry):
    groups, t_len, d_model = x3.shape
    rows = nseg * seg_len
    steps = t_len // rows
    in_cols = w_in.shape[1]
    pool_w = prefix.shape[-1]
    attn_w = d_model
    assert in_cols == 3 * attn_w + pool_w + 2 * d_model
    tok = lambda b, t: (b, t, 0)
    fixed2 = lambda b, t: (0, 0)
    act = lambda dt: jax.ShapeDtypeStruct((groups, t_len, d_model), dt)
    by_head = jax.ShapeDtypeStruct((groups, t_len * N_HEADS, V_DIM), F32)
    tok_spec = pl.BlockSpec((1, rows, d_model), tok)
    head_spec = pl.BlockSpec((1, rows * N_HEADS, V_DIM), tok)
    body = functools.partial(_inproj_body, nseg=nseg, seg_len=seg_len, start_pos=start_pos,
                             carry=carry, attn_w=attn_w, pool_w=pool_w)
    return pl.pallas_call(
        body,
        grid=(groups, steps),
        in_specs=[
            pl.BlockSpec((1, rows, d_model), tok),
            pl.BlockSpec((nseg, POOL_HALO, pool_w), lambda b, t: (b, 0, 0)),
            pl.BlockSpec((1, d_model), fixed2),
            pl.BlockSpec((d_model, in_cols), fixed2, pipeline_mode=pl.Buffered(1)),
            pl.BlockSpec((1, NORM_BLOCK), fixed2),
            pl.BlockSpec((1, NORM_BLOCK), fixed2),
            pl.BlockSpec((NORM_BLOCK, NORM_BLOCK), fixed2),
            pl.BlockSpec(pw.shape, lambda b, t: (0, 0, 0)),
            pl.BlockSpec((1, d_model), fixed2),
        ],
        out_specs=[tok_spec, head_spec, tok_spec, head_spec, tok_spec, tok_spec, tok_spec,
                   pl.BlockSpec((nseg, POOL_HALO, pool_w), lambda b, t: (b, 0, 0))],
        out_shape=[act(BF16), by_head, act(BF16), by_head, act(BF16), act(BF16), act(BF16),
                   jax.ShapeDtypeStruct(prefix.shape, F32)],
        scratch_shapes=[pltpu.VMEM((nseg, POOL_HALO + seg_len, pool_w), F32)],
        compiler_params=_cparams(("arbitrary", "arbitrary")),
        name="inproj",
    )(x3, prefix, ng, w_in, qg, kg, bd, pw, ps)


def _half_masks(q):
    lane = lax.broadcasted_iota(jnp.int32, q.shape, 1)
    zero = jnp.zeros_like(q)
    return jnp.where(lane < HEAD_DIM, q, zero), jnp.where(lane >= HEAD_DIM, q, zero)


def _qk(qz, kblk):
    return lax.dot_general(qz, kblk, (((1,), (1,)), ((), ())), preferred_element_type=F32)


def _subln(o, sg_ref):
    ms = jnp.mean(o * o, axis=-1, keepdims=True)
    return o * lax.rsqrt(ms + RMS_EPS) * sg_ref[...]


def _attn_body(slope_ref, lam_ref, q_ref, k_ref, v_ref, ka_ref, sg_ref, o_ref,
               kf_ref, vf_ref, s_ref, p_ref, *, tq, nq):
    h = pl.program_id(1)
    slope = slope_ref[h]
    lam = lam_ref[0]
    lane = lax.broadcasted_iota(jnp.int32, (tq, LANES), 1)
    kf_ref[:, 0:V_DIM] = k_ref[0]
    kf_ref[:, V_DIM:] = ka_ref[...]
    vf_ref[:, 0:V_DIM] = v_ref[0]
    t_len = vf_ref.shape[0]
    vf_ref[:, V_DIM:] = (lax.broadcasted_iota(jnp.int32, (t_len, LANES), 1) == 0).astype(BF16)

    row = lax.broadcasted_iota(jnp.int32, (tq, tq), 0)
    col = lax.broadcasted_iota(jnp.int32, (tq, tq), 1)
    rc = (row - col).astype(F32)
    vis = lax.shift_right_logical(col, 6) <= lax.shift_right_logical(row, 6)
    corr = jnp.where(vis, jnp.minimum(rc, 0.0) * (2.0 * slope), NEG_INF)

    def scores(qi, m):
        nk = (qi + 1) * tq
        qz = _half_masks(q_ref[0, qi * tq:(qi + 1) * tq, :])[m]
        t = lax.broadcasted_iota(jnp.int32, (tq, LANES), 0) + qi * tq
        t_hi = lax.shift_left(lax.shift_right_logical(t, 8), 8).astype(F32)
        t_lo = jnp.bitwise_and(t, 255).astype(F32)
        qaug = jnp.where(lane == 0, -slope * t_hi,
                         jnp.where(lane == 1, -slope * t_lo,
                                   jnp.where(lane < 4, slope, 0.0))).astype(BF16)
        qa = jnp.concatenate([qz, qaug], axis=1)
        for c0 in range(0, nk, KEY_CHUNK):
            c1 = min(c0 + KEY_CHUNK, nk)
            s_ref[qi % 2, m, :, c0:c1] = _qk(qa, kf_ref[c0:c1, :])

    def softmax_pv(qi, m):
        nk = (qi + 1) * tq
        b = qi % 2
        s_ref[b, m, :, nk - tq:nk] += corr
        chunks = [(c0, min(c0 + KEY_CHUNK, nk)) for c0 in range(0, nk, KEY_CHUNK)]
        mx = None
        for c0, c1 in chunks:
            part = jnp.max(s_ref[b, m, :, c0:c1], axis=-1, keepdims=True)
            mx = part if mx is None else jnp.maximum(mx, part)
        for c0, c1 in chunks:
            p_ref[m, :, c0:c1] = jnp.exp(s_ref[b, m, :, c0:c1] - mx).astype(BF16)
        return jnp.dot(p_ref[m, :, 0:nk], vf_ref[0:nk, :], preferred_element_type=F32)

    scores(0, 0)
    scores(0, 1)
    for qi in range(nq):
        if qi + 1 < nq:
            scores(qi + 1, 0)
            scores(qi + 1, 1)
        o1, o2 = softmax_pv(qi, 0), softmax_pv(qi, 1)
        c1 = 1.0 / o1[:, V_DIM:V_DIM + 1]
        c2 = lam / o2[:, V_DIM:V_DIM + 1]
        o = o1[:, 0:V_DIM] * c1 - o2[:, 0:V_DIM] * c2
        o_ref[0, qi * tq:(qi + 1) * tq, :] = _subln(o, sg_ref).astype(BF16)


def _attn_prompt(q, kb, vb, slopes, lam, sg):
    batch, t_len, width = q.shape
    tq = min(ATTN_TILE, t_len)
    nq = t_len // tq
    pos = jnp.arange(t_len, dtype=jnp.int32)
    ka = jnp.zeros((t_len, LANES), F32)
    ka = ka.at[:, 0:2].set(1.0).at[:, 2].set(((pos >> 8) << 8).astype(F32)).at[:, 3].set((pos & 255).astype(F32))
    smem = pl.BlockSpec(memory_space=pltpu.SMEM)
    seq = pl.BlockSpec((1, t_len, V_DIM), lambda b, h: (b, 0, h))
    return pl.pallas_call(
        functools.partial(_attn_body, tq=tq, nq=nq),
        grid=(batch, N_HEADS),
        in_specs=[
            smem, smem, seq, seq, seq,
            pl.BlockSpec((t_len, LANES), lambda b, h: (0, 0)),
            pl.BlockSpec((1, V_DIM), lambda b, h: (0, 0)),
        ],
        out_specs=seq,
        out_shape=jax.ShapeDtypeStruct((batch, t_len, width), BF16),
        scratch_shapes=[
            pltpu.VMEM((t_len, V_DIM + LANES), BF16),
            pltpu.VMEM((t_len, V_DIM + LANES), BF16),
            pltpu.VMEM((2, 2, tq, t_len), F32),
            pltpu.VMEM((2, tq, t_len), BF16),
        ],
        compiler_params=_cparams(("arbitrary", "arbitrary")),
        name="attn_prompt",
    )(slopes, lam, q, kb, vb, ka.astype(BF16), sg)


def _attn_dec_body(slope_ref, lam_ref, q_ref, kc_ref, vc_ref, kn_ref, vn_ref, sg_ref, o_ref, *, past):
    lam = lam_ref[0]
    tq = q_ref.shape[1]
    qpos_a = lax.broadcasted_iota(jnp.int32, (tq, past), 0) + past
    kpos_a = lax.broadcasted_iota(jnp.int32, (tq, past), 1)
    qpos_b = lax.broadcasted_iota(jnp.int32, (tq, tq), 0) + past
    kpos_b = lax.broadcasted_iota(jnp.int32, (tq, tq), 1) + past

    def dist_and_vis(qpos, kpos):
        vis = lax.shift_right_logical(kpos, 6) <= lax.shift_right_logical(qpos, 6)
        return jnp.abs(qpos - kpos).astype(F32), vis

    dist_a, vis_a = dist_and_vis(qpos_a, kpos_a)
    dist_b, vis_b = dist_and_vis(qpos_b, kpos_b)

    for h in range(N_HEADS):
        cs = slice(h * V_DIM, (h + 1) * V_DIM)
        slope = slope_ref[h]
        bias_a = jnp.where(vis_a, -slope * dist_a, NEG_INF)
        bias_b = jnp.where(vis_b, -slope * dist_b, NEG_INF)
        q1z, q2z = _half_masks(q_ref[0, :, cs])
        kc = kc_ref[0, pl.ds(h, past, stride=N_HEADS), :].astype(BF16)
        vc = vc_ref[0, pl.ds(h, past, stride=N_HEADS), :].astype(BF16)
        kn = kn_ref[0, :, cs]

        qcat = jnp.concatenate([q1z, q2z], axis=0)
        sa_both = _qk(qcat, kc)
        sb_both = _qk(qcat, kn)

        def softmax_parts(m_idx):
            rs = slice(m_idx * tq, (m_idx + 1) * tq)
            sa = sa_both[rs, :] + bias_a
            sb = sb_both[rs, :] + bias_b
            m = jnp.maximum(jnp.max(sa, axis=-1, keepdims=True), jnp.max(sb, axis=-1, keepdims=True))
            pa = jnp.exp(sa - m)
            pb = jnp.exp(sb - m)
            l = jnp.sum(pa, axis=-1, keepdims=True) + jnp.sum(pb, axis=-1, keepdims=True)
            return pa, pb, l

        pa1, pb1, l1 = softmax_parts(0)
        pa2, pb2, l2 = softmax_parts(1)
        c1 = 1.0 / l1
        c2 = lam / l2
        wa = (pa1 * c1 - pa2 * c2).astype(BF16)
        wb = (pb1 * c1 - pb2 * c2).astype(BF16)
        o = (jnp.dot(wa, vc, preferred_element_type=F32)
             + jnp.dot(wb, vn_ref[0, :, cs], preferred_element_type=F32))
        o_ref[0, :, cs] = _subln(o, sg_ref).astype(BF16)


def _attn_sample(q, kn, vn, cache_k, cache_v, slopes, lam, sg):
    batch, tq, width = q.shape
    past = cache_k.shape[1] // N_HEADS
    smem = pl.BlockSpec(memory_space=pltpu.SMEM)
    new = pl.BlockSpec((1, tq, width), lambda b: (b, 0, 0))
    old = pl.BlockSpec((1, past * N_HEADS, V_DIM), lambda b: (b, 0, 0))
    return pl.pallas_call(
        functools.partial(_attn_dec_body, past=past),
        grid=(batch,),
        in_specs=[smem, smem, new, old, old, new, new,
                  pl.BlockSpec((1, V_DIM), lambda b: (0, 0))],
        out_specs=new,
        out_shape=jax.ShapeDtypeStruct((batch, tq, width), BF16),
        compiler_params=_cparams(("arbitrary",)),
        name="attn_sample",
    )(slopes, lam, q, cache_k, cache_v, kn, vn, sg)


def _outproj_body(at_ref, sa_ref, gp_ref, x_ref, wo_ref, g_ref, rw_ref, rb_ref,
                  h_ref, ti_ref, tg_ref, cnt_ref):
    i = pl.program_id(0)
    rows = x_ref.shape[0]
    n_sub = 2 if rows % (2 * LANES) == 0 else 1
    sub = rows // n_sub

    def project(j):
        rs = slice(j * sub, (j + 1) * sub)
        merged = (sa_ref[rs, :].astype(F32) * at_ref[rs, :].astype(F32) + gp_ref[rs, :].astype(F32)).astype(BF16)
        return x_ref[rs, :] + jnp.dot(merged, wo_ref[...], preferred_element_type=F32)

    def route(hh, j):
        for c in range(ROW_SUB):
            h_ref[pl.ds(j * sub * ROW_SUB + c, sub, stride=ROW_SUB), :] = hh[:, c * LANES:(c + 1) * LANES]
        ms = jnp.mean(hh * hh, axis=-1, keepdims=True)
        hn = (hh * lax.rsqrt(ms + RMS_EPS) * g_ref[...]).astype(BF16)
        logits = jnp.dot(hn, rw_ref[...], preferred_element_type=F32) + rb_ref[...]
        lt = jnp.transpose(logits)[0:N_EXPERTS, :]
        e_iota = lax.broadcasted_iota(jnp.int32, (N_EXPERTS, sub), 0)
        vals, idxs, hots = [], [], []
        cur = lt
        for _ in range(TOP_K):
            m = jnp.max(cur, axis=0, keepdims=True)
            idx = jnp.min(jnp.where(cur == m, e_iota, N_EXPERTS), axis=0, keepdims=True)
            hit = e_iota == idx
            vals.append(m)
            idxs.append(idx)
            hots.append(hit)
            cur = jnp.where(hit, -jnp.inf, cur)
        ex = [jnp.exp(v - vals[0]) for v in vals]
        inv = 1.0 / (ex[0] + ex[1] + ex[2] + ex[3])
        zi = jnp.zeros((8 - TOP_K, sub), jnp.int32)
        cs = slice(j * sub, (j + 1) * sub)
        ti_ref[:, cs] = jnp.concatenate(idxs + [zi], axis=0)
        wide = jnp.concatenate([jnp.broadcast_to(e * inv, (GATE_LANES, sub)) for e in ex], axis=0)
        tg_ref[cs, :] = jnp.transpose(wide)
        hot = jnp.concatenate([hh_.astype(F32) for hh_ in hots], axis=0)
        return jnp.sum(hot, axis=1, keepdims=True)

    hs = [project(j) for j in range(n_sub)]
    csum = route(hs[0], 0)
    for j in range(1, n_sub):
        csum = csum + route(hs[j], j)

    @pl.when(i == 0)
    def _():
        cnt_ref[...] = jnp.zeros_like(cnt_ref)

    cnt_ref[...] += jnp.broadcast_to(csum, cnt_ref.shape)


def _outproj(attn, sa, gp, x, w_out, g, rw, rb):
    n_tok, d_model = x.shape
    tm = min(2 * ROW_TILE, n_tok)
    row = lambda i: (i, 0)
    fixed = lambda i: (0, 0)
    colb = lambda i: (0, i)
    return pl.pallas_call(
        _outproj_body,
        grid=(n_tok // tm,),
        in_specs=[
            pl.BlockSpec((tm, d_model), row),
            pl.BlockSpec((tm, d_model), row),
            pl.BlockSpec((tm, d_model), row),
            pl.BlockSpec((tm, d_model), row),
            pl.BlockSpec((d_model, d_model), fixed),
            pl.BlockSpec((1, d_model), fixed),
            pl.BlockSpec((d_model, LANES), fixed),
            pl.BlockSpec((1, LANES), fixed),
        ],
        out_specs=[
            pl.BlockSpec((tm * ROW_SUB, LANES), row),
            pl.BlockSpec((8, tm), colb),
            pl.BlockSpec((tm, LANES), row),
            pl.BlockSpec((TOP_K * N_EXPERTS, LANES), fixed),
        ],
        out_shape=[
            jax.ShapeDtypeStruct((n_tok * ROW_SUB, LANES), F32),
            jax.ShapeDtypeStruct((8, n_tok), jnp.int32),
            jax.ShapeDtypeStruct((n_tok, LANES), F32),
            jax.ShapeDtypeStruct((TOP_K * N_EXPERTS, LANES), F32),
        ],
        compiler_params=_cparams(("arbitrary",)),
        name="outproj",
    )(attn, sa, gp, x, w_out, g, rw, rb)


def _rank_body(ti_ref, base_ref, tri_ref, dest_ref, carry_ref):
    i = pl.program_id(0)

    @pl.when(i == 0)
    def _():
        carry_ref[...] = jnp.zeros_like(carry_ref)

    tt = ti_ref.shape[1]
    e_iota = lax.broadcasted_iota(jnp.int32, (N_EXPERTS, tt), 0)
    hot = jnp.concatenate([(ti_ref[k:k + 1, :] == e_iota).astype(F32) for k in range(TOP_K)], axis=0)
    incl = jnp.dot(hot.astype(BF16), tri_ref[...], preferred_element_type=F32)
    slot = base_ref[:, 0:1] + carry_ref[:, 0:1] + incl - 1.0
    picked = hot * slot
    rows = [jnp.sum(picked[k * N_EXPERTS:(k + 1) * N_EXPERTS, :], axis=0, keepdims=True)
            for k in range(TOP_K)]
    rows.append(jnp.zeros((8 - TOP_K, tt), F32))
    dest_ref[...] = jnp.concatenate(rows, axis=0).astype(jnp.int32)
    carry_ref[...] += jnp.broadcast_to(jnp.sum(hot, axis=1, keepdims=True), carry_ref.shape)


def _rank(topi, base, tri):
    n_tok = topi.shape[1]
    tt = tri.shape[0]
    return pl.pallas_call(
        _rank_body,
        grid=(n_tok // tt,),
        in_specs=[
            pl.BlockSpec((8, tt), lambda i: (0, i)),
            pl.BlockSpec(base.shape, lambda i: (0, 0)),
            pl.BlockSpec((tt, tt), lambda i: (0, 0)),
        ],
        out_specs=pl.BlockSpec((8, tt), lambda i: (0, i)),
        out_shape=jax.ShapeDtypeStruct((8, n_tok), jnp.int32),
        scratch_shapes=[pltpu.VMEM((TOP_K * N_EXPERTS, LANES), F32)],
        compiler_params=_cparams(("arbitrary",)),
        name="rank",
    )(topi, base, tri)


DISPATCH_BUFS = 3


def _slab_copy(src, src_row, dst, dst_row, sem):
    def first_sublane(row):
        return row * ROW_SUB if isinstance(row, int) else pl.multiple_of(row * ROW_SUB, ROW_SUB)

    return pltpu.make_async_copy(src.at[pl.ds(first_sublane(src_row), ROW_SUB)],
                                 dst.at[pl.ds(first_sublane(dst_row), ROW_SUB)], sem)


def _dispatch_body(zs_ref, zl_ref, tz_ref, dest_hbm, hp_hbm, hs_hbm, xs_hbm,
                   idx0, idx1, idx2, hb0, hb1, hb2, lsem, dsem, zsem,
                   *, tt, n_prompt_tiles, n_out_tiles):
    i = pl.program_id(0)
    n = pl.num_programs(0)
    idx = (idx0, idx1, idx2)
    hbuf = (hb0, hb1, hb2)
    nbuf = DISPATCH_BUFS
    tile_sub = tt * ROW_SUB

    def load_wait(s):
        pltpu.make_async_copy(dest_hbm.at[0], idx[s], lsem.at[s]).wait()
        pltpu.make_async_copy(hp_hbm.at[pl.ds(0, tile_sub)], hbuf[s], lsem.at[s]).wait()

    def load_start(tile, s):
        pltpu.make_async_copy(dest_hbm.at[tile], idx[s], lsem.at[s]).start()

        @pl.when(tile < n_prompt_tiles)
        def _():
            r0 = pl.multiple_of(tile * tile_sub, tile_sub)
            pltpu.make_async_copy(hp_hbm.at[pl.ds(r0, tile_sub)], hbuf[s], lsem.at[s]).start()

        @pl.when(tile >= n_prompt_tiles)
        def _():
            r0 = pl.multiple_of((tile - n_prompt_tiles) * tile_sub, tile_sub)
            pltpu.make_async_copy(hs_hbm.at[pl.ds(r0, tile_sub)], hbuf[s], lsem.at[s]).start()

    def scatter_start(s):
        for k in range(TOP_K):
            for r in range(tt):
                dst0 = pl.multiple_of(idx[s][k * tt + r], ROW_SUB)
                pltpu.make_async_copy(hbuf[s].at[pl.ds(r * ROW_SUB, ROW_SUB)], xs_hbm.at[pl.ds(dst0, ROW_SUB)],
                                      dsem.at[s]).start(priority=r % 2)

    def scatter_wait(s):
        for _ in range(TOP_K):
            pltpu.make_async_copy(hbuf[s], xs_hbm.at[pl.ds(0, tile_sub)], dsem.at[s]).wait()

    @pl.when(i == 0)
    def _():
        zero = hbuf[nbuf - 1]
        zero[...] = jnp.zeros_like(zero)

        def pad_pieces(e, act):
            length = zl_ref[e]
            for shift in range(tt.bit_length() - 1, -1, -1):
                piece = 1 << shift
                first = zs_ref[e] + jnp.bitwise_and(length, ~(2 * piece - 1))

                @pl.when(jnp.bitwise_and(length, piece) != 0)
                def _(piece=piece, first=first):
                    dst0 = pl.multiple_of(first * ROW_SUB, ROW_SUB)
                    act(pltpu.make_async_copy(zero.at[pl.ds(0, piece * ROW_SUB)],
                                              xs_hbm.at[pl.ds(dst0, piece * ROW_SUB)], zsem))

        def per_expert(e, c):
            pad_pieces(e, lambda cp: cp.start())
            pad_pieces(e, lambda cp: cp.wait())
            return c
        lax.fori_loop(0, N_EXPERTS, per_expert, 0)

        def tail_copy(j):
            r0 = pl.multiple_of(j * tile_sub, tile_sub)
            return pltpu.make_async_copy(zero, xs_hbm.at[pl.ds(r0, tile_sub)], zsem)

        def tail_one(j, c):
            tail_copy(j).start()
            tail_copy(j).wait()
            return c
        lax.fori_loop(tz_ref[0], n_out_tiles, tail_one, 0)

        load_start(0, 0)

        @pl.when(n > 1)
        def _():
            load_start(1, 1)

    for s in range(nbuf):
        @pl.when(i % nbuf == s)
        def _(s=s):
            prev = (s + nbuf - 1) % nbuf
            load_wait(s)
            scatter_start(s)

            @pl.when(i > 0)
            def _():
                scatter_wait(prev)

            @pl.when(i + 2 < n)
            def _():
                load_start(i + 2, prev)

            @pl.when(i == n - 1)
            def _():
                scatter_wait(s)


def _dispatch(zero_start, zero_len, tail_tile, dest_tiles, hp, hs, n_rows):
    n_tiles, width = dest_tiles.shape
    tt = width // TOP_K
    n_prompt_tiles = hp.shape[0] // (tt * ROW_SUB)
    any_spec = pl.BlockSpec(memory_space=pl.ANY)
    grid_spec = pltpu.PrefetchScalarGridSpec(
        num_scalar_prefetch=3,
        grid=(n_tiles,),
        in_specs=[any_spec, any_spec, any_spec],
        out_specs=any_spec,
        scratch_shapes=[pltpu.SMEM((width,), jnp.int32)] * DISPATCH_BUFS
        + [pltpu.VMEM((tt * ROW_SUB, LANES), F32)] * DISPATCH_BUFS
        + [
            pltpu.SemaphoreType.DMA((DISPATCH_BUFS,)),
            pltpu.SemaphoreType.DMA((DISPATCH_BUFS,)),
            pltpu.SemaphoreType.DMA,
        ],
    )
    return pl.pallas_call(
        functools.partial(_dispatch_body, tt=tt, n_prompt_tiles=n_prompt_tiles, n_out_tiles=n_rows // tt),
        grid_spec=grid_spec,
        out_shape=jax.ShapeDtypeStruct((n_rows * ROW_SUB, LANES), F32),
        compiler_params=_cparams(("arbitrary",)),
        name="dispatch",
    )(zero_start, zero_len, tail_tile, dest_tiles, hp, hs)


def _moe_body(te_ref, nu_ref, nx_ref, x_ref, g_ref, wg_hbm, wu_hbm, wd_hbm, b_ref,
              y_ref, wgs, wus, wds, wgb, wub, wdb, wsem, *, tm):
    i = pl.program_id(0)
    n_used = nu_ref[0]

    def weight_copies(e):
        return (pltpu.make_async_copy(wg_hbm.at[e], wgs, wsem),
                pltpu.make_async_copy(wu_hbm.at[e], wus, wsem),
                pltpu.make_async_copy(wd_hbm.at[e], wds, wsem))

    @pl.when(i >= n_used)
    def _():
        y_ref[...] = jnp.zeros_like(y_ref)

    @pl.when(i < n_used)
    def _():
        @pl.when(i == 0)
        def _():
            for cp in weight_copies(te_ref[0]):
                cp.start()

        prev = te_ref[jnp.maximum(i - 1, 0)]
        changed = jnp.logical_or(i == 0, te_ref[i] != prev)

        @pl.when(changed)
        def _():
            for cp in weight_copies(te_ref[i]):
                cp.wait()
            wgb[...] = wgs[...].astype(BF16)
            wub[...] = wus[...].astype(BF16)
            wdb[...] = wds[...].astype(BF16)

            @pl.when(nx_ref[i] >= 0)
            def _():
                for cp in weight_copies(nx_ref[i]):
                    cp.start()

        half = tm // MOE_SUBTILES
        d_model = ROW_SUB * LANES
        bias = b_ref[te_ref[i]]

        def normed_half(r0):
            cols = [x_ref[pl.ds(r0 * ROW_SUB + c, half, stride=ROW_SUB), :] for c in range(ROW_SUB)]
            ssq = cols[0] * cols[0]
            for xc in cols[1:]:
                ssq = ssq + xc * xc
            r = lax.rsqrt(jnp.sum(ssq, axis=-1, keepdims=True) * (1.0 / d_model) + RMS_EPS)
            return jnp.concatenate(
                [(xc * r * g_ref[:, c * LANES:(c + 1) * LANES]).astype(BF16) for c, xc in enumerate(cols)], axis=1)

        def gate_up(xb):
            gt = jnp.dot(xb, wgb[...], preferred_element_type=F32) + bias[0:1, :]
            up = jnp.dot(xb, wub[...], preferred_element_type=F32) + bias[1:2, :]
            return gt, up

        def down(gt, up, r0):
            gt = jnp.minimum(gt, SWIGLU_LIMIT)
            up = jnp.clip(up, -SWIGLU_LIMIT, SWIGLU_LIMIT)
            hdn = (up + 1.0) * (gt * _sigmoid(SWIGLU_ALPHA * gt))
            y = jnp.dot(hdn.astype(BF16), wdb[...], preferred_element_type=F32) + bias[2:3, :]
            for c in range(ROW_SUB):
                y_ref[pl.ds(r0 * ROW_SUB + c, half, stride=ROW_SUB), :] = y[:, c * LANES:(c + 1) * LANES]

        pending = gate_up(normed_half(0))
        for j in range(MOE_SUBTILES):
            cur = pending
            if j + 1 < MOE_SUBTILES:
                pending = gate_up(normed_half((j + 1) * half))
            down(cur[0], cur[1], j * half)


def _moe(tile_expert, n_used, next_expert, x_rows, g, w_gate, b_gate, w_up, b_up, w_down, b_down, *, tm):
    n_tiles = x_rows.shape[0] // (tm * ROW_SUB)
    d_model = ROW_SUB * LANES
    d_ff = w_gate.shape[2]
    assert w_gate.shape[1] == d_model and d_ff == d_model
    biases = jnp.stack([b_gate, b_up, b_down], axis=1).astype(F32)
    any_spec = pl.BlockSpec(memory_space=pl.ANY)
    grid_spec = pltpu.PrefetchScalarGridSpec(
        num_scalar_prefetch=3,
        grid=(n_tiles,),
        in_specs=[
            pl.BlockSpec((tm * ROW_SUB, LANES), lambda i, te, nu, nx: (jnp.minimum(i, nu[0] - 1), 0)),
            pl.BlockSpec((1, d_model), lambda i, te, nu, nx: (0, 0)),
            any_spec, any_spec, any_spec,
            pl.BlockSpec(biases.shape, lambda i, te, nu, nx: (0, 0, 0)),
        ],
        out_specs=pl.BlockSpec((tm * ROW_SUB, LANES), lambda i, te, nu, nx: (i, 0)),
        scratch_shapes=[
            pltpu.VMEM((d_model, d_ff), F32),
            pltpu.VMEM((d_model, d_ff), F32),
            pltpu.VMEM((d_ff, d_model), F32),
            pltpu.VMEM((d_model, d_ff), BF16),
            pltpu.VMEM((d_model, d_ff), BF16),
            pltpu.VMEM((d_ff, d_model), BF16),
            pltpu.SemaphoreType.DMA,
        ],
    )
    return pl.pallas_call(
        functools.partial(_moe_body, tm=tm),
        grid_spec=grid_spec,
        out_shape=jax.ShapeDtypeStruct((n_tiles * tm * ROW_SUB, LANES), F32),
        compiler_params=_cparams(("arbitrary",)),
        name="moe",
    )(tile_expert, n_used, next_expert, x_rows, g, w_gate, w_up, w_down, biases)


def _combine_body(dest_hbm, y_hbm, h_ref, gt_ref, o_ref, idx0, idx1, yb0, yb1, isem, gsem, *, tt):
    i = pl.program_id(0)
    n = pl.num_programs(0)
    idx = (idx0, idx1)
    ybuf = (yb0, yb1)

    def idx_copy(tile, s):
        return pltpu.make_async_copy(dest_hbm.at[tile], idx[s], isem.at[s])

    def issue_gather(s):
        for k in range(TOP_K):
            for r in range(tt):
                src = pl.multiple_of(idx[s][k * tt + r], ROW_SUB)
                pltpu.make_async_copy(y_hbm.at[pl.ds(src, ROW_SUB)],
                                      ybuf[s].at[k, pl.ds(r * ROW_SUB, ROW_SUB)], gsem.at[s]).start(priority=r % 2)

    def wait_gather(s):
        for k in range(TOP_K):
            pltpu.make_async_copy(y_hbm.at[pl.ds(0, tt * ROW_SUB)], ybuf[s].at[k], gsem.at[s]).wait()

    @pl.when(i == 0)
    def _():
        idx_copy(0, 0).start()
        idx_copy(0, 0).wait()
        issue_gather(0)

        @pl.when(n > 1)
        def _():
            idx_copy(1, 1).start()

    for s in range(2):
        @pl.when(i % 2 == s)
        def _(s=s):
            o = 1 - s

            @pl.when(i + 1 < n)
            def _():
                idx_copy(i + 1, o).wait()
                issue_gather(o)

            @pl.when(i + 2 < n)
            def _():
                idx_copy(i + 2, s).start()

            wait_gather(s)
            gts = gt_ref[...]
            for c in range(ROW_SUB):
                cs = slice(c * LANES, (c + 1) * LANES)
                acc = _load_slab_cols(h_ref, tt, c)
                for k in range(TOP_K):
                    gate = gts[:, k * GATE_LANES:k * GATE_LANES + 1]
                    acc = acc + gate * ybuf[s][k, pl.ds(c, tt, stride=ROW_SUB), :]
                o_ref[:, cs] = acc


def _combine(dest_tiles, y_rows, h, gates_tok):
    n_tiles, width = dest_tiles.shape
    tt = width // TOP_K
    n_tok, d_model = h.shape[0] // ROW_SUB, ROW_SUB * LANES
    any_spec = pl.BlockSpec(memory_space=pl.ANY)
    return pl.pallas_call(
        functools.partial(_combine_body, tt=tt),
        grid=(n_tiles,),
        in_specs=[any_spec, any_spec,
                  pl.BlockSpec((tt * ROW_SUB, LANES), lambda i: (i, 0)),
                  pl.BlockSpec((tt, LANES), lambda i: (i, 0))],
        out_specs=pl.BlockSpec((tt, d_model), lambda i: (i, 0)),
        out_shape=jax.ShapeDtypeStruct((n_tok, d_model), F32),
        scratch_shapes=[
            pltpu.SMEM((TOP_K * tt,), jnp.int32),
            pltpu.SMEM((TOP_K * tt,), jnp.int32),
            pltpu.VMEM((TOP_K, tt * ROW_SUB, LANES), F32),
            pltpu.VMEM((TOP_K, tt * ROW_SUB, LANES), F32),
            pltpu.SemaphoreType.DMA((2,)),
            pltpu.SemaphoreType.DMA((2,)),
        ],
        compiler_params=_cparams(("arbitrary",)),
        name="combine",
    )(dest_tiles, y_rows, h, gates_tok)


def _block_diag_ones():
    r = lax.broadcasted_iota(jnp.int32, (NORM_BLOCK, NORM_BLOCK), 0) // HEAD_DIM
    c = lax.broadcasted_iota(jnp.int32, (NORM_BLOCK, NORM_BLOCK), 1) // HEAD_DIM
    return (r == c).astype(BF16)


def _dest_tiles(dest, tt):
    n_tok = dest.shape[1]
    tiles = dest[:TOP_K].reshape(TOP_K, n_tok // tt, tt).transpose(1, 0, 2).reshape(n_tok // tt, TOP_K * tt)
    return tiles * ROW_SUB


def _layer(xp, xs, ck, cv, sp, lam, lam_init, ng, w_in, qng, kng, slg, pw, ps, w_out, fg,
           rw, rb, w_gate, b_gate, w_up, b_up, w_down, b_down):
    bp, t_len, d_model = xp.shape
    bs, ts, _ = xs.shape
    past = ck.shape[1]
    pool_w = sp.shape[-1]
    n_p, n_s = bp * t_len, bs * ts

    w_in_b = w_in.astype(BF16)
    w_out_b = w_out.astype(BF16)
    pw_b = pw.astype(BF16)
    reps = NORM_BLOCK // HEAD_DIM
    qg = (jnp.tile(qng.astype(F32), reps) * (HEAD_DIM ** -0.5)).reshape(1, NORM_BLOCK)
    kg = jnp.tile(kng.astype(F32), reps).reshape(1, NORM_BLOCK)
    bd = _block_diag_ones()
    ng2 = ng.astype(F32).reshape(1, d_model)
    ps2 = ps.astype(F32).reshape(1, d_model)
    sg = (slg.astype(F32) * (1.0 - lam_init)).reshape(1, V_DIM)
    slopes = jnp.exp2(-(8.0 / N_HEADS) * jnp.arange(1, N_HEADS + 1, dtype=F32))
    lam1 = lam.reshape(1).astype(F32)
    fg2 = fg.astype(F32).reshape(1, d_model)
    rw_b = jnp.zeros((d_model, LANES), BF16).at[:, :N_EXPERTS].set(rw.astype(BF16))
    rb2 = jnp.zeros((1, LANES), F32).at[0, :N_EXPERTS].set(rb.astype(F32))

    tm = min(ROW_TILE, t_len)
    zero_pre = jnp.zeros((bp, POOL_HALO, pool_w), F32)
    qp, kp, kpb, vp, vpb, sap, gpp, utp = _inproj(
        xp, zero_pre, ng2, w_in_b, qg, kg, bd, pw_b, ps2, nseg=1, seg_len=tm, start_pos=0, carry=True)
    atp = _attn_prompt(qp, kpb, vpb, slopes, lam1, sg)
    hp, tip, tgp, cntp = _outproj(atp.reshape(n_p, d_model), sap.reshape(n_p, d_model),
                                  gpp.reshape(n_p, d_model), xp.reshape(n_p, d_model),
                                  w_out_b, fg2, rw_b, rb2)

    pre_s = jnp.concatenate([jnp.zeros((bs, POOL_HALO - sp.shape[1], pool_w), F32), sp.astype(F32)], axis=1)
    qs, ks, ksb, vs, vsb, sas, gps, uts = _inproj(
        xs.reshape(1, n_s, d_model), pre_s, ng2, w_in_b, qg, kg, bd, pw_b, ps2,
        nseg=bs, seg_len=ts, start_pos=past, carry=False)
    ats = _attn_sample(qs.reshape(bs, ts, d_model), ksb.reshape(bs, ts, d_model), vsb.reshape(bs, ts, d_model),
                       ck.reshape(bs, past * N_HEADS, V_DIM), cv.reshape(bs, past * N_HEADS, V_DIM),
                       slopes, lam1, sg)
    hs, tis, tgs, cnts = _outproj(ats.reshape(n_s, d_model), sas.reshape(n_s, d_model),
                                  gps.reshape(n_s, d_model), xs.reshape(n_s, d_model),
                                  w_out_b, fg2, rw_b, rb2)

    n_tok = n_p + n_s
    topi = jnp.concatenate([tip, tis], axis=1)
    cnt = (cntp[:, 0] + cnts[:, 0]).astype(jnp.int32).reshape(TOP_K, N_EXPERTS)
    per_expert = jnp.sum(cnt, axis=0)
    padded = (per_expert + MOE_TILE - 1) // MOE_TILE * MOE_TILE
    pad_end = jnp.cumsum(padded)
    pad_start = pad_end - padded
    base = pad_start[None, :] + jnp.cumsum(cnt, axis=0) - cnt
    base_f = jnp.broadcast_to(base.reshape(-1, 1).astype(F32), (TOP_K * N_EXPERTS, LANES))
    tt = TOK_TILE
    rt = max(m * tt for m in range(1, RANK_TILE_MAX // tt + 1) if n_tok % (m * tt) == 0)
    tri = (lax.broadcasted_iota(jnp.int32, (rt, rt), 0) <= lax.broadcasted_iota(jnp.int32, (rt, rt), 1)).astype(BF16)
    dest = _rank(topi, base_f, tri)

    n_tiles = -(-(n_tok * TOP_K) // MOE_TILE) + N_EXPERTS
    n_used = (pad_end[-1] // MOE_TILE).astype(jnp.int32)
    tile_start = jnp.arange(n_tiles, dtype=jnp.int32) * MOE_TILE
    last_start = jnp.maximum(pad_end[-1] - MOE_TILE, 0)
    tile_expert = jnp.minimum(
        jnp.sum((jnp.minimum(tile_start, last_start)[:, None] >= pad_end[None, :]).astype(jnp.int32), axis=1),
        N_EXPERTS - 1)
    dest_p = _dest_tiles(dest[:, :n_p], tt)
    dest_s = _dest_tiles(dest[:, n_p:], tt)
    dest_all = jnp.concatenate([dest_p, dest_s], axis=0)
    assert MOE_TILE <= 2 * tt
    x_rows = _dispatch(pad_start + per_expert, padded - per_expert, (pad_end[-1:] // tt).astype(jnp.int32),
                       dest_all, hp, hs, n_tiles * MOE_TILE)
    e_ids = jnp.arange(N_EXPERTS, dtype=jnp.int32)
    later_present = jnp.logical_and(e_ids[None, :] > e_ids[:, None], (padded > 0)[None, :])
    next_of = jnp.min(jnp.where(later_present, e_ids[None, :], N_EXPERTS), axis=1)
    next_of = jnp.where(next_of == N_EXPERTS, -1, next_of).astype(jnp.int32)
    next_expert = jnp.sum(jnp.where(tile_expert[:, None] == e_ids[None, :], next_of[None, :], 0), axis=1)
    y_rows = _moe(tile_expert, n_used.reshape(1), next_expert.astype(jnp.int32), x_rows, fg2,
                  w_gate, b_gate, w_up, b_up, w_down, b_down, tm=MOE_TILE)

    yp = _combine(dest_p, y_rows, hp, tgp)
    ys = _combine(dest_s, y_rows, hs, tgs)

    heads = (N_HEADS, V_DIM)
    return (yp.reshape(bp, t_len, d_model), ys.reshape(bs, ts, d_model),
            kp.reshape(bp, t_len, *heads), vp.reshape(bp, t_len, *heads), utp[:, 1:],
            ks.reshape(bs, ts, *heads), vs.reshape(bs, ts, *heads), uts[:, 1:])


def kernel(x_prompt, x_sample, cache_k, cache_v, state_pool, norm_mix_g, w_in, q_norm_g, k_norm_g,
           lambda_q1, lambda_k1, lambda_q2, lambda_k2, subln_g, pool_w, pool_scale, w_out, norm_ffn_g,
           router_w, router_b, w_gate, b_gate, w_up, b_up, w_down, b_down):
    depth = w_in.shape[0]
    hp, hs = x_prompt, x_sample
    outs = [[] for _ in range(6)]
    for layer in range(depth):
        lam_init = 0.8 - 0.6 * math.exp(-0.3 * layer)
        lam = (jnp.exp(jnp.sum(lambda_q1[layer].astype(F32) * lambda_k1[layer].astype(F32)))
               - jnp.exp(jnp.sum(lambda_q2[layer].astype(F32) * lambda_k2[layer].astype(F32)))
               + lam_init)
        hp, hs, kp, vp, up, ks, vs, us = _layer(
            hp, hs, cache_k[layer], cache_v[layer], state_pool[layer], lam, lam_init,
            norm_mix_g[layer], w_in[layer], q_norm_g[layer], k_norm_g[layer], subln_g[layer],
            pool_w[layer], pool_scale[layer], w_out[layer], norm_ffn_g[layer],
            router_w[layer], router_b[layer], w_gate[layer], b_gate[layer], w_up[layer], b_up[layer],
            w_down[layer], b_down[layer])
        for lst, val in zip(outs, (kp, vp, up, ks, vs, us)):
            lst.append(val)
    return (hp, hs) + tuple(jnp.stack(o) for o in outs)
```

```python
import functools
import math

import jax
import jax.numpy as jnp
from jax import lax
from jax.experimental import pallas as pl
from jax.experimental.pallas import tpu as pltpu

F32 = jnp.float32
BF16 = jnp.bfloat16

CHUNK = 64
N_HEADS = 8
HEAD_DIM = 64
V_DIM = 2 * HEAD_DIM
POOL_WINDOWS = (2, 4, 8, 16)
POOL_GROUP_DIM = 128
POOL_OUT_DIM = 256
POOL_HALO = 16
N_EXPERTS = 32
TOP_K = 4
SWIGLU_LIMIT = 7.0
SWIGLU_ALPHA = 1.702
RMS_EPS = 1e-6
NEG_INF = -1e30

LANES = 128
ROW_SUB = 8
NORM_BLOCK = 256
VMEM_LIMIT = 56 * 1024 * 1024

ROW_TILE = 512
ATTN_TILE = 256
KEY_CHUNK = 512
MOE_TILE = 512
MOE_SUBTILES = 2
GATE_LANES = LANES // TOP_K
TOK_TILE = 256
RANK_TILE_MAX = 1024


def _sigmoid(x):
    return 1.0 / (1.0 + jnp.exp(-x))


def _cparams(sem):
    return pltpu.CompilerParams(dimension_semantics=sem, vmem_limit_bytes=VMEM_LIMIT)


def _store_slabs(ref, val, row0=0):
    rows = val.shape[0]
    for c in range(ROW_SUB):
        ref[pl.ds(row0 * ROW_SUB + c, rows, stride=ROW_SUB), :] = val[:, c * LANES:(c + 1) * LANES]


def _load_slab_cols(ref, rows, c, row0=0):
    return ref[pl.ds(row0 * ROW_SUB + c, rows, stride=ROW_SUB), :]


def _inproj_body(x_ref, pre_ref, ng_ref, w_ref, qg_ref, kg_ref, bd_ref, pw_ref, ps_ref,
                 q_ref, k_ref, kb_ref, v_ref, vb_ref, sa_ref, gp_ref, ut_ref, ext_ref,
                 *, nseg, seg_len, start_pos, carry, attn_w, pool_w):
    rows = nseg * seg_len
    t = pl.program_id(1)
    x = x_ref[0]
    ms = jnp.mean(x * x, axis=-1, keepdims=True)
    xn = (x * lax.rsqrt(ms + RMS_EPS) * ng_ref[...]).astype(BF16)

    def proj(c0, width):
        return jnp.dot(xn, w_ref[:, c0:c0 + width], preferred_element_type=F32)

    bd = bd_ref[...]

    def group_norm(p, g_ref):
        ss = jnp.dot((p * p).astype(BF16), bd, preferred_element_type=F32)
        return p * lax.rsqrt(ss * (1.0 / HEAD_DIM) + RMS_EPS) * g_ref[...]

    nb = NORM_BLOCK
    heads_per_block = nb // V_DIM

    def store_heads(ref, val, c):
        for j in range(heads_per_block):
            head = c * heads_per_block + j
            ref[0, pl.ds(head, rows, stride=N_HEADS), :] = val[:, j * V_DIM:(j + 1) * V_DIM]

    def sink_q(val, c):
        q_ref[0, :, c * nb:(c + 1) * nb] = group_norm(val, qg_ref).astype(BF16)

    def sink_k(val, c):
        kn = group_norm(val, kg_ref)
        store_heads(k_ref, kn, c)
        kb_ref[0, :, c * nb:(c + 1) * nb] = kn.astype(BF16)

    def sink_v(val, c):
        store_heads(v_ref, val, c)
        vb_ref[0, :, c * nb:(c + 1) * nb] = val.astype(BF16)

    n_chunks = attn_w // nb
    work = [(part * attn_w + c * nb, sink, c)
            for part, sink in enumerate((sink_q, sink_k, sink_v)) for c in range(n_chunks)]
    pending = proj(work[0][0], nb)
    for j, (_, sink, c) in enumerate(work):
        cur = pending
        if j + 1 < len(work):
            pending = proj(work[j + 1][0], nb)
        sink(cur, c)

    if carry:
        @pl.when(t == 0)
        def _():
            ext_ref[:, 0:POOL_HALO, :] = pre_ref[...]
    else:
        ext_ref[:, 0:POOL_HALO, :] = pre_ref[...]
    for c in range(pool_w // nb):
        cs = slice(c * nb, (c + 1) * nb)
        u = proj(3 * attn_w + c * nb, nb)
        ext_ref[:, POOL_HALO:POOL_HALO + seg_len, cs] = u.reshape(nseg, seg_len, nb)

    ga0 = 3 * attn_w + pool_w
    d_model = attn_w
    for c in range(d_model // nb):
        cs = slice(c * nb, (c + 1) * nb)
        sa_ref[0, :, cs] = _sigmoid(proj(ga0 + c * nb, nb)).astype(BF16)

    gb0 = ga0 + d_model
    row = lax.broadcasted_iota(jnp.int32, (1, seg_len, 1), 1)
    pos = row + start_pos
    if carry:
        pos = pos + t * seg_len
    for g, w in enumerate(POOL_WINDOWS):
        cs = slice(g * POOL_GROUP_DIM, (g + 1) * POOL_GROUP_DIM)
        own = ext_ref[:, POOL_HALO:POOL_HALO + seg_len, cs]
        acc = own
        for i in range(1, w):
            acc = acc + ext_ref[:, POOL_HALO - i:POOL_HALO - i + seg_len, cs]
        inv = 1.0 / jnp.minimum(w, pos + 1).astype(F32)
        z = (acc * inv - own).reshape(rows, POOL_GROUP_DIM)
        os_ = slice(g * POOL_OUT_DIM, (g + 1) * POOL_OUT_DIM)
        yp = jnp.dot(z.astype(BF16), pw_ref[g], preferred_element_type=F32) * ps_ref[:, os_]
        gb = proj(gb0 + g * POOL_OUT_DIM, POOL_OUT_DIM)
        gp_ref[0, :, os_] = (_sigmoid(gb) * yp).astype(BF16)

    tail = ext_ref[:, seg_len:seg_len + POOL_HALO, :]
    ut_ref[...] = tail
    if carry:
        ext_ref[:, 0:POOL_HALO, :] = tail


def _inproj(x3, prefix, ng, w_in, qg, kg, bd, pw, ps, *, nseg, seg_len, start_pos, carry):
    groups, t_len, d_model = x3.shape
    rows = nseg * seg_len
    steps = t_len // rows
    in_cols = w_in.shape[1]
    pool_w = prefix.shape[-1]
    attn_w = d_model
    assert in_cols == 3 * attn_w + pool_w + 2 * d_model
    tok = lambda b, t: (b, t, 0)
    fixed2 = lambda b, t: (0, 0)
    act = lambda dt: jax.ShapeDtypeStruct((groups, t_len, d_model), dt)
    by_head = jax.ShapeDtypeStruct((groups, t_len * N_HEADS, V_DIM), F32)
    tok_spec = pl.BlockSpec((1, rows, d_model), tok)
    head_spec = pl.BlockSpec((1, rows * N_HEADS, V_DIM), tok)
    body = functools.partial(_inproj_body, nseg=nseg, seg_len=seg_len, start_pos=start_pos,
                             carry=carry, attn_w=attn_w, pool_w=pool_w)
    return pl.pallas_call(
        body,
        grid=(groups, steps),
        in_specs=[
            pl.BlockSpec((1, rows, d_model), tok),
            pl.BlockSpec((nseg, POOL_HALO, pool_w), lambda b, t: (b, 0, 0)),
            pl.BlockSpec((1, d_model), fixed2),
            pl.BlockSpec((d_model, in_cols), fixed2, pipeline_mode=pl.Buffered(1)),
            pl.BlockSpec((1, NORM_BLOCK), fixed2),
            pl.BlockSpec((1, NORM_BLOCK), fixed2),
            pl.BlockSpec((NORM_BLOCK, NORM_BLOCK), fixed2),
            pl.BlockSpec(pw.shape, lambda b, t: (0, 0, 0)),
            pl.BlockSpec((1, d_model), fixed2),
        ],
        out_specs=[tok_spec, head_spec, tok_spec, head_spec, tok_spec, tok_spec, tok_spec,
                   pl.BlockSpec((nseg, POOL_HALO, pool_w), lambda b, t: (b, 0, 0))],
        out_shape=[act(BF16), by_head, act(BF16), by_head, act(BF16), act(BF16), act(BF16),
                   jax.ShapeDtypeStruct(prefix.shape, F32)],
        scratch_shapes=[pltpu.VMEM((nseg, POOL_HALO + seg_len, pool_w), F32)],
        compiler_params=_cparams(("arbitrary", "arbitrary")),
        name="inproj",
    )(x3, prefix, ng, w_in, qg, kg, bd, pw, ps)


def _half_masks(q):
    lane = lax.broadcasted_iota(jnp.int32, q.shape, 1)
    zero = jnp.zeros_like(q)
    return jnp.where(lane < HEAD_DIM, q, zero), jnp.where(lane >= HEAD_DIM, q, zero)


def _qk(qz, kblk):
    return lax.dot_general(qz, kblk, (((1,), (1,)), ((), ())), preferred_element_type=F32)


def _subln(o, sg_ref):
    ms = jnp.mean(o * o, axis=-1, keepdims=True)
    return o * lax.rsqrt(ms + RMS_EPS) * sg_ref[...]


def _attn_body(slope_ref, lam_ref, q_ref, k_ref, v_ref, ka_ref, sg_ref, o_ref,
               kf_ref, vf_ref, s_ref, p_ref, *, tq, nq):
    h = pl.program_id(1)
    slope = slope_ref[h]
    lam = lam_ref[0]
    lane = lax.broadcasted_iota(jnp.int32, (tq, LANES), 1)
    kf_ref[:, 0:V_DIM] = k_ref[0]
    kf_ref[:, V_DIM:] = ka_ref[...]
    vf_ref[:, 0:V_DIM] = v_ref[0]
    t_len = vf_ref.shape[0]
    vf_ref[:, V_DIM:] = (lax.broadcasted_iota(jnp.int32, (t_len, LANES), 1) == 0).astype(BF16)

    row = lax.broadcasted_iota(jnp.int32, (tq, tq), 0)
    col = lax.broadcasted_iota(jnp.int32, (tq, tq), 1)
    rc = (row - col).astype(F32)
    vis = lax.shift_right_logical(col, 6) <= lax.shift_right_logical(row, 6)
    corr = jnp.where(vis, jnp.minimum(rc, 0.0) * (2.0 * slope), NEG_INF)

    def scores(qi, m):
        nk = (qi + 1) * tq
        qz = _half_masks(q_ref[0, qi * tq:(qi + 1) * tq, :])[m]
        t = lax.broadcasted_iota(jnp.int32, (tq, LANES), 0) + qi * tq
        t_hi = lax.shift_left(lax.shift_right_logical(t, 8), 8).astype(F32)
        t_lo = jnp.bitwise_and(t, 255).astype(F32)
        qaug = jnp.where(lane == 0, -slope * t_hi,
                         jnp.where(lane == 1, -slope * t_lo,
                                   jnp.where(lane < 4, slope, 0.0))).astype(BF16)
        qa = jnp.concatenate([qz, qaug], axis=1)
        for c0 in range(0, nk, KEY_CHUNK):
            c1 = min(c0 + KEY_CHUNK, nk)
            s_ref[qi % 2, m, :, c0:c1] = _qk(qa, kf_ref[c0:c1, :])

    def softmax_pv(qi, m):
        nk = (qi + 1) * tq
        b = qi % 2
        s_ref[b, m, :, nk - tq:nk] += corr
        chunks = [(c0, min(c0 + KEY_CHUNK, nk)) for c0 in range(0, nk, KEY_CHUNK)]
        mx = None
        for c0, c1 in chunks:
            part = jnp.max(s_ref[b, m, :, c0:c1], axis=-1, keepdims=True)
            mx = part if mx is None else jnp.maximum(mx, part)
        for c0, c1 in chunks:
            p_ref[m, :, c0:c1] = jnp.exp(s_ref[b, m, :, c0:c1] - mx).astype(BF16)
        return jnp.dot(p_ref[m, :, 0:nk], vf_ref[0:nk, :], preferred_element_type=F32)

    scores(0, 0)
    scores(0, 1)
    for qi in range(nq):
        if qi + 1 < nq:
            scores(qi + 1, 0)
            scores(qi + 1, 1)
        o1, o2 = softmax_pv(qi, 0), softmax_pv(qi, 1)
        c1 = 1.0 / o1[:, V_DIM:V_DIM + 1]
        c2 = lam / o2[:, V_DIM:V_DIM + 1]
        o = o1[:, 0:V_DIM] * c1 - o2[:, 0:V_DIM] * c2
        o_ref[0, qi * tq:(qi + 1) * tq, :] = _subln(o, sg_ref).astype(BF16)


def _attn_prompt(q, kb, vb, slopes, lam, sg):
    batch, t_len, width = q.shape
    tq = min(ATTN_TILE, t_len)
    nq = t_len // tq
    pos = jnp.arange(t_len, dtype=jnp.int32)
    ka = jnp.zeros((t_len, LANES), F32)
    ka = ka.at[:, 0:2].set(1.0).at[:, 2].set(((pos >> 8) << 8).astype(F32)).at[:, 3].set((pos & 255).astype(F32))
    smem = pl.BlockSpec(memory_space=pltpu.SMEM)
    seq = pl.BlockSpec((1, t_len, V_DIM), lambda b, h: (b, 0, h))
    return pl.pallas_call(
        functools.partial(_attn_body, tq=tq, nq=nq),
        grid=(batch, N_HEADS),
        in_specs=[
            smem, smem, seq, seq, seq,
            pl.BlockSpec((t_len, LANES), lambda b, h: (0, 0)),
            pl.BlockSpec((1, V_DIM), lambda b, h: (0, 0)),
        ],
        out_specs=seq,
        out_shape=jax.ShapeDtypeStruct((batch, t_len, width), BF16),
        scratch_shapes=[
            pltpu.VMEM((t_len, V_DIM + LANES), BF16),
            pltpu.VMEM((t_len, V_DIM + LANES), BF16),
            pltpu.VMEM((2, 2, tq, t_len), F32),
            pltpu.VMEM((2, tq, t_len), BF16),
        ],
        compiler_params=_cparams(("arbitrary", "arbitrary")),
        name="attn_prompt",
    )(slopes, lam, q, kb, vb, ka.astype(BF16), sg)


def _attn_dec_body(slope_ref, lam_ref, q_ref, kc_ref, vc_ref, kn_ref, vn_ref, sg_ref, o_ref, *, past):
    lam = lam_ref[0]
    tq = q_ref.shape[1]
    qpos_a = lax.broadcasted_iota(jnp.int32, (tq, past), 0) + past
    kpos_a = lax.broadcasted_iota(jnp.int32, (tq, past), 1)
    qpos_b = lax.broadcasted_iota(jnp.int32, (tq, tq), 0) + past
    kpos_b = lax.broadcasted_iota(jnp.int32, (tq, tq), 1) + past

    def dist_and_vis(qpos, kpos):
        vis = lax.shift_right_logical(kpos, 6) <= lax.shift_right_logical(qpos, 6)
        return jnp.abs(qpos - kpos).astype(F32), vis

    dist_a, vis_a = dist_and_vis(qpos_a, kpos_a)
    dist_b, vis_b = dist_and_vis(qpos_b, kpos_b)

    for h in range(N_HEADS):
        cs = slice(h * V_DIM, (h + 1) * V_DIM)
        slope = slope_ref[h]
        bias_a = jnp.where(vis_a, -slope * dist_a, NEG_INF)
        bias_b = jnp.where(vis_b, -slope * dist_b, NEG_INF)
        q1z, q2z = _half_masks(q_ref[0, :, cs])
        kc = kc_ref[0, pl.ds(h, past, stride=N_HEADS), :].astype(BF16)
        vc = vc_ref[0, pl.ds(h, past, stride=N_HEADS), :].astype(BF16)
        kn = kn_ref[0, :, cs]

        qcat = jnp.concatenate([q1z, q2z], axis=0)
        sa_both = _qk(qcat, kc)
        sb_both = _qk(qcat, kn)

        def softmax_parts(m_idx):
            rs = slice(m_idx * tq, (m_idx + 1) * tq)
            sa = sa_both[rs, :] + bias_a
            sb = sb_both[rs, :] + bias_b
            m = jnp.maximum(jnp.max(sa, axis=-1, keepdims=True), jnp.max(sb, axis=-1, keepdims=True))
            pa = jnp.exp(sa - m)
            pb = jnp.exp(sb - m)
            l = jnp.sum(pa, axis=-1, keepdims=True) + jnp.sum(pb, axis=-1, keepdims=True)
            return pa, pb, l

        pa1, pb1, l1 = softmax_parts(0)
        pa2, pb2, l2 = softmax_parts(1)
        c1 = 1.0 / l1
        c2 = lam / l2
        wa = (pa1 * c1 - pa2 * c2).astype(BF16)
        wb = (pb1 * c1 - pb2 * c2).astype(BF16)
        o = (jnp.dot(wa, vc, preferred_element_type=F32)
             + jnp.dot(wb, vn_ref[0, :, cs], preferred_element_type=F32))
        o_ref[0, :, cs] = _subln(o, sg_ref).astype(BF16)


def _attn_sample(q, kn, vn, cache_k, cache_v, slopes, lam, sg):
    batch, tq, width = q.shape
    past = cache_k.shape[1] // N_HEADS
    smem = pl.BlockSpec(memory_space=pltpu.SMEM)
    new = pl.BlockSpec((1, tq, width), lambda b: (b, 0, 0))
    old = pl.BlockSpec((1, past * N_HEADS, V_DIM), lambda b: (b, 0, 0))
    return pl.pallas_call(
        functools.partial(_attn_dec_body, past=past),
        grid=(batch,),
        in_specs=[smem, smem, new, old, old, new, new,
                  pl.BlockSpec((1, V_DIM), lambda b: (0, 0))],
        out_specs=new,
        out_shape=jax.ShapeDtypeStruct((batch, tq, width), BF16),
        compiler_params=_cparams(("arbitrary",)),
        name="attn_sample",
    )(slopes, lam, q, cache_k, cache_v, kn, vn, sg)


def _outproj_body(at_ref, sa_ref, gp_ref, x_ref, wo_ref, g_ref, rw_ref, rb_ref,
                  h_ref, ti_ref, tg_ref, cnt_ref):
    i = pl.program_id(0)
    rows = x_ref.shape[0]
    n_sub = 2 if rows % (2 * LANES) == 0 else 1
    sub = rows // n_sub

    def project(j):
        rs = slice(j * sub, (j + 1) * sub)
        merged = (sa_ref[rs, :].astype(F32) * at_ref[rs, :].astype(F32) + gp_ref[rs, :].astype(F32)).astype(BF16)
        return x_ref[rs, :] + jnp.dot(merged, wo_ref[...], preferred_element_type=F32)

    def route(hh, j):
        _store_slabs(h_ref, hh, j * sub)
        ms = jnp.mean(hh * hh, axis=-1, keepdims=True)
        hn = (hh * lax.rsqrt(ms + RMS_EPS) * g_ref[...]).astype(BF16)
        logits = jnp.dot(hn, rw_ref[...], preferred_element_type=F32) + rb_ref[...]
        lt = jnp.transpose(logits)[0:N_EXPERTS, :]
        e_iota = lax.broadcasted_iota(jnp.int32, (N_EXPERTS, sub), 0)
        vals, idxs, hots = [], [], []
        cur = lt
        for _ in range(TOP_K):
            m = jnp.max(cur, axis=0, keepdims=True)
            idx = jnp.min(jnp.where(cur == m, e_iota, N_EXPERTS), axis=0, keepdims=True)
            hit = e_iota == idx
            vals.append(m)
            idxs.append(idx)
            hots.append(hit)
            cur = jnp.where(hit, -jnp.inf, cur)
        ex = [jnp.exp(v - vals[0]) for v in vals]
        inv = 1.0 / (ex[0] + ex[1] + ex[2] + ex[3])
        zi = jnp.zeros((8 - TOP_K, sub), jnp.int32)
        cs = slice(j * sub, (j + 1) * sub)
        ti_ref[:, cs] = jnp.concatenate(idxs + [zi], axis=0)
        wide = jnp.concatenate([jnp.broadcast_to(e * inv, (GATE_LANES, sub)) for e in ex], axis=0)
        tg_ref[cs, :] = jnp.transpose(wide)
        hot = jnp.concatenate([hh_.astype(F32) for hh_ in hots], axis=0)
        return jnp.sum(hot, axis=1, keepdims=True)

    hs = [project(j) for j in range(n_sub)]
    csum = route(hs[0], 0)
    for j in range(1, n_sub):
        csum = csum + route(hs[j], j)

    @pl.when(i == 0)
    def _():
        cnt_ref[...] = jnp.zeros_like(cnt_ref)

    cnt_ref[...] += jnp.broadcast_to(csum, cnt_ref.shape)


def _outproj(attn, sa, gp, x, w_out, g, rw, rb):
    n_tok, d_model = x.shape
    tm = min(2 * ROW_TILE, n_tok)
    row = lambda i: (i, 0)
    fixed = lambda i: (0, 0)
    colb = lambda i: (0, i)
    return pl.pallas_call(
        _outproj_body,
        grid=(n_tok // tm,),
        in_specs=[
            pl.BlockSpec((tm, d_model), row),
            pl.BlockSpec((tm, d_model), row),
            pl.BlockSpec((tm, d_model), row),
            pl.BlockSpec((tm, d_model), row),
            pl.BlockSpec((d_model, d_model), fixed),
            pl.BlockSpec((1, d_model), fixed),
            pl.BlockSpec((d_model, LANES), fixed),
            pl.BlockSpec((1, LANES), fixed),
        ],
        out_specs=[
            pl.BlockSpec((tm * ROW_SUB, LANES), row),
            pl.BlockSpec((8, tm), colb),
            pl.BlockSpec((tm, LANES), row),
            pl.BlockSpec((TOP_K * N_EXPERTS, LANES), fixed),
        ],
        out_shape=[
            jax.ShapeDtypeStruct((n_tok * ROW_SUB, LANES), F32),
            jax.ShapeDtypeStruct((8, n_tok), jnp.int32),
            jax.ShapeDtypeStruct((n_tok, LANES), F32),
            jax.ShapeDtypeStruct((TOP_K * N_EXPERTS, LANES), F32),
        ],
        compiler_params=_cparams(("arbitrary",)),
        name="outproj",
    )(attn, sa, gp, x, w_out, g, rw, rb)


def _rank_body(ti_ref, base_ref, tri_ref, dest_ref, carry_ref):
    i = pl.program_id(0)

    @pl.when(i == 0)
    def _():
        carry_ref[...] = jnp.zeros_like(carry_ref)

    tt = ti_ref.shape[1]
    e_iota = lax.broadcasted_iota(jnp.int32, (N_EXPERTS, tt), 0)
    hot = jnp.concatenate([(ti_ref[k:k + 1, :] == e_iota).astype(F32) for k in range(TOP_K)], axis=0)
    incl = jnp.dot(hot.astype(BF16), tri_ref[...], preferred_element_type=F32)
    slot = base_ref[:, 0:1] + carry_ref[:, 0:1] + incl - 1.0
    picked = hot * slot
    rows = [jnp.sum(picked[k * N_EXPERTS:(k + 1) * N_EXPERTS, :], axis=0, keepdims=True)
            for k in range(TOP_K)]
    rows.append(jnp.zeros((8 - TOP_K, tt), F32))
    dest_ref[...] = jnp.concatenate(rows, axis=0).astype(jnp.int32)
    carry_ref[...] += jnp.broadcast_to(jnp.sum(hot, axis=1, keepdims=True), carry_ref.shape)


def _rank(topi, base, tri):
    n_tok = topi.shape[1]
    tt = tri.shape[0]
    return pl.pallas_call(
        _rank_body,
        grid=(n_tok // tt,),
        in_specs=[
            pl.BlockSpec((8, tt), lambda i: (0, i)),
            pl.BlockSpec(base.shape, lambda i: (0, 0)),
            pl.BlockSpec((tt, tt), lambda i: (0, 0)),
        ],
        out_specs=pl.BlockSpec((8, tt), lambda i: (0, i)),
        out_shape=jax.ShapeDtypeStruct((8, n_tok), jnp.int32),
        scratch_shapes=[pltpu.VMEM((TOP_K * N_EXPERTS, LANES), F32)],
        compiler_params=_cparams(("arbitrary",)),
        name="rank",
    )(topi, base, tri)


DISPATCH_BUFS = 3


def _dispatch_body(zs_ref, zl_ref, tz_ref, dest_hbm, hp_hbm, hs_hbm, xs_hbm,
                   idx0, idx1, idx2, hb0, hb1, hb2, lsem, dsem, zsem,
                   *, tt, n_prompt_tiles, n_out_tiles):
    i = pl.program_id(0)
    n = pl.num_programs(0)
    idx = (idx0, idx1, idx2)
    hbuf = (hb0, hb1, hb2)
    nbuf = DISPATCH_BUFS
    tile_sub = tt * ROW_SUB

    def load_wait(s):
        pltpu.make_async_copy(dest_hbm.at[0], idx[s], lsem.at[s]).wait()
        pltpu.make_async_copy(hp_hbm.at[pl.ds(0, tile_sub)], hbuf[s], lsem.at[s]).wait()

    def load_start(tile, s):
        pltpu.make_async_copy(dest_hbm.at[tile], idx[s], lsem.at[s]).start()

        @pl.when(tile < n_prompt_tiles)
        def _():
            r0 = pl.multiple_of(tile * tile_sub, tile_sub)
            pltpu.make_async_copy(hp_hbm.at[pl.ds(r0, tile_sub)], hbuf[s], lsem.at[s]).start()

        @pl.when(tile >= n_prompt_tiles)
        def _():
            r0 = pl.multiple_of((tile - n_prompt_tiles) * tile_sub, tile_sub)
            pltpu.make_async_copy(hs_hbm.at[pl.ds(r0, tile_sub)], hbuf[s], lsem.at[s]).start()

    def scatter_start(s):
        for k in range(TOP_K):
            for r in range(tt):
                dst0 = pl.multiple_of(idx[s][k * tt + r], ROW_SUB)
                pltpu.make_async_copy(hbuf[s].at[pl.ds(r * ROW_SUB, ROW_SUB)], xs_hbm.at[pl.ds(dst0, ROW_SUB)],
                                      dsem.at[s]).start(priority=r % 2)

    def scatter_wait(s):
        for _ in range(TOP_K):
            pltpu.make_async_copy(hbuf[s], xs_hbm.at[pl.ds(0, tile_sub)], dsem.at[s]).wait()

    @pl.when(i == 0)
    def _():
        zero = hbuf[nbuf - 1]
        zero[...] = jnp.zeros_like(zero)

        def pad_pieces(e, act):
            length = zl_ref[e]
            for shift in range(tt.bit_length() - 1, -1, -1):
                piece = 1 << shift
                first = zs_ref[e] + jnp.bitwise_and(length, ~(2 * piece - 1))

                @pl.when(jnp.bitwise_and(length, piece) != 0)
                def _(piece=piece, first=first):
                    dst0 = pl.multiple_of(first * ROW_SUB, ROW_SUB)
                    act(pltpu.make_async_copy(zero.at[pl.ds(0, piece * ROW_SUB)],
                                              xs_hbm.at[pl.ds(dst0, piece * ROW_SUB)], zsem))

        def per_expert(e, c):
            pad_pieces(e, lambda cp: cp.start())
            pad_pieces(e, lambda cp: cp.wait())
            return c
        lax.fori_loop(0, N_EXPERTS, per_expert, 0)

        def tail_copy(j):
            r0 = pl.multiple_of(j * tile_sub, tile_sub)
            return pltpu.make_async_copy(zero, xs_hbm.at[pl.ds(r0, tile_sub)], zsem)

        def tail_one(j, c):
            tail_copy(j).start()
            tail_copy(j).wait()
            return c
        lax.fori_loop(tz_ref[0], n_out_tiles, tail_one, 0)

        load_start(0, 0)

        @pl.when(n > 1)
        def _():
            load_start(1, 1)

    for s in range(nbuf):
        @pl.when(i % nbuf == s)
        def _(s=s):
            prev = (s + nbuf - 1) % nbuf
            load_wait(s)
            scatter_start(s)

            @pl.when(i > 0)
            def _():
                scatter_wait(prev)

            @pl.when(i + 2 < n)
            def _():
                load_start(i + 2, prev)

            @pl.when(i == n - 1)
            def _():
                scatter_wait(s)


def _dispatch(zero_start, zero_len, tail_tile, dest_tiles, hp, hs, n_rows):
    n_tiles, width = dest_tiles.shape
    tt = width // TOP_K
    n_prompt_tiles = hp.shape[0] // (tt * ROW_SUB)
    any_spec = pl.BlockSpec(memory_space=pl.ANY)
    grid_spec = pltpu.PrefetchScalarGridSpec(
        num_scalar_prefetch=3,
        grid=(n_tiles,),
        in_specs=[any_spec, any_spec, any_spec],
        out_specs=any_spec,
        scratch_shapes=[pltpu.SMEM((width,), jnp.int32)] * DISPATCH_BUFS
        + [pltpu.VMEM((tt * ROW_SUB, LANES), F32)] * DISPATCH_BUFS
        + [
            pltpu.SemaphoreType.DMA((DISPATCH_BUFS,)),
            pltpu.SemaphoreType.DMA((DISPATCH_BUFS,)),
            pltpu.SemaphoreType.DMA,
        ],
    )
    return pl.pallas_call(
        functools.partial(_dispatch_body, tt=tt, n_prompt_tiles=n_prompt_tiles, n_out_tiles=n_rows // tt),
        grid_spec=grid_spec,
        out_shape=jax.ShapeDtypeStruct((n_rows * ROW_SUB, LANES), F32),
        compiler_params=_cparams(("arbitrary",)),
        name="dispatch",
    )(zero_start, zero_len, tail_tile, dest_tiles, hp, hs)


def _moe_body(te_ref, nu_ref, nx_ref, x_ref, g_ref, wg_hbm, wu_hbm, wd_hbm, b_ref,
              y_ref, wgs, wus, wds, wgb, wub, wdb, wsem, *, tm):
    i = pl.program_id(0)
    n_used = nu_ref[0]

    def weight_copies(e):
        return (pltpu.make_async_copy(wg_hbm.at[e], wgs, wsem),
                pltpu.make_async_copy(wu_hbm.at[e], wus, wsem),
                pltpu.make_async_copy(wd_hbm.at[e], wds, wsem))

    @pl.when(i >= n_used)
    def _():
        y_ref[...] = jnp.zeros_like(y_ref)

    @pl.when(i < n_used)
    def _():
        @pl.when(i == 0)
        def _():
            for cp in weight_copies(te_ref[0]):
                cp.start()

        prev = te_ref[jnp.maximum(i - 1, 0)]
        changed = jnp.logical_or(i == 0, te_ref[i] != prev)

        @pl.when(changed)
        def _():
            for cp in weight_copies(te_ref[i]):
                cp.wait()
            wgb[...] = wgs[...].astype(BF16)
            wub[...] = wus[...].astype(BF16)
            wdb[...] = wds[...].astype(BF16)

            @pl.when(nx_ref[i] >= 0)
            def _():
                for cp in weight_copies(nx_ref[i]):
                    cp.start()

        half = tm // MOE_SUBTILES
        d_model = ROW_SUB * LANES
        bias = b_ref[te_ref[i]]

        def normed_half(r0):
            cols = [_load_slab_cols(x_ref, half, c, r0) for c in range(ROW_SUB)]
            ssq = cols[0] * cols[0]
            for xc in cols[1:]:
                ssq = ssq + xc * xc
            r = lax.rsqrt(jnp.sum(ssq, axis=-1, keepdims=True) * (1.0 / d_model) + RMS_EPS)
            return jnp.concatenate(
                [(xc * r * g_ref[:, c * LANES:(c + 1) * LANES]).astype(BF16) for c, xc in enumerate(cols)], axis=1)

        def gate_up(xb):
            gt = jnp.dot(xb, wgb[...], preferred_element_type=F32) + bias[0:1, :]
            up = jnp.dot(xb, wub[...], preferred_element_type=F32) + bias[1:2, :]
            return gt, up

        def down(gt, up, r0):
            gt = jnp.minimum(gt, SWIGLU_LIMIT)
            up = jnp.clip(up, -SWIGLU_LIMIT, SWIGLU_LIMIT)
            hdn = (up + 1.0) * (gt * _sigmoid(SWIGLU_ALPHA * gt))
            y = jnp.dot(hdn.astype(BF16), wdb[...], preferred_element_type=F32) + bias[2:3, :]
            _store_slabs(y_ref, y, r0)

        pending = gate_up(normed_half(0))
        for j in range(MOE_SUBTILES):
            cur = pending
            if j + 1 < MOE_SUBTILES:
                pending = gate_up(normed_half((j + 1) * half))
            down(cur[0], cur[1], j * half)


def _moe(tile_expert, n_used, next_expert, x_rows, g, w_gate, b_gate, w_up, b_up, w_down, b_down, *, tm):
    n_tiles = x_rows.shape[0] // (tm * ROW_SUB)
    d_model = ROW_SUB * LANES
    d_ff = w_gate.shape[2]
    assert w_gate.shape[1] == d_model and d_ff == d_model
    biases = jnp.stack([b_gate, b_up, b_down], axis=1).astype(F32)
    any_spec = pl.BlockSpec(memory_space=pl.ANY)
    grid_spec = pltpu.PrefetchScalarGridSpec(
        num_scalar_prefetch=3,
        grid=(n_tiles,),
        in_specs=[
            pl.BlockSpec((tm * ROW_SUB, LANES), lambda i, te, nu, nx: (jnp.minimum(i, nu[0] - 1), 0)),
            pl.BlockSpec((1, d_model), lambda i, te, nu, nx: (0, 0)),
            any_spec, any_spec, any_spec,
            pl.BlockSpec(biases.shape, lambda i, te, nu, nx: (0, 0, 0)),
        ],
        out_specs=pl.BlockSpec((tm * ROW_SUB, LANES), lambda i, te, nu, nx: (i, 0)),
        scratch_shapes=[
            pltpu.VMEM((d_model, d_ff), F32),
            pltpu.VMEM((d_model, d_ff), F32),
            pltpu.VMEM((d_ff, d_model), F32),
            pltpu.VMEM((d_model, d_ff), BF16),
            pltpu.VMEM((d_model, d_ff), BF16),
            pltpu.VMEM((d_ff, d_model), BF16),
            pltpu.SemaphoreType.DMA,
        ],
    )
    return pl.pallas_call(
        functools.partial(_moe_body, tm=tm),
        grid_spec=grid_spec,
        out_shape=jax.ShapeDtypeStruct((n_tiles * tm * ROW_SUB, LANES), F32),
        compiler_params=_cparams(("arbitrary",)),
        name="moe",
    )(tile_expert, n_used, next_expert, x_rows, g, w_gate, w_up, w_down, biases)


def _combine_body(dest_hbm, y_hbm, h_ref, gt_ref, o_ref, idx0, idx1, yb0, yb1, isem, gsem, *, tt):
    i = pl.program_id(0)
    n = pl.num_programs(0)
    idx = (idx0, idx1)
    ybuf = (yb0, yb1)

    def idx_copy(tile, s):
        return pltpu.make_async_copy(dest_hbm.at[tile], idx[s], isem.at[s])

    def issue_gather(s):
        for k in range(TOP_K):
            for r in range(tt):
                src = pl.multiple_of(idx[s][k * tt + r], ROW_SUB)
                pltpu.make_async_copy(y_hbm.at[pl.ds(src, ROW_SUB)],
                                      ybuf[s].at[k, pl.ds(r * ROW_SUB, ROW_SUB)], gsem.at[s]).start(priority=r % 2)

    def wait_gather(s):
        for k in range(TOP_K):
            pltpu.make_async_copy(y_hbm.at[pl.ds(0, tt * ROW_SUB)], ybuf[s].at[k], gsem.at[s]).wait()

    @pl.when(i == 0)
    def _():
        idx_copy(0, 0).start()
        idx_copy(0, 0).wait()
        issue_gather(0)

        @pl.when(n > 1)
        def _():
            idx_copy(1, 1).start()

    for s in range(2):
        @pl.when(i % 2 == s)
        def _(s=s):
            o = 1 - s

            @pl.when(i + 1 < n)
            def _():
                idx_copy(i + 1, o).wait()
                issue_gather(o)

            @pl.when(i + 2 < n)
            def _():
                idx_copy(i + 2, s).start()

            wait_gather(s)
            gts = gt_ref[...]
            for c in range(ROW_SUB):
                cs = slice(c * LANES, (c + 1) * LANES)
                acc = _load_slab_cols(h_ref, tt, c)
                for k in range(TOP_K):
                    gate = gts[:, k * GATE_LANES:k * GATE_LANES + 1]
                    acc = acc + gate * ybuf[s][k, pl.ds(c, tt, stride=ROW_SUB), :]
                o_ref[:, cs] = acc


def _combine(dest_tiles, y_rows, h, gates_tok):
    n_tiles, width = dest_tiles.shape
    tt = width // TOP_K
    n_tok, d_model = h.shape[0] // ROW_SUB, ROW_SUB * LANES
    any_spec = pl.BlockSpec(memory_space=pl.ANY)
    return pl.pallas_call(
        functools.partial(_combine_body, tt=tt),
        grid=(n_tiles,),
        in_specs=[any_spec, any_spec,
                  pl.BlockSpec((tt * ROW_SUB, LANES), lambda i: (i, 0)),
                  pl.BlockSpec((tt, LANES), lambda i: (i, 0))],
        out_specs=pl.BlockSpec((tt, d_model), lambda i: (i, 0)),
        out_shape=jax.ShapeDtypeStruct((n_tok, d_model), F32),
        scratch_shapes=[
            pltpu.SMEM((TOP_K * tt,), jnp.int32),
            pltpu.SMEM((TOP_K * tt,), jnp.int32),
            pltpu.VMEM((TOP_K, tt * ROW_SUB, LANES), F32),
            pltpu.VMEM((TOP_K, tt * ROW_SUB, LANES), F32),
            pltpu.SemaphoreType.DMA((2,)),
            pltpu.SemaphoreType.DMA((2,)),
        ],
        compiler_params=_cparams(("arbitrary",)),
        name="combine",
    )(dest_tiles, y_rows, h, gates_tok)


def _block_diag_ones():
    r = lax.broadcasted_iota(jnp.int32, (NORM_BLOCK, NORM_BLOCK), 0) // HEAD_DIM
    c = lax.broadcasted_iota(jnp.int32, (NORM_BLOCK, NORM_BLOCK), 1) // HEAD_DIM
    return (r == c).astype(BF16)


def _dest_tiles(dest, tt):
    n_tok = dest.shape[1]
    tiles = dest[:TOP_K].reshape(TOP_K, n_tok // tt, tt).transpose(1, 0, 2).reshape(n_tok // tt, TOP_K * tt)
    return tiles * ROW_SUB


def _layer(xp, xs, ck, cv, sp, lam, lam_init, ng, w_in, qng, kng, slg, pw, ps, w_out, fg,
           rw, rb, w_gate, b_gate, w_up, b_up, w_down, b_down):
    bp, t_len, d_model = xp.shape
    bs, ts, _ = xs.shape
    past = ck.shape[1]
    pool_w = sp.shape[-1]
    n_p, n_s = bp * t_len, bs * ts

    w_in_b = w_in.astype(BF16)
    w_out_b = w_out.astype(BF16)
    pw_b = pw.astype(BF16)
    reps = NORM_BLOCK // HEAD_DIM
    qg = (jnp.tile(qng.astype(F32), reps) * (HEAD_DIM ** -0.5)).reshape(1, NORM_BLOCK)
    kg = jnp.tile(kng.astype(F32), reps).reshape(1, NORM_BLOCK)
    bd = _block_diag_ones()
    ng2 = ng.astype(F32).reshape(1, d_model)
    ps2 = ps.astype(F32).reshape(1, d_model)
    sg = (slg.astype(F32) * (1.0 - lam_init)).reshape(1, V_DIM)
    slopes = jnp.exp2(-(8.0 / N_HEADS) * jnp.arange(1, N_HEADS + 1, dtype=F32))
    lam1 = lam.reshape(1).astype(F32)
    fg2 = fg.astype(F32).reshape(1, d_model)
    rw_b = jnp.zeros((d_model, LANES), BF16).at[:, :N_EXPERTS].set(rw.astype(BF16))
    rb2 = jnp.zeros((1, LANES), F32).at[0, :N_EXPERTS].set(rb.astype(F32))

    tm = min(ROW_TILE, t_len)
    zero_pre = jnp.zeros((bp, POOL_HALO, pool_w), F32)
    qp, kp, kpb, vp, vpb, sap, gpp, utp = _inproj(
        xp, zero_pre, ng2, w_in_b, qg, kg, bd, pw_b, ps2, nseg=1, seg_len=tm, start_pos=0, carry=True)
    atp = _attn_prompt(qp, kpb, vpb, slopes, lam1, sg)
    hp, tip, tgp, cntp = _outproj(atp.reshape(n_p, d_model), sap.reshape(n_p, d_model),
                                  gpp.reshape(n_p, d_model), xp.reshape(n_p, d_model),
                                  w_out_b, fg2, rw_b, rb2)

    pre_s = jnp.concatenate([jnp.zeros((bs, POOL_HALO - sp.shape[1], pool_w), F32), sp.astype(F32)], axis=1)
    qs, ks, ksb, vs, vsb, sas, gps, uts = _inproj(
        xs.reshape(1, n_s, d_model), pre_s, ng2, w_in_b, qg, kg, bd, pw_b, ps2,
        nseg=bs, seg_len=ts, start_pos=past, carry=False)
    ats = _attn_sample(qs.reshape(bs, ts, d_model), ksb.reshape(bs, ts, d_model), vsb.reshape(bs, ts, d_model),
                       ck.reshape(bs, past * N_HEADS, V_DIM), cv.reshape(bs, past * N_HEADS, V_DIM),
                       slopes, lam1, sg)
    hs, tis, tgs, cnts = _outproj(ats.reshape(n_s, d_model), sas.reshape(n_s, d_model),
                                  gps.reshape(n_s, d_model), xs.reshape(n_s, d_model),
                                  w_out_b, fg2, rw_b, rb2)

    n_tok = n_p + n_s
    topi = jnp.concatenate([tip, tis], axis=1)
    cnt = (cntp[:, 0] + cnts[:, 0]).astype(jnp.int32).reshape(TOP_K, N_EXPERTS)
    per_expert = jnp.sum(cnt, axis=0)
    padded = (per_expert + MOE_TILE - 1) // MOE_TILE * MOE_TILE
    pad_end = jnp.cumsum(padded)
    pad_start = pad_end - padded
    base = pad_start[None, :] + jnp.cumsum(cnt, axis=0) - cnt
    base_f = jnp.broadcast_to(base.reshape(-1, 1).astype(F32), (TOP_K * N_EXPERTS, LANES))
    tt = TOK_TILE
    rt = max(m * tt for m in range(1, RANK_TILE_MAX // tt + 1) if n_tok % (m * tt) == 0)
    tri = (lax.broadcasted_iota(jnp.int32, (rt, rt), 0) <= lax.broadcasted_iota(jnp.int32, (rt, rt), 1)).astype(BF16)
    dest = _rank(topi, base_f, tri)

    n_tiles = -(-(n_tok * TOP_K) // MOE_TILE) + N_EXPERTS
    n_used = (pad_end[-1] // MOE_TILE).astype(jnp.int32)
    tile_start = jnp.arange(n_tiles, dtype=jnp.int32) * MOE_TILE
    last_start = jnp.maximum(pad_end[-1] - MOE_TILE, 0)
    tile_expert = jnp.minimum(
        jnp.sum((jnp.minimum(tile_start, last_start)[:, None] >= pad_end[None, :]).astype(jnp.int32), axis=1),
        N_EXPERTS - 1)
    dest_p = _dest_tiles(dest[:, :n_p], tt)
    dest_s = _dest_tiles(dest[:, n_p:], tt)
    dest_all = jnp.concatenate([dest_p, dest_s], axis=0)
    assert MOE_TILE <= 2 * tt
    x_rows = _dispatch(pad_start + per_expert, padded - per_expert, (pad_end[-1:] // tt).astype(jnp.int32),
                       dest_all, hp, hs, n_tiles * MOE_TILE)
    e_ids = jnp.arange(N_EXPERTS, dtype=jnp.int32)
    later_present = jnp.logical_and(e_ids[None, :] > e_ids[:, None], (padded > 0)[None, :])
    next_of = jnp.min(jnp.where(later_present, e_ids[None, :], N_EXPERTS), axis=1)
    next_of = jnp.where(next_of == N_EXPERTS, -1, next_of).astype(jnp.int32)
    next_expert = jnp.sum(jnp.where(tile_expert[:, None] == e_ids[None, :], next_of[None, :], 0), axis=1)
    y_rows = _moe(tile_expert, n_used.reshape(1), next_expert.astype(jnp.int32), x_rows, fg2,
                  w_gate, b_gate, w_up, b_up, w_down, b_down, tm=MOE_TILE)

    yp = _combine(dest_p, y_rows, hp, tgp)
    ys = _combine(dest_s, y_rows, hs, tgs)

    heads = (N_HEADS, V_DIM)
    return (yp.reshape(bp, t_len, d_model), ys.reshape(bs, ts, d_model),
            kp.reshape(bp, t_len, *heads), vp.reshape(bp, t_len, *heads), utp[:, 1:],
            ks.reshape(bs, ts, *heads), vs.reshape(bs, ts, *heads), uts[:, 1:])


def kernel(x_prompt, x_sample, cache_k, cache_v, state_pool, norm_mix_g, w_in, q_norm_g, k_norm_g,
           lambda_q1, lambda_k1, lambda_q2, lambda_k2, subln_g, pool_w, pool_scale, w_out, norm_ffn_g,
           router_w, router_b, w_gate, b_gate, w_up, b_up, w_down, b_down):
    depth = w_in.shape[0]
    hp, hs = x_prompt, x_sample
    outs = [[] for _ in range(6)]
    for layer in range(depth):
        lam_init = 0.8 - 0.6 * math.exp(-0.3 * layer)
        lam = (jnp.exp(jnp.sum(lambda_q1[layer].astype(F32) * lambda_k1[layer].astype(F32)))
               - jnp.exp(jnp.sum(lambda_q2[layer].astype(F32) * lambda_k2[layer].astype(F32)))
               + lam_init)
        hp, hs, kp, vp, up, ks, vs, us = _layer(
            hp, hs, cache_k[layer], cache_v[layer], state_pool[layer], lam, lam_init,
            norm_mix_g[layer], w_in[layer], q_norm_g[layer], k_norm_g[layer], subln_g[layer],
            pool_w[layer], pool_scale[layer], w_out[layer], norm_ffn_g[layer],
            router_w[layer], router_b[layer], w_gate[layer], b_gate[layer], w_up[layer], b_up[layer],
            w_down[layer], b_down[layer])
        for lst, val in zip(outs, (kp, vp, up, ks, vs, us)):
            lst.append(val)
    return (hp, hs) + tuple(jnp.stack(o) for o in outs)
```

```python
import functools
import math

import jax
import jax.numpy as jnp
from jax import lax
from jax.experimental import pallas as pl
from jax.experimental.pallas import tpu as pltpu

F32 = jnp.float32
BF16 = jnp.bfloat16

CHUNK = 64
N_HEADS = 8
HEAD_DIM = 64
V_DIM = 2 * HEAD_DIM
POOL_WINDOWS = (2, 4, 8, 16)
POOL_GROUP_DIM = 128
POOL_OUT_DIM = 256
POOL_HALO = 16
N_EXPERTS = 32
TOP_K = 4
SWIGLU_LIMIT = 7.0
SWIGLU_ALPHA = 1.702
RMS_EPS = 1e-6
NEG_INF = -1e30

LANES = 128
ROW_SUB = 8
NORM_BLOCK = 256
VMEM_LIMIT = 56 * 1024 * 1024

ROW_TILE = 512
ATTN_TILE = 256
KEY_CHUNK = 512
MOE_TILE = 512
MOE_SUBTILES = 2
GATE_LANES = LANES // TOP_K
TOK_TILE = 256
RANK_TILE_MAX = 1024


def _sigmoid(x):
    return 1.0 / (1.0 + jnp.exp(-x))


def _cparams(sem):
    return pltpu.CompilerParams(dimension_semantics=sem, vmem_limit_bytes=VMEM_LIMIT)


def _store_slabs(ref, val, row0=0):
    rows = val.shape[0]
    for c in range(ROW_SUB):
        ref[pl.ds(row0 * ROW_SUB + c, rows, stride=ROW_SUB), :] = val[:, c * LANES:(c + 1) * LANES]


def _load_slab_cols(ref, rows, c, row0=0):
    return ref[pl.ds(row0 * ROW_SUB + c, rows, stride=ROW_SUB), :]


def _inproj_body(x_ref, pre_ref, ng_ref, w_ref, qg_ref, kg_ref, bd_ref, pw_ref, ps_ref,
                 q_ref, k_ref, kb_ref, v_ref, vb_ref, sa_ref, gp_ref, ut_ref, ext_ref,
                 *, nseg, seg_len, start_pos, carry, attn_w, pool_w):
    rows = nseg * seg_len
    t = pl.program_id(1)
    x = x_ref[0]
    ms = jnp.mean(x * x, axis=-1, keepdims=True)
    xn = (x * lax.rsqrt(ms + RMS_EPS) * ng_ref[...]).astype(BF16)

    def proj(c0, width):
        return jnp.dot(xn, w_ref[:, c0:c0 + width], preferred_element_type=F32)

    bd = bd_ref[...]

    def group_norm(p, g_ref):
        ss = jnp.dot((p * p).astype(BF16), bd, preferred_element_type=F32)
        return p * lax.rsqrt(ss * (1.0 / HEAD_DIM) + RMS_EPS) * g_ref[...]

    nb = NORM_BLOCK
    heads_per_block = nb // V_DIM

    def store_heads(ref, val, c):
        for j in range(heads_per_block):
            head = c * heads_per_block + j
            ref[0, pl.ds(head, rows, stride=N_HEADS), :] = val[:, j * V_DIM:(j + 1) * V_DIM]

    def sink_q(val, c):
        q_ref[0, :, c * nb:(c + 1) * nb] = group_norm(val, qg_ref).astype(BF16)

    def sink_k(val, c):
        kn = group_norm(val, kg_ref)
        store_heads(k_ref, kn, c)
        kb_ref[0, :, c * nb:(c + 1) * nb] = kn.astype(BF16)

    def sink_v(val, c):
        store_heads(v_ref, val, c)
        vb_ref[0, :, c * nb:(c + 1) * nb] = val.astype(BF16)

    n_chunks = attn_w // nb
    work = [(part * attn_w + c * nb, sink, c)
            for part, sink in enumerate((sink_q, sink_k, sink_v)) for c in range(n_chunks)]
    pending = proj(work[0][0], nb)
    for j, (_, sink, c) in enumerate(work):
        cur = pending
        if j + 1 < len(work):
            pending = proj(work[j + 1][0], nb)
        sink(cur, c)

    if carry:
        @pl.when(t == 0)
        def _():
            ext_ref[:, 0:POOL_HALO, :] = pre_ref[...]
    else:
        ext_ref[:, 0:POOL_HALO, :] = pre_ref[...]
    for c in range(pool_w // nb):
        cs = slice(c * nb, (c + 1) * nb)
        u = proj(3 * attn_w + c * nb, nb)
        ext_ref[:, POOL_HALO:POOL_HALO + seg_len, cs] = u.reshape(nseg, seg_len, nb)

    ga0 = 3 * attn_w + pool_w
    d_model = attn_w
    for c in range(d_model // nb):
        cs = slice(c * nb, (c + 1) * nb)
        sa_ref[0, :, cs] = _sigmoid(proj(ga0 + c * nb, nb)).astype(BF16)

    gb0 = ga0 + d_model
    row = lax.broadcasted_iota(jnp.int32, (1, seg_len, 1), 1)
    pos = row + start_pos
    if carry:
        pos = pos + t * seg_len
    for g, w in enumerate(POOL_WINDOWS):
        cs = slice(g * POOL_GROUP_DIM, (g + 1) * POOL_GROUP_DIM)
        own = ext_ref[:, POOL_HALO:POOL_HALO + seg_len, cs]
        acc = own
        for i in range(1, w):
            acc = acc + ext_ref[:, POOL_HALO - i:POOL_HALO - i + seg_len, cs]
        inv = 1.0 / jnp.minimum(w, pos + 1).astype(F32)
        z = (acc * inv - own).reshape(rows, POOL_GROUP_DIM)
        os_ = slice(g * POOL_OUT_DIM, (g + 1) * POOL_OUT_DIM)
        yp = jnp.dot(z.astype(BF16), pw_ref[g], preferred_element_type=F32) * ps_ref[:, os_]
        gb = proj(gb0 + g * POOL_OUT_DIM, POOL_OUT_DIM)
        gp_ref[0, :, os_] = (_sigmoid(gb) * yp).astype(BF16)

    tail = ext_ref[:, seg_len:seg_len + POOL_HALO, :]
    ut_ref[...] = tail
    if carry:
        ext_ref[:, 0:POOL_HALO, :] = tail


def _inproj(x3, prefix, ng, w_in, qg, kg, bd, pw, ps, *, nseg, seg_len, start_pos, carry):
    groups, t_len, d_model = x3.shape
    rows = nseg * seg_len
    steps = t_len // rows
    in_cols = w_in.shape[1]
    pool_w = prefix.shape[-1]
    attn_w = d_model
    assert in_cols == 3 * attn_w + pool_w + 2 * d_model
    tok = lambda b, t: (b, t, 0)
    fixed2 = lambda b, t: (0, 0)
    act = lambda dt: jax.ShapeDtypeStruct((groups, t_len, d_model), dt)
    by_head = jax.ShapeDtypeStruct((groups, t_len * N_HEADS, V_DIM), F32)
    tok_spec = pl.BlockSpec((1, rows, d_model), tok)
    head_spec = pl.BlockSpec((1, rows * N_HEADS, V_DIM), tok)
    body = functools.partial(_inproj_body, nseg=nseg, seg_len=seg_len, start_pos=start_pos,
                             carry=carry, attn_w=attn_w, pool_w=pool_w)
    return pl.pallas_call(
        body,
        grid=(groups, steps),
        in_specs=[
            pl.BlockSpec((1, rows, d_model), tok),
            pl.BlockSpec((nseg, POOL_HALO, pool_w), lambda b, t: (b, 0, 0)),
            pl.BlockSpec((1, d_model), fixed2),
            pl.BlockSpec((d_model, in_cols), fixed2, pipeline_mode=pl.Buffered(1)),
            pl.BlockSpec((1, NORM_BLOCK), fixed2),
            pl.BlockSpec((1, NORM_BLOCK), fixed2),
            pl.BlockSpec((NORM_BLOCK, NORM_BLOCK), fixed2),
            pl.BlockSpec(pw.shape, lambda b, t: (0, 0, 0)),
            pl.BlockSpec((1, d_model), fixed2),
        ],
        out_specs=[tok_spec, head_spec, tok_spec, head_spec, tok_spec, tok_spec, tok_spec,
                   pl.BlockSpec((nseg, POOL_HALO, pool_w), lambda b, t: (b, 0, 0))],
        out_shape=[act(BF16), by_head, act(BF16), by_head, act(BF16), act(BF16), act(BF16),
                   jax.ShapeDtypeStruct(prefix.shape, F32)],
        scratch_shapes=[pltpu.VMEM((nseg, POOL_HALO + seg_len, pool_w), F32)],
        compiler_params=_cparams(("arbitrary", "arbitrary")),
        name="inproj",
    )(x3, prefix, ng, w_in, qg, kg, bd, pw, ps)


def _half_masks(q):
    lane = lax.broadcasted_iota(jnp.int32, q.shape, 1)
    zero = jnp.zeros_like(q)
    return jnp.where(lane < HEAD_DIM, q, zero), jnp.where(lane >= HEAD_DIM, q, zero)


def _qk(qz, kblk):
    return lax.dot_general(qz, kblk, (((1,), (1,)), ((), ())), preferred_element_type=F32)


def _subln(o, sg_ref):
    ms = jnp.mean(o * o, axis=-1, keepdims=True)
    return o * lax.rsqrt(ms + RMS_EPS) * sg_ref[...]


def _attn_body(slope_ref, lam_ref, q_ref, k_ref, v_ref, ka_ref, sg_ref, o_ref,
               kf_ref, vf_ref, s_ref, p_ref, *, tq, nq):
    h = pl.program_id(1)
    slope = slope_ref[h]
    lam = lam_ref[0]
    lane = lax.broadcasted_iota(jnp.int32, (tq, LANES), 1)
    kf_ref[:, 0:V_DIM] = k_ref[0]
    kf_ref[:, V_DIM:] = ka_ref[...]
    vf_ref[:, 0:V_DIM] = v_ref[0]
    t_len = vf_ref.shape[0]
    vf_ref[:, V_DIM:] = (lax.broadcasted_iota(jnp.int32, (t_len, LANES), 1) == 0).astype(BF16)

    row = lax.broadcasted_iota(jnp.int32, (tq, tq), 0)
    col = lax.broadcasted_iota(jnp.int32, (tq, tq), 1)
    rc = (row - col).astype(F32)
    vis = lax.shift_right_logical(col, 6) <= lax.shift_right_logical(row, 6)
    corr = jnp.where(vis, jnp.minimum(rc, 0.0) * (2.0 * slope), NEG_INF)

    def scores(qi, m):
        nk = (qi + 1) * tq
        qz = _half_masks(q_ref[0, qi * tq:(qi + 1) * tq, :])[m]
        t = lax.broadcasted_iota(jnp.int32, (tq, LANES), 0) + qi * tq
        t_hi = lax.shift_left(lax.shift_right_logical(t, 8), 8).astype(F32)
        t_lo = jnp.bitwise_and(t, 255).astype(F32)
        qaug = jnp.where(lane == 0, -slope * t_hi,
                         jnp.where(lane == 1, -slope * t_lo,
                                   jnp.where(lane < 4, slope, 0.0))).astype(BF16)
        qa = jnp.concatenate([qz, qaug], axis=1)
        for c0 in range(0, nk, KEY_CHUNK):
            c1 = min(c0 + KEY_CHUNK, nk)
            s_ref[qi % 2, m, :, c0:c1] = _qk(qa, kf_ref[c0:c1, :])

    def softmax_pv(qi, m):
        nk = (qi + 1) * tq
        b = qi % 2
        s_ref[b, m, :, nk - tq:nk] += corr
        chunks = [(c0, min(c0 + KEY_CHUNK, nk)) for c0 in range(0, nk, KEY_CHUNK)]
        mx = None
        for c0, c1 in chunks:
            part = jnp.max(s_ref[b, m, :, c0:c1], axis=-1, keepdims=True)
            mx = part if mx is None else jnp.maximum(mx, part)
        for c0, c1 in chunks:
            p_ref[m, :, c0:c1] = jnp.exp(s_ref[b, m, :, c0:c1] - mx).astype(BF16)
        return jnp.dot(p_ref[m, :, 0:nk], vf_ref[0:nk, :], preferred_element_type=F32)

    scores(0, 0)
    scores(0, 1)
    for qi in range(nq):
        if qi + 1 < nq:
            scores(qi + 1, 0)
            scores(qi + 1, 1)
        o1, o2 = softmax_pv(qi, 0), softmax_pv(qi, 1)
        c1 = 1.0 / o1[:, V_DIM:V_DIM + 1]
        c2 = lam / o2[:, V_DIM:V_DIM + 1]
        o = o1[:, 0:V_DIM] * c1 - o2[:, 0:V_DIM] * c2
        o_ref[0, qi * tq:(qi + 1) * tq, :] = _subln(o, sg_ref).astype(BF16)


def _attn_prompt(q, kb, vb, slopes, lam, sg):
    batch, t_len, width = q.shape
    tq = min(ATTN_TILE, t_len)
    nq = t_len // tq
    pos = jnp.arange(t_len, dtype=jnp.int32)
    ka = jnp.zeros((t_len, LANES), F32)
    ka = ka.at[:, 0:2].set(1.0).at[:, 2].set(((pos >> 8) << 8).astype(F32)).at[:, 3].set((pos & 255).astype(F32))
    smem = pl.BlockSpec(memory_space=pltpu.SMEM)
    seq = pl.BlockSpec((1, t_len, V_DIM), lambda b, h: (b, 0, h))
    return pl.pallas_call(
        functools.partial(_attn_body, tq=tq, nq=nq),
        grid=(batch, N_HEADS),
        in_specs=[
            smem, smem, seq, seq, seq,
            pl.BlockSpec((t_len, LANES), lambda b, h: (0, 0)),
            pl.BlockSpec((1, V_DIM), lambda b, h: (0, 0)),
        ],
        out_specs=seq,
        out_shape=jax.ShapeDtypeStruct((batch, t_len, width), BF16),
        scratch_shapes=[
            pltpu.VMEM((t_len, V_DIM + LANES), BF16),
            pltpu.VMEM((t_len, V_DIM + LANES), BF16),
            pltpu.VMEM((2, 2, tq, t_len), F32),
            pltpu.VMEM((2, tq, t_len), BF16),
        ],
        compiler_params=_cparams(("arbitrary", "arbitrary")),
        name="attn_prompt",
    )(slopes, lam, q, kb, vb, ka.astype(BF16), sg)


def _attn_dec_body(slope_ref, lam_ref, q_ref, kc_ref, vc_ref, kn_ref, vn_ref, sg_ref, o_ref, *, past):
    lam = lam_ref[0]
    tq = q_ref.shape[1]
    qpos_a = lax.broadcasted_iota(jnp.int32, (tq, past), 0) + past
    kpos_a = lax.broadcasted_iota(jnp.int32, (tq, past), 1)
    qpos_b = lax.broadcasted_iota(jnp.int32, (tq, tq), 0) + past
    kpos_b = lax.broadcasted_iota(jnp.int32, (tq, tq), 1) + past

    def dist_and_vis(qpos, kpos):
        vis = lax.shift_right_logical(kpos, 6) <= lax.shift_right_logical(qpos, 6)
        return jnp.abs(qpos - kpos).astype(F32), vis

    dist_a, vis_a = dist_and_vis(qpos_a, kpos_a)
    dist_b, vis_b = dist_and_vis(qpos_b, kpos_b)

    def scores(h):
        cs = slice(h * V_DIM, (h + 1) * V_DIM)
        q1z, q2z = _half_masks(q_ref[0, :, cs])
        kc = kc_ref[0, pl.ds(h, past, stride=N_HEADS), :].astype(BF16)
        qcat = jnp.concatenate([q1z, q2z], axis=0)
        return _qk(qcat, kc), _qk(qcat, kn_ref[0, :, cs])

    pending = scores(0)
    for h in range(N_HEADS):
        sa_both, sb_both = pending
        if h + 1 < N_HEADS:
            pending = scores(h + 1)
        cs = slice(h * V_DIM, (h + 1) * V_DIM)
        slope = slope_ref[h]
        bias_a = jnp.where(vis_a, -slope * dist_a, NEG_INF)
        bias_b = jnp.where(vis_b, -slope * dist_b, NEG_INF)
        vc = vc_ref[0, pl.ds(h, past, stride=N_HEADS), :].astype(BF16)

        def softmax_parts(m_idx, sa_both=sa_both, sb_both=sb_both, bias_a=bias_a, bias_b=bias_b):
            rs = slice(m_idx * tq, (m_idx + 1) * tq)
            sa = sa_both[rs, :] + bias_a
            sb = sb_both[rs, :] + bias_b
            m = jnp.maximum(jnp.max(sa, axis=-1, keepdims=True), jnp.max(sb, axis=-1, keepdims=True))
            pa = jnp.exp(sa - m)
            pb = jnp.exp(sb - m)
            l = jnp.sum(pa, axis=-1, keepdims=True) + jnp.sum(pb, axis=-1, keepdims=True)
            return pa, pb, l

        pa1, pb1, l1 = softmax_parts(0)
        pa2, pb2, l2 = softmax_parts(1)
        c1 = 1.0 / l1
        c2 = lam / l2
        wa = (pa1 * c1 - pa2 * c2).astype(BF16)
        wb = (pb1 * c1 - pb2 * c2).astype(BF16)
        o = (jnp.dot(wa, vc, preferred_element_type=F32)
             + jnp.dot(wb, vn_ref[0, :, cs], preferred_element_type=F32))
        o_ref[0, :, cs] = _subln(o, sg_ref).astype(BF16)


def _attn_sample(q, kn, vn, cache_k, cache_v, slopes, lam, sg):
    batch, tq, width = q.shape
    past = cache_k.shape[1] // N_HEADS
    smem = pl.BlockSpec(memory_space=pltpu.SMEM)
    new = pl.BlockSpec((1, tq, width), lambda b: (b, 0, 0))
    old = pl.BlockSpec((1, past * N_HEADS, V_DIM), lambda b: (b, 0, 0))
    return pl.pallas_call(
        functools.partial(_attn_dec_body, past=past),
        grid=(batch,),
        in_specs=[smem, smem, new, old, old, new, new,
                  pl.BlockSpec((1, V_DIM), lambda b: (0, 0))],
        out_specs=new,
        out_shape=jax.ShapeDtypeStruct((batch, tq, width), BF16),
        compiler_params=_cparams(("arbitrary",)),
        name="attn_sample",
    )(slopes, lam, q, cache_k, cache_v, kn, vn, sg)


def _outproj_body(at_ref, sa_ref, gp_ref, x_ref, wo_ref, g_ref, rw_ref, rb_ref,
                  h_ref, ti_ref, tg_ref, cnt_ref):
    i = pl.program_id(0)
    rows = x_ref.shape[0]
    n_sub = 2 if rows % (2 * LANES) == 0 else 1
    sub = rows // n_sub

    def project(j):
        rs = slice(j * sub, (j + 1) * sub)
        merged = (sa_ref[rs, :].astype(F32) * at_ref[rs, :].astype(F32) + gp_ref[rs, :].astype(F32)).astype(BF16)
        return x_ref[rs, :] + jnp.dot(merged, wo_ref[...], preferred_element_type=F32)

    def route(hh, j):
        _store_slabs(h_ref, hh, j * sub)
        ms = jnp.mean(hh * hh, axis=-1, keepdims=True)
        hn = (hh * lax.rsqrt(ms + RMS_EPS) * g_ref[...]).astype(BF16)
        logits = jnp.dot(hn, rw_ref[...], preferred_element_type=F32) + rb_ref[...]
        lt = jnp.transpose(logits)[0:N_EXPERTS, :]
        e_iota = lax.broadcasted_iota(jnp.int32, (N_EXPERTS, sub), 0)
        vals, idxs, hots = [], [], []
        cur = lt
        for _ in range(TOP_K):
            m = jnp.max(cur, axis=0, keepdims=True)
            idx = jnp.min(jnp.where(cur == m, e_iota, N_EXPERTS), axis=0, keepdims=True)
            hit = e_iota == idx
            vals.append(m)
            idxs.append(idx)
            hots.append(hit)
            cur = jnp.where(hit, -jnp.inf, cur)
        ex = [jnp.exp(v - vals[0]) for v in vals]
        inv = 1.0 / (ex[0] + ex[1] + ex[2] + ex[3])
        zi = jnp.zeros((8 - TOP_K, sub), jnp.int32)
        cs = slice(j * sub, (j + 1) * sub)
        ti_ref[:, cs] = jnp.concatenate(idxs + [zi], axis=0)
        wide = jnp.concatenate([jnp.broadcast_to(e * inv, (GATE_LANES, sub)) for e in ex], axis=0)
        tg_ref[cs, :] = jnp.transpose(wide)
        hot = jnp.concatenate([hh_.astype(F32) for hh_ in hots], axis=0)
        return jnp.sum(hot, axis=1, keepdims=True)

    hs = [project(j) for j in range(n_sub)]
    csum = route(hs[0], 0)
    for j in range(1, n_sub):
        csum = csum + route(hs[j], j)

    @pl.when(i == 0)
    def _():
        cnt_ref[...] = jnp.zeros_like(cnt_ref)

    cnt_ref[...] += jnp.broadcast_to(csum, cnt_ref.shape)


def _outproj(attn, sa, gp, x, w_out, g, rw, rb):
    n_tok, d_model = x.shape
    tm = min(2 * ROW_TILE, n_tok)
    row = lambda i: (i, 0)
    fixed = lambda i: (0, 0)
    colb = lambda i: (0, i)
    return pl.pallas_call(
        _outproj_body,
        grid=(n_tok // tm,),
        in_specs=[
            pl.BlockSpec((tm, d_model), row),
            pl.BlockSpec((tm, d_model), row),
            pl.BlockSpec((tm, d_model), row),
            pl.BlockSpec((tm, d_model), row),
            pl.BlockSpec((d_model, d_model), fixed),
            pl.BlockSpec((1, d_model), fixed),
            pl.BlockSpec((d_model, LANES), fixed),
            pl.BlockSpec((1, LANES), fixed),
        ],
        out_specs=[
            pl.BlockSpec((tm * ROW_SUB, LANES), row),
            pl.BlockSpec((8, tm), colb),
            pl.BlockSpec((tm, LANES), row),
            pl.BlockSpec((TOP_K * N_EXPERTS, LANES), fixed),
        ],
        out_shape=[
            jax.ShapeDtypeStruct((n_tok * ROW_SUB, LANES), F32),
            jax.ShapeDtypeStruct((8, n_tok), jnp.int32),
            jax.ShapeDtypeStruct((n_tok, LANES), F32),
            jax.ShapeDtypeStruct((TOP_K * N_EXPERTS, LANES), F32),
        ],
        compiler_params=_cparams(("arbitrary",)),
        name="outproj",
    )(attn, sa, gp, x, w_out, g, rw, rb)


def _rank_body(ti_ref, base_ref, tri_ref, dest_ref, carry_ref):
    i = pl.program_id(0)

    @pl.when(i == 0)
    def _():
        carry_ref[...] = jnp.zeros_like(carry_ref)

    tt = ti_ref.shape[1]
    e_iota = lax.broadcasted_iota(jnp.int32, (N_EXPERTS, tt), 0)
    hot = jnp.concatenate([(ti_ref[k:k + 1, :] == e_iota).astype(F32) for k in range(TOP_K)], axis=0)
    incl = jnp.dot(hot.astype(BF16), tri_ref[...], preferred_element_type=F32)
    slot = base_ref[:, 0:1] + carry_ref[:, 0:1] + incl - 1.0
    picked = hot * slot
    rows = [jnp.sum(picked[k * N_EXPERTS:(k + 1) * N_EXPERTS, :], axis=0, keepdims=True)
            for k in range(TOP_K)]
    rows.append(jnp.zeros((8 - TOP_K, tt), F32))
    dest_ref[...] = jnp.concatenate(rows, axis=0).astype(jnp.int32)
    carry_ref[...] += jnp.broadcast_to(jnp.sum(hot, axis=1, keepdims=True), carry_ref.shape)


def _rank(topi, base, tri):
    n_tok = topi.shape[1]
    tt = tri.shape[0]
    return pl.pallas_call(
        _rank_body,
        grid=(n_tok // tt,),
        in_specs=[
            pl.BlockSpec((8, tt), lambda i: (0, i)),
            pl.BlockSpec(base.shape, lambda i: (0, 0)),
            pl.BlockSpec((tt, tt), lambda i: (0, 0)),
        ],
        out_specs=pl.BlockSpec((8, tt), lambda i: (0, i)),
        out_shape=jax.ShapeDtypeStruct((8, n_tok), jnp.int32),
        scratch_shapes=[pltpu.VMEM((TOP_K * N_EXPERTS, LANES), F32)],
        compiler_params=_cparams(("arbitrary",)),
        name="rank",
    )(topi, base, tri)


DISPATCH_BUFS = 3


def _dispatch_body(zs_ref, zl_ref, tz_ref, dest_hbm, hp_hbm, hs_hbm, xs_hbm,
                   idx0, idx1, idx2, hb0, hb1, hb2, lsem, dsem, zsem,
                   *, tt, n_prompt_tiles, n_out_tiles):
    i = pl.program_id(0)
    n = pl.num_programs(0)
    idx = (idx0, idx1, idx2)
    hbuf = (hb0, hb1, hb2)
    nbuf = DISPATCH_BUFS
    tile_sub = tt * ROW_SUB

    def load_wait(s):
        pltpu.make_async_copy(dest_hbm.at[0], idx[s], lsem.at[s]).wait()
        pltpu.make_async_copy(hp_hbm.at[pl.ds(0, tile_sub)], hbuf[s], lsem.at[s]).wait()

    def load_start(tile, s):
        pltpu.make_async_copy(dest_hbm.at[tile], idx[s], lsem.at[s]).start()

        @pl.when(tile < n_prompt_tiles)
        def _():
            r0 = pl.multiple_of(tile * tile_sub, tile_sub)
            pltpu.make_async_copy(hp_hbm.at[pl.ds(r0, tile_sub)], hbuf[s], lsem.at[s]).start()

        @pl.when(tile >= n_prompt_tiles)
        def _():
            r0 = pl.multiple_of((tile - n_prompt_tiles) * tile_sub, tile_sub)
            pltpu.make_async_copy(hs_hbm.at[pl.ds(r0, tile_sub)], hbuf[s], lsem.at[s]).start()

    def scatter_start(s):
        for k in range(TOP_K):
            for r in range(tt):
                dst0 = pl.multiple_of(idx[s][k * tt + r], ROW_SUB)
                pltpu.make_async_copy(hbuf[s].at[pl.ds(r * ROW_SUB, ROW_SUB)], xs_hbm.at[pl.ds(dst0, ROW_SUB)],
                                      dsem.at[s]).start(priority=r % 2)

    def scatter_wait(s):
        for _ in range(TOP_K):
            pltpu.make_async_copy(hbuf[s], xs_hbm.at[pl.ds(0, tile_sub)], dsem.at[s]).wait()

    @pl.when(i == 0)
    def _():
        zero = hbuf[nbuf - 1]
        zero[...] = jnp.zeros_like(zero)

        def pad_pieces(e, act):
            length = zl_ref[e]
            for shift in range(tt.bit_length() - 1, -1, -1):
                piece = 1 << shift
                first = zs_ref[e] + jnp.bitwise_and(length, ~(2 * piece - 1))

                @pl.when(jnp.bitwise_and(length, piece) != 0)
                def _(piece=piece, first=first):
                    dst0 = pl.multiple_of(first * ROW_SUB, ROW_SUB)
                    act(pltpu.make_async_copy(zero.at[pl.ds(0, piece * ROW_SUB)],
                                              xs_hbm.at[pl.ds(dst0, piece * ROW_SUB)], zsem))

        def per_expert(e, c):
            pad_pieces(e, lambda cp: cp.start())
            pad_pieces(e, lambda cp: cp.wait())
            return c
        lax.fori_loop(0, N_EXPERTS, per_expert, 0)

        def tail_copy(j):
            r0 = pl.multiple_of(j * tile_sub, tile_sub)
            return pltpu.make_async_copy(zero, xs_hbm.at[pl.ds(r0, tile_sub)], zsem)

        def tail_one(j, c):
            tail_copy(j).start()
            tail_copy(j).wait()
            return c
        lax.fori_loop(tz_ref[0], n_out_tiles, tail_one, 0)

        load_start(0, 0)

        @pl.when(n > 1)
        def _():
            load_start(1, 1)

    for s in range(nbuf):
        @pl.when(i % nbuf == s)
        def _(s=s):
            prev = (s + nbuf - 1) % nbuf
            load_wait(s)
            scatter_start(s)

            @pl.when(i > 0)
            def _():
                scatter_wait(prev)

            @pl.when(i + 2 < n)
            def _():
                load_start(i + 2, prev)

            @pl.when(i == n - 1)
            def _():
                scatter_wait(s)


def _dispatch(zero_start, zero_len, tail_tile, dest_tiles, hp, hs, n_rows):
    n_tiles, width = dest_tiles.shape
    tt = width // TOP_K
    n_prompt_tiles = hp.shape[0] // (tt * ROW_SUB)
    any_spec = pl.BlockSpec(memory_space=pl.ANY)
    grid_spec = pltpu.PrefetchScalarGridSpec(
        num_scalar_prefetch=3,
        grid=(n_tiles,),
        in_specs=[any_spec, any_spec, any_spec],
        out_specs=any_spec,
        scratch_shapes=[pltpu.SMEM((width,), jnp.int32)] * DISPATCH_BUFS
        + [pltpu.VMEM((tt * ROW_SUB, LANES), F32)] * DISPATCH_BUFS
        + [
            pltpu.SemaphoreType.DMA((DISPATCH_BUFS,)),
            pltpu.SemaphoreType.DMA((DISPATCH_BUFS,)),
            pltpu.SemaphoreType.DMA,
        ],
    )
    return pl.pallas_call(
        functools.partial(_dispatch_body, tt=tt, n_prompt_tiles=n_prompt_tiles, n_out_tiles=n_rows // tt),
        grid_spec=grid_spec,
        out_shape=jax.ShapeDtypeStruct((n_rows * ROW_SUB, LANES), F32),
        compiler_params=_cparams(("arbitrary",)),
        name="dispatch",
    )(zero_start, zero_len, tail_tile, dest_tiles, hp, hs)


def _moe_body(te_ref, nu_ref, nx_ref, x_ref, g_ref, wg_hbm, wu_hbm, wd_hbm, b_ref,
              y_ref, wgs, wus, wds, wgb, wub, wdb, wsem, *, tm):
    i = pl.program_id(0)
    n_used = nu_ref[0]

    def weight_copies(e):
        return (pltpu.make_async_copy(wg_hbm.at[e], wgs, wsem),
                pltpu.make_async_copy(wu_hbm.at[e], wus, wsem),
                pltpu.make_async_copy(wd_hbm.at[e], wds, wsem))

    @pl.when(i >= n_used)
    def _():
        y_ref[...] = jnp.zeros_like(y_ref)

    @pl.when(i < n_used)
    def _():
        @pl.when(i == 0)
        def _():
            for cp in weight_copies(te_ref[0]):
                cp.start()

        prev = te_ref[jnp.maximum(i - 1, 0)]
        changed = jnp.logical_or(i == 0, te_ref[i] != prev)

        @pl.when(changed)
        def _():
            for cp in weight_copies(te_ref[i]):
                cp.wait()
            wgb[...] = wgs[...].astype(BF16)
            wub[...] = wus[...].astype(BF16)
            wdb[...] = wds[...].astype(BF16)

            @pl.when(nx_ref[i] >= 0)
            def _():
                for cp in weight_copies(nx_ref[i]):
                    cp.start()

        half = tm // MOE_SUBTILES
        d_model = ROW_SUB * LANES
        bias = b_ref[te_ref[i]]

        def normed_half(r0):
            cols = [_load_slab_cols(x_ref, half, c, r0) for c in range(ROW_SUB)]
            ssq = cols[0] * cols[0]
            for xc in cols[1:]:
                ssq = ssq + xc * xc
            r = lax.rsqrt(jnp.sum(ssq, axis=-1, keepdims=True) * (1.0 / d_model) + RMS_EPS)
            return jnp.concatenate(
                [(xc * r * g_ref[:, c * LANES:(c + 1) * LANES]).astype(BF16) for c, xc in enumerate(cols)], axis=1)

        def gate_up(xb):
            gt = jnp.dot(xb, wgb[...], preferred_element_type=F32) + bias[0:1, :]
            up = jnp.dot(xb, wub[...], preferred_element_type=F32) + bias[1:2, :]
            return gt, up

        def down(gt, up, r0):
            gt = jnp.minimum(gt, SWIGLU_LIMIT)
            up = jnp.clip(up, -SWIGLU_LIMIT, SWIGLU_LIMIT)
            hdn = (up + 1.0) * (gt * _sigmoid(SWIGLU_ALPHA * gt))
            y = jnp.dot(hdn.astype(BF16), wdb[...], preferred_element_type=F32) + bias[2:3, :]
            _store_slabs(y_ref, y, r0)

        pending = gate_up(normed_half(0))
        for j in range(MOE_SUBTILES):
            cur = pending
            if j + 1 < MOE_SUBTILES:
                pending = gate_up(normed_half((j + 1) * half))
            down(cur[0], cur[1], j * half)


def _moe(tile_expert, n_used, next_expert, x_rows, g, w_gate, b_gate, w_up, b_up, w_down, b_down, *, tm):
    n_tiles = x_rows.shape[0] // (tm * ROW_SUB)
    d_model = ROW_SUB * LANES
    d_ff = w_gate.shape[2]
    assert w_gate.shape[1] == d_model and d_ff == d_model
    biases = jnp.stack([b_gate, b_up, b_down], axis=1).astype(F32)
    any_spec = pl.BlockSpec(memory_space=pl.ANY)
    grid_spec = pltpu.PrefetchScalarGridSpec(
        num_scalar_prefetch=3,
        grid=(n_tiles,),
        in_specs=[
            pl.BlockSpec((tm * ROW_SUB, LANES), lambda i, te, nu, nx: (jnp.minimum(i, nu[0] - 1), 0)),
            pl.BlockSpec((1, d_model), lambda i, te, nu, nx: (0, 0)),
            any_spec, any_spec, any_spec,
            pl.BlockSpec(biases.shape, lambda i, te, nu, nx: (0, 0, 0)),
        ],
        out_specs=pl.BlockSpec((tm * ROW_SUB, LANES), lambda i, te, nu, nx: (i, 0)),
        scratch_shapes=[
            pltpu.VMEM((d_model, d_ff), F32),
            pltpu.VMEM((d_model, d_ff), F32),
            pltpu.VMEM((d_ff, d_model), F32),
            pltpu.VMEM((d_model, d_ff), BF16),
            pltpu.VMEM((d_model, d_ff), BF16),
            pltpu.VMEM((d_ff, d_model), BF16),
            pltpu.SemaphoreType.DMA,
        ],
    )
    return pl.pallas_call(
        functools.partial(_moe_body, tm=tm),
        grid_spec=grid_spec,
        out_shape=jax.ShapeDtypeStruct((n_tiles * tm * ROW_SUB, LANES), F32),
        compiler_params=_cparams(("arbitrary",)),
        name="moe",
    )(tile_expert, n_used, next_expert, x_rows, g, w_gate, w_up, w_down, biases)


def _combine_body(dest_hbm, y_hbm, h_ref, gt_ref, o_ref, idx0, idx1, yb0, yb1, isem, gsem, *, tt):
    i = pl.program_id(0)
    n = pl.num_programs(0)
    idx = (idx0, idx1)
    ybuf = (yb0, yb1)

    def idx_copy(tile, s):
        return pltpu.make_async_copy(dest_hbm.at[tile], idx[s], isem.at[s])

    def issue_gather(s):
        for k in range(TOP_K):
            for r in range(tt):
                src = pl.multiple_of(idx[s][k * tt + r], ROW_SUB)
                pltpu.make_async_copy(y_hbm.at[pl.ds(src, ROW_SUB)],
                                      ybuf[s].at[k, pl.ds(r * ROW_SUB, ROW_SUB)], gsem.at[s]).start(priority=r % 2)

    def wait_gather(s):
        for k in range(TOP_K):
            pltpu.make_async_copy(y_hbm.at[pl.ds(0, tt * ROW_SUB)], ybuf[s].at[k], gsem.at[s]).wait()

    @pl.when(i == 0)
    def _():
        idx_copy(0, 0).start()
        idx_copy(0, 0).wait()
        issue_gather(0)

        @pl.when(n > 1)
        def _():
            idx_copy(1, 1).start()

    for s in range(2):
        @pl.when(i % 2 == s)
        def _(s=s):
            o = 1 - s

            @pl.when(i + 1 < n)
            def _():
                idx_copy(i + 1, o).wait()
                issue_gather(o)

            @pl.when(i + 2 < n)
            def _():
                idx_copy(i + 2, s).start()

            wait_gather(s)
            gts = gt_ref[...]
            for c in range(ROW_SUB):
                cs = slice(c * LANES, (c + 1) * LANES)
                acc = _load_slab_cols(h_ref, tt, c)
                for k in range(TOP_K):
                    gate = gts[:, k * GATE_LANES:k * GATE_LANES + 1]
                    acc = acc + gate * ybuf[s][k, pl.ds(c, tt, stride=ROW_SUB), :]
                o_ref[:, cs] = acc


def _combine(dest_tiles, y_rows, h, gates_tok):
    n_tiles, width = dest_tiles.shape
    tt = width // TOP_K
    n_tok, d_model = h.shape[0] // ROW_SUB, ROW_SUB * LANES
    any_spec = pl.BlockSpec(memory_space=pl.ANY)
    return pl.pallas_call(
        functools.partial(_combine_body, tt=tt),
        grid=(n_tiles,),
        in_specs=[any_spec, any_spec,
                  pl.BlockSpec((tt * ROW_SUB, LANES), lambda i: (i, 0)),
                  pl.BlockSpec((tt, LANES), lambda i: (i, 0))],
        out_specs=pl.BlockSpec((tt, d_model), lambda i: (i, 0)),
        out_shape=jax.ShapeDtypeStruct((n_tok, d_model), F32),
        scratch_shapes=[
            pltpu.SMEM((TOP_K * tt,), jnp.int32),
            pltpu.SMEM((TOP_K * tt,), jnp.int32),
            pltpu.VMEM((TOP_K, tt * ROW_SUB, LANES), F32),
            pltpu.VMEM((TOP_K, tt * ROW_SUB, LANES), F32),
            pltpu.SemaphoreType.DMA((2,)),
            pltpu.SemaphoreType.DMA((2,)),
        ],
        compiler_params=_cparams(("arbitrary",)),
        name="combine",
    )(dest_tiles, y_rows, h, gates_tok)


def _block_diag_ones():
    r = lax.broadcasted_iota(jnp.int32, (NORM_BLOCK, NORM_BLOCK), 0) // HEAD_DIM
    c = lax.broadcasted_iota(jnp.int32, (NORM_BLOCK, NORM_BLOCK), 1) // HEAD_DIM
    return (r == c).astype(BF16)


def _dest_tiles(dest, tt):
    n_tok = dest.shape[1]
    tiles = dest[:TOP_K].reshape(TOP_K, n_tok // tt, tt).transpose(1, 0, 2).reshape(n_tok // tt, TOP_K * tt)
    return tiles * ROW_SUB


def _layer(xp, xs, ck, cv, sp, lam, lam_init, ng, w_in, qng, kng, slg, pw, ps, w_out, fg,
           rw, rb, w_gate, b_gate, w_up, b_up, w_down, b_down):
    bp, t_len, d_model = xp.shape
    bs, ts, _ = xs.shape
    past = ck.shape[1]
    pool_w = sp.shape[-1]
    n_p, n_s = bp * t_len, bs * ts

    w_in_b = w_in.astype(BF16)
    w_out_b = w_out.astype(BF16)
    pw_b = pw.astype(BF16)
    reps = NORM_BLOCK // HEAD_DIM
    qg = (jnp.tile(qng.astype(F32), reps) * (HEAD_DIM ** -0.5)).reshape(1, NORM_BLOCK)
    kg = jnp.tile(kng.astype(F32), reps).reshape(1, NORM_BLOCK)
    bd = _block_diag_ones()
    ng2 = ng.astype(F32).reshape(1, d_model)
    ps2 = ps.astype(F32).reshape(1, d_model)
    sg = (slg.astype(F32) * (1.0 - lam_init)).reshape(1, V_DIM)
    slopes = jnp.exp2(-(8.0 / N_HEADS) * jnp.arange(1, N_HEADS + 1, dtype=F32))
    lam1 = lam.reshape(1).astype(F32)
    fg2 = fg.astype(F32).reshape(1, d_model)
    rw_b = jnp.zeros((d_model, LANES), BF16).at[:, :N_EXPERTS].set(rw.astype(BF16))
    rb2 = jnp.zeros((1, LANES), F32).at[0, :N_EXPERTS].set(rb.astype(F32))

    tm = min(ROW_TILE, t_len)
    zero_pre = jnp.zeros((bp, POOL_HALO, pool_w), F32)
    qp, kp, kpb, vp, vpb, sap, gpp, utp = _inproj(
        xp, zero_pre, ng2, w_in_b, qg, kg, bd, pw_b, ps2, nseg=1, seg_len=tm, start_pos=0, carry=True)
    atp = _attn_prompt(qp, kpb, vpb, slopes, lam1, sg)
    hp, tip, tgp, cntp = _outproj(atp.reshape(n_p, d_model), sap.reshape(n_p, d_model),
                                  gpp.reshape(n_p, d_model), xp.reshape(n_p, d_model),
                                  w_out_b, fg2, rw_b, rb2)

    pre_s = jnp.concatenate([jnp.zeros((bs, POOL_HALO - sp.shape[1], pool_w), F32), sp.astype(F32)], axis=1)
    qs, ks, ksb, vs, vsb, sas, gps, uts = _inproj(
        xs.reshape(1, n_s, d_model), pre_s, ng2, w_in_b, qg, kg, bd, pw_b, ps2,
        nseg=bs, seg_len=ts, start_pos=past, carry=False)
    ats = _attn_sample(qs.reshape(bs, ts, d_model), ksb.reshape(bs, ts, d_model), vsb.reshape(bs, ts, d_model),
                       ck.reshape(bs, past * N_HEADS, V_DIM), cv.reshape(bs, past * N_HEADS, V_DIM),
                       slopes, lam1, sg)
    hs, tis, tgs, cnts = _outproj(ats.reshape(n_s, d_model), sas.reshape(n_s, d_model),
                                  gps.reshape(n_s, d_model), xs.reshape(n_s, d_model),
                                  w_out_b, fg2, rw_b, rb2)

    n_tok = n_p + n_s
    topi = jnp.concatenate([tip, tis], axis=1)
    cnt = (cntp[:, 0] + cnts[:, 0]).astype(jnp.int32).reshape(TOP_K, N_EXPERTS)
    per_expert = jnp.sum(cnt, axis=0)
    padded = (per_expert + MOE_TILE - 1) // MOE_TILE * MOE_TILE
    pad_end = jnp.cumsum(padded)
    pad_start = pad_end - padded
    base = pad_start[None, :] + jnp.cumsum(cnt, axis=0) - cnt
    base_f = jnp.broadcast_to(base.reshape(-1, 1).astype(F32), (TOP_K * N_EXPERTS, LANES))
    tt = TOK_TILE
    rt = max(m * tt for m in range(1, RANK_TILE_MAX // tt + 1) if n_tok % (m * tt) == 0)
    tri = (lax.broadcasted_iota(jnp.int32, (rt, rt), 0) <= lax.broadcasted_iota(jnp.int32, (rt, rt), 1)).astype(BF16)
    dest = _rank(topi, base_f, tri)

    n_tiles = -(-(n_tok * TOP_K) // MOE_TILE) + N_EXPERTS
    n_used = (pad_end[-1] // MOE_TILE).astype(jnp.int32)
    tile_start = jnp.arange(n_tiles, dtype=jnp.int32) * MOE_TILE
    last_start = jnp.maximum(pad_end[-1] - MOE_TILE, 0)
    tile_expert = jnp.minimum(
        jnp.sum((jnp.minimum(tile_start, last_start)[:, None] >= pad_end[None, :]).astype(jnp.int32), axis=1),
        N_EXPERTS - 1)
    dest_p = _dest_tiles(dest[:, :n_p], tt)
    dest_s = _dest_tiles(dest[:, n_p:], tt)
    dest_all = jnp.concatenate([dest_p, dest_s], axis=0)
    assert MOE_TILE <= 2 * tt
    x_rows = _dispatch(pad_start + per_expert, padded - per_expert, (pad_end[-1:] // tt).astype(jnp.int32),
                       dest_all, hp, hs, n_tiles * MOE_TILE)
    e_ids = jnp.arange(N_EXPERTS, dtype=jnp.int32)
    later_present = jnp.logical_and(e_ids[None, :] > e_ids[:, None], (padded > 0)[None, :])
    next_of = jnp.min(jnp.where(later_present, e_ids[None, :], N_EXPERTS), axis=1)
    next_of = jnp.where(next_of == N_EXPERTS, -1, next_of).astype(jnp.int32)
    next_expert = jnp.sum(jnp.where(tile_expert[:, None] == e_ids[None, :], next_of[None, :], 0), axis=1)
    y_rows = _moe(tile_expert, n_used.reshape(1), next_expert.astype(jnp.int32), x_rows, fg2,
                  w_gate, b_gate, w_up, b_up, w_down, b_down, tm=MOE_TILE)

    yp = _combine(dest_p, y_rows, hp, tgp)
    ys = _combine(dest_s, y_rows, hs, tgs)

    heads = (N_HEADS, V_DIM)
    return (yp.reshape(bp, t_len, d_model), ys.reshape(bs, ts, d_model),
            kp.reshape(bp, t_len, *heads), vp.reshape(bp, t_len, *heads), utp[:, 1:],
            ks.reshape(bs, ts, *heads), vs.reshape(bs, ts, *heads), uts[:, 1:])


def kernel(x_prompt, x_sample, cache_k, cache_v, state_pool, norm_mix_g, w_in, q_norm_g, k_norm_g,
           lambda_q1, lambda_k1, lambda_q2, lambda_k2, subln_g, pool_w, pool_scale, w_out, norm_ffn_g,
           router_w, router_b, w_gate, b_gate, w_up, b_up, w_down, b_down):
    depth = w_in.shape[0]
    hp, hs = x_prompt, x_sample
    outs = [[] for _ in range(6)]
    for layer in range(depth):
        lam_init = 0.8 - 0.6 * math.exp(-0.3 * layer)
        lam = (jnp.exp(jnp.sum(lambda_q1[layer].astype(F32) * lambda_k1[layer].astype(F32)))
               - jnp.exp(jnp.sum(lambda_q2[layer].astype(F32) * lambda_k2[layer].astype(F32)))
               + lam_init)
        hp, hs, kp, vp, up, ks, vs, us = _layer(
            hp, hs, cache_k[layer], cache_v[layer], state_pool[layer], lam, lam_init,
            norm_mix_g[layer], w_in[layer], q_norm_g[layer], k_norm_g[layer], subln_g[layer],
            pool_w[layer], pool_scale[layer], w_out[layer], norm_ffn_g[layer],
            router_w[layer], router_b[layer], w_gate[layer], b_gate[layer], w_up[layer], b_up[layer],
            w_down[layer], b_down[layer])
        for lst, val in zip(outs, (kp, vp, up, ks, vs, us)):
            lst.append(val)
    return (hp, hs) + tuple(jnp.stack(o) for o in outs)
```

```python
import functools
import math

import jax
import jax.numpy as jnp
from jax import lax
from jax.experimental import pallas as pl
from jax.experimental.pallas import tpu as pltpu

F32 = jnp.float32
BF16 = jnp.bfloat16

CHUNK = 64
N_HEADS = 8
HEAD_DIM = 64
V_DIM = 2 * HEAD_DIM
POOL_WINDOWS = (2, 4, 8, 16)
POOL_GROUP_DIM = 128
POOL_OUT_DIM = 256
POOL_HALO = 16
N_EXPERTS = 32
TOP_K = 4
SWIGLU_LIMIT = 7.0
SWIGLU_ALPHA = 1.702
RMS_EPS = 1e-6
NEG_INF = -1e30

LANES = 128
ROW_SUB = 8
NORM_BLOCK = 256
VMEM_LIMIT = 56 * 1024 * 1024

ROW_TILE = 512
ATTN_TILE = 256
KEY_CHUNK = 512
MOE_TILE = 512
MOE_SUBTILES = 2
GATE_LANES = LANES // TOP_K
TOK_TILE = 256
RANK_TILE_MAX = 1024


def _sigmoid(x):
    return 1.0 / (1.0 + jnp.exp(-x))


def _cparams(sem):
    return pltpu.CompilerParams(dimension_semantics=sem, vmem_limit_bytes=VMEM_LIMIT)


def _store_slabs(ref, val, row0=0):
    rows = val.shape[0]
    for c in range(ROW_SUB):
        ref[pl.ds(row0 * ROW_SUB + c, rows, stride=ROW_SUB), :] = val[:, c * LANES:(c + 1) * LANES]


def _load_slab_cols(ref, rows, c, row0=0):
    return ref[pl.ds(row0 * ROW_SUB + c, rows, stride=ROW_SUB), :]


def _inproj_body(x_ref, pre_ref, ng_ref, w_ref, qg_ref, kg_ref, bd_ref, pw_ref, ps_ref,
                 q_ref, k_ref, kb_ref, v_ref, vb_ref, sa_ref, gp_ref, ut_ref, ext_ref,
                 *, nseg, seg_len, start_pos, carry, attn_w, pool_w):
    rows = nseg * seg_len
    t = pl.program_id(1)
    x = x_ref[0]
    ms = jnp.mean(x * x, axis=-1, keepdims=True)
    xn = (x * lax.rsqrt(ms + RMS_EPS) * ng_ref[...]).astype(BF16)

    def proj(c0, width):
        return jnp.dot(xn, w_ref[:, c0:c0 + width], preferred_element_type=F32)

    bd = bd_ref[...]

    def group_norm(p, g_ref):
        ss = jnp.dot((p * p).astype(BF16), bd, preferred_element_type=F32)
        return p * lax.rsqrt(ss * (1.0 / HEAD_DIM) + RMS_EPS) * g_ref[...]

    nb = NORM_BLOCK
    heads_per_block = nb // V_DIM

    def store_heads(ref, val, c):
        for j in range(heads_per_block):
            head = c * heads_per_block + j
            ref[0, pl.ds(head, rows, stride=N_HEADS), :] = val[:, j * V_DIM:(j + 1) * V_DIM]

    def sink_q(val, c):
        q_ref[0, :, c * nb:(c + 1) * nb] = group_norm(val, qg_ref).astype(BF16)

    def sink_k(val, c):
        kn = group_norm(val, kg_ref)
        store_heads(k_ref, kn, c)
        kb_ref[0, :, c * nb:(c + 1) * nb] = kn.astype(BF16)

    def sink_v(val, c):
        store_heads(v_ref, val, c)
        vb_ref[0, :, c * nb:(c + 1) * nb] = val.astype(BF16)

    n_chunks = attn_w // nb
    work = [(part * attn_w + c * nb, sink, c)
            for part, sink in enumerate((sink_q, sink_k, sink_v)) for c in range(n_chunks)]
    pending = proj(work[0][0], nb)
    for j, (_, sink, c) in enumerate(work):
        cur = pending
        if j + 1 < len(work):
            pending = proj(work[j + 1][0], nb)
        sink(cur, c)

    if carry:
        @pl.when(t == 0)
        def _():
            ext_ref[:, 0:POOL_HALO, :] = pre_ref[...]
    else:
        ext_ref[:, 0:POOL_HALO, :] = pre_ref[...]
    for c in range(pool_w // nb):
        cs = slice(c * nb, (c + 1) * nb)
        u = proj(3 * attn_w + c * nb, nb)
        ext_ref[:, POOL_HALO:POOL_HALO + seg_len, cs] = u.reshape(nseg, seg_len, nb)

    ga0 = 3 * attn_w + pool_w
    d_model = attn_w
    for c in range(d_model // nb):
        cs = slice(c * nb, (c + 1) * nb)
        sa_ref[0, :, cs] = _sigmoid(proj(ga0 + c * nb, nb)).astype(BF16)

    gb0 = ga0 + d_model
    row = lax.broadcasted_iota(jnp.int32, (1, seg_len, 1), 1)
    pos = row + start_pos
    if carry:
        pos = pos + t * seg_len
    for g, w in enumerate(POOL_WINDOWS):
        cs = slice(g * POOL_GROUP_DIM, (g + 1) * POOL_GROUP_DIM)
        own = ext_ref[:, POOL_HALO:POOL_HALO + seg_len, cs]
        acc = own
        for i in range(1, w):
            acc = acc + ext_ref[:, POOL_HALO - i:POOL_HALO - i + seg_len, cs]
        inv = 1.0 / jnp.minimum(w, pos + 1).astype(F32)
        z = (acc * inv - own).reshape(rows, POOL_GROUP_DIM)
        os_ = slice(g * POOL_OUT_DIM, (g + 1) * POOL_OUT_DIM)
        yp = jnp.dot(z.astype(BF16), pw_ref[g], preferred_element_type=F32) * ps_ref[:, os_]
        gb = proj(gb0 + g * POOL_OUT_DIM, POOL_OUT_DIM)
        gp_ref[0, :, os_] = (_sigmoid(gb) * yp).astype(BF16)

    tail = ext_ref[:, seg_len:seg_len + POOL_HALO, :]
    ut_ref[...] = tail
    if carry:
        ext_ref[:, 0:POOL_HALO, :] = tail


def _inproj(x3, prefix, ng, w_in, qg, kg, bd, pw, ps, *, nseg, seg_len, start_pos, carry):
    groups, t_len, d_model = x3.shape
    rows = nseg * seg_len
    steps = t_len // rows
    in_cols = w_in.shape[1]
    pool_w = prefix.shape[-1]
    attn_w = d_model
    assert in_cols == 3 * attn_w + pool_w + 2 * d_model
    tok = lambda b, t: (b, t, 0)
    fixed2 = lambda b, t: (0, 0)
    act = lambda dt: jax.ShapeDtypeStruct((groups, t_len, d_model), dt)
    by_head = jax.ShapeDtypeStruct((groups, t_len * N_HEADS, V_DIM), F32)
    tok_spec = pl.BlockSpec((1, rows, d_model), tok)
    head_spec = pl.BlockSpec((1, rows * N_HEADS, V_DIM), tok)
    body = functools.partial(_inproj_body, nseg=nseg, seg_len=seg_len, start_pos=start_pos,
                             carry=carry, attn_w=attn_w, pool_w=pool_w)
    return pl.pallas_call(
        body,
        grid=(groups, steps),
        in_specs=[
            pl.BlockSpec((1, rows, d_model), tok),
            pl.BlockSpec((nseg, POOL_HALO, pool_w), lambda b, t: (b, 0, 0)),
            pl.BlockSpec((1, d_model), fixed2),
            pl.BlockSpec((d_model, in_cols), fixed2, pipeline_mode=pl.Buffered(1)),
            pl.BlockSpec((1, NORM_BLOCK), fixed2),
            pl.BlockSpec((1, NORM_BLOCK), fixed2),
            pl.BlockSpec((NORM_BLOCK, NORM_BLOCK), fixed2),
            pl.BlockSpec(pw.shape, lambda b, t: (0, 0, 0)),
            pl.BlockSpec((1, d_model), fixed2),
        ],
        out_specs=[tok_spec, head_spec, tok_spec, head_spec, tok_spec, tok_spec, tok_spec,
                   pl.BlockSpec((nseg, POOL_HALO, pool_w), lambda b, t: (b, 0, 0))],
        out_shape=[act(BF16), by_head, act(BF16), by_head, act(BF16), act(BF16), act(BF16),
                   jax.ShapeDtypeStruct(prefix.shape, F32)],
        scratch_shapes=[pltpu.VMEM((nseg, POOL_HALO + seg_len, pool_w), F32)],
        compiler_params=_cparams(("arbitrary", "arbitrary")),
        name="inproj",
    )(x3, prefix, ng, w_in, qg, kg, bd, pw, ps)


def _half_masks(q):
    lane = lax.broadcasted_iota(jnp.int32, q.shape, 1)
    zero = jnp.zeros_like(q)
    return jnp.where(lane < HEAD_DIM, q, zero), jnp.where(lane >= HEAD_DIM, q, zero)


def _qk(qz, kblk):
    return lax.dot_general(qz, kblk, (((1,), (1,)), ((), ())), preferred_element_type=F32)


def _subln(o, sg_ref):
    ms = jnp.mean(o * o, axis=-1, keepdims=True)
    return o * lax.rsqrt(ms + RMS_EPS) * sg_ref[...]


def _attn_body(slope_ref, lam_ref, q_ref, k_ref, v_ref, ka_ref, sg_ref, o_ref,
               kf_ref, vf_ref, s_ref, p_ref, *, tq, nq):
    h = pl.program_id(1)
    slope = slope_ref[h]
    lam = lam_ref[0]
    lane = lax.broadcasted_iota(jnp.int32, (tq, LANES), 1)
    kf_ref[:, 0:V_DIM] = k_ref[0]
    kf_ref[:, V_DIM:] = ka_ref[...]
    vf_ref[:, 0:V_DIM] = v_ref[0]
    t_len = vf_ref.shape[0]
    vf_ref[:, V_DIM:] = (lax.broadcasted_iota(jnp.int32, (t_len, LANES), 1) == 0).astype(BF16)

    row = lax.broadcasted_iota(jnp.int32, (tq, tq), 0)
    col = lax.broadcasted_iota(jnp.int32, (tq, tq), 1)
    rc = (row - col).astype(F32)
    vis = lax.shift_right_logical(col, 6) <= lax.shift_right_logical(row, 6)
    corr = jnp.where(vis, jnp.minimum(rc, 0.0) * (2.0 * slope), NEG_INF)

    def scores(qi, m):
        nk = (qi + 1) * tq
        qz = _half_masks(q_ref[0, qi * tq:(qi + 1) * tq, :])[m]
        t = lax.broadcasted_iota(jnp.int32, (tq, LANES), 0) + qi * tq
        t_hi = lax.shift_left(lax.shift_right_logical(t, 8), 8).astype(F32)
        t_lo = jnp.bitwise_and(t, 255).astype(F32)
        qaug = jnp.where(lane == 0, -slope * t_hi,
                         jnp.where(lane == 1, -slope * t_lo,
                                   jnp.where(lane < 4, slope, 0.0))).astype(BF16)
        qa = jnp.concatenate([qz, qaug], axis=1)
        for c0 in range(0, nk, KEY_CHUNK):
            c1 = min(c0 + KEY_CHUNK, nk)
            s_ref[qi % 2, m, :, c0:c1] = _qk(qa, kf_ref[c0:c1, :])

    def softmax_pv(qi, m):
        nk = (qi + 1) * tq
        b = qi % 2
        s_ref[b, m, :, nk - tq:nk] += corr
        chunks = [(c0, min(c0 + KEY_CHUNK, nk)) for c0 in range(0, nk, KEY_CHUNK)]
        mx = None
        for c0, c1 in chunks:
            part = jnp.max(s_ref[b, m, :, c0:c1], axis=-1, keepdims=True)
            mx = part if mx is None else jnp.maximum(mx, part)
        for c0, c1 in chunks:
            p_ref[m, :, c0:c1] = jnp.exp(s_ref[b, m, :, c0:c1] - mx).astype(BF16)
        return jnp.dot(p_ref[m, :, 0:nk], vf_ref[0:nk, :], preferred_element_type=F32)

    scores(0, 0)
    scores(0, 1)
    for qi in range(nq):
        if qi + 1 < nq:
            scores(qi + 1, 0)
            scores(qi + 1, 1)
        o1, o2 = softmax_pv(qi, 0), softmax_pv(qi, 1)
        c1 = 1.0 / o1[:, V_DIM:V_DIM + 1]
        c2 = lam / o2[:, V_DIM:V_DIM + 1]
        o = o1[:, 0:V_DIM] * c1 - o2[:, 0:V_DIM] * c2
        o_ref[0, qi * tq:(qi + 1) * tq, :] = _subln(o, sg_ref).astype(BF16)


def _attn_prompt(q, kb, vb, slopes, lam, sg):
    batch, t_len, width = q.shape
    tq = min(ATTN_TILE, t_len)
    nq = t_len // tq
    pos = jnp.arange(t_len, dtype=jnp.int32)
    ka = jnp.zeros((t_len, LANES), F32)
    ka = ka.at[:, 0:2].set(1.0).at[:, 2].set(((pos >> 8) << 8).astype(F32)).at[:, 3].set((pos & 255).astype(F32))
    smem = pl.BlockSpec(memory_space=pltpu.SMEM)
    seq = pl.BlockSpec((1, t_len, V_DIM), lambda b, h: (b, 0, h))
    return pl.pallas_call(
        functools.partial(_attn_body, tq=tq, nq=nq),
        grid=(batch, N_HEADS),
        in_specs=[
            smem, smem, seq, seq, seq,
            pl.BlockSpec((t_len, LANES), lambda b, h: (0, 0)),
            pl.BlockSpec((1, V_DIM), lambda b, h: (0, 0)),
        ],
        out_specs=seq,
        out_shape=jax.ShapeDtypeStruct((batch, t_len, width), BF16),
        scratch_shapes=[
            pltpu.VMEM((t_len, V_DIM + LANES), BF16),
            pltpu.VMEM((t_len, V_DIM + LANES), BF16),
            pltpu.VMEM((2, 2, tq, t_len), F32),
            pltpu.VMEM((2, tq, t_len), BF16),
        ],
        compiler_params=_cparams(("arbitrary", "arbitrary")),
        name="attn_prompt",
    )(slopes, lam, q, kb, vb, ka.astype(BF16), sg)


def _attn_dec_body(slope_ref, lam_ref, q_ref, kc_ref, vc_ref, kn_ref, vn_ref, sg_ref, o_ref, *, past):
    lam = lam_ref[0]
    tq = q_ref.shape[1]
    qpos_a = lax.broadcasted_iota(jnp.int32, (tq, past), 0) + past
    kpos_a = lax.broadcasted_iota(jnp.int32, (tq, past), 1)
    qpos_b = lax.broadcasted_iota(jnp.int32, (tq, tq), 0) + past
    kpos_b = lax.broadcasted_iota(jnp.int32, (tq, tq), 1) + past

    def dist_and_vis(qpos, kpos):
        vis = lax.shift_right_logical(kpos, 6) <= lax.shift_right_logical(qpos, 6)
        return jnp.abs(qpos - kpos).astype(F32), vis

    dist_a, vis_a = dist_and_vis(qpos_a, kpos_a)
    dist_b, vis_b = dist_and_vis(qpos_b, kpos_b)

    def scores(h):
        cs = slice(h * V_DIM, (h + 1) * V_DIM)
        q1z, q2z = _half_masks(q_ref[0, :, cs])
        kc = kc_ref[0, pl.ds(h, past, stride=N_HEADS), :].astype(BF16)
        qcat = jnp.concatenate([q1z, q2z], axis=0)
        return _qk(qcat, kc), _qk(qcat, kn_ref[0, :, cs])

    pending = scores(0)
    for h in range(N_HEADS):
        sa_both, sb_both = pending
        if h + 1 < N_HEADS:
            pending = scores(h + 1)
        cs = slice(h * V_DIM, (h + 1) * V_DIM)
        slope = slope_ref[h]
        bias_a = jnp.where(vis_a, -slope * dist_a, NEG_INF)
        bias_b = jnp.where(vis_b, -slope * dist_b, NEG_INF)
        vc = vc_ref[0, pl.ds(h, past, stride=N_HEADS), :].astype(BF16)

        def softmax_parts(m_idx, sa_both=sa_both, sb_both=sb_both, bias_a=bias_a, bias_b=bias_b):
            rs = slice(m_idx * tq, (m_idx + 1) * tq)
            sa = sa_both[rs, :] + bias_a
            sb = sb_both[rs, :] + bias_b
            m = jnp.maximum(jnp.max(sa, axis=-1, keepdims=True), jnp.max(sb, axis=-1, keepdims=True))
            pa = jnp.exp(sa - m)
            pb = jnp.exp(sb - m)
            l = jnp.sum(pa, axis=-1, keepdims=True) + jnp.sum(pb, axis=-1, keepdims=True)
            return pa, pb, l

        pa1, pb1, l1 = softmax_parts(0)
        pa2, pb2, l2 = softmax_parts(1)
        c1 = 1.0 / l1
        c2 = lam / l2
        wa = (pa1 * c1 - pa2 * c2).astype(BF16)
        wb = (pb1 * c1 - pb2 * c2).astype(BF16)
        o = (jnp.dot(wa, vc, preferred_element_type=F32)
             + jnp.dot(wb, vn_ref[0, :, cs], preferred_element_type=F32))
        o_ref[0, :, cs] = _subln(o, sg_ref).astype(BF16)


def _attn_sample(q, kn, vn, cache_k, cache_v, slopes, lam, sg):
    batch, tq, width = q.shape
    past = cache_k.shape[1] // N_HEADS
    smem = pl.BlockSpec(memory_space=pltpu.SMEM)
    new = pl.BlockSpec((1, tq, width), lambda b: (b, 0, 0))
    old = pl.BlockSpec((1, past * N_HEADS, V_DIM), lambda b: (b, 0, 0))
    return pl.pallas_call(
        functools.partial(_attn_dec_body, past=past),
        grid=(batch,),
        in_specs=[smem, smem, new, old, old, new, new,
                  pl.BlockSpec((1, V_DIM), lambda b: (0, 0))],
        out_specs=new,
        out_shape=jax.ShapeDtypeStruct((batch, tq, width), BF16),
        compiler_params=_cparams(("arbitrary",)),
        name="attn_sample",
    )(slopes, lam, q, cache_k, cache_v, kn, vn, sg)


def _outproj_body(at_ref, sa_ref, gp_ref, x_ref, wo_ref, g_ref, rw_ref, rb_ref,
                  h_ref, ti_ref, tg_ref, cnt_ref):
    i = pl.program_id(0)
    rows = x_ref.shape[0]
    n_sub = 2 if rows % (2 * LANES) == 0 else 1
    sub = rows // n_sub

    def project(j):
        rs = slice(j * sub, (j + 1) * sub)
        merged = (sa_ref[rs, :].astype(F32) * at_ref[rs, :].astype(F32) + gp_ref[rs, :].astype(F32)).astype(BF16)
        return x_ref[rs, :] + jnp.dot(merged, wo_ref[...], preferred_element_type=F32)

    def route(hh, j):
        _store_slabs(h_ref, hh, j * sub)
        ms = jnp.mean(hh * hh, axis=-1, keepdims=True)
        hn = (hh * lax.rsqrt(ms + RMS_EPS) * g_ref[...]).astype(BF16)
        logits = jnp.dot(hn, rw_ref[...], preferred_element_type=F32) + rb_ref[...]
        lt = jnp.transpose(logits)[0:N_EXPERTS, :]
        e_iota = lax.broadcasted_iota(jnp.int32, (N_EXPERTS, sub), 0)
        vals, idxs, hots = [], [], []
        cur = lt
        for _ in range(TOP_K):
            m = jnp.max(cur, axis=0, keepdims=True)
            idx = jnp.min(jnp.where(cur == m, e_iota, N_EXPERTS), axis=0, keepdims=True)
            hit = e_iota == idx
            vals.append(m)
            idxs.append(idx)
            hots.append(hit)
            cur = jnp.where(hit, -jnp.inf, cur)
        ex = [jnp.exp(v - vals[0]) for v in vals]
        inv = 1.0 / (ex[0] + ex[1] + ex[2] + ex[3])
        zi = jnp.zeros((8 - TOP_K, sub), jnp.int32)
        cs = slice(j * sub, (j + 1) * sub)
        ti_ref[:, cs] = jnp.concatenate(idxs + [zi], axis=0)
        wide = jnp.concatenate([jnp.broadcast_to(e * inv, (GATE_LANES, sub)) for e in ex], axis=0)
        tg_ref[cs, :] = jnp.transpose(wide)
        hot = jnp.concatenate([hh_.astype(F32) for hh_ in hots], axis=0)
        return jnp.sum(hot, axis=1, keepdims=True)

    hs = [project(j) for j in range(n_sub)]
    csum = route(hs[0], 0)
    for j in range(1, n_sub):
        csum = csum + route(hs[j], j)

    @pl.when(i == 0)
    def _():
        cnt_ref[...] = jnp.zeros_like(cnt_ref)

    cnt_ref[...] += jnp.broadcast_to(csum, cnt_ref.shape)


def _outproj(attn, sa, gp, x, w_out, g, rw, rb):
    n_tok, d_model = x.shape
    tm = min(2 * ROW_TILE, n_tok)
    row = lambda i: (i, 0)
    fixed = lambda i: (0, 0)
    colb = lambda i: (0, i)
    return pl.pallas_call(
        _outproj_body,
        grid=(n_tok // tm,),
        in_specs=[
            pl.BlockSpec((tm, d_model), row),
            pl.BlockSpec((tm, d_model), row),
            pl.BlockSpec((tm, d_model), row),
            pl.BlockSpec((tm, d_model), row),
            pl.BlockSpec((d_model, d_model), fixed),
            pl.BlockSpec((1, d_model), fixed),
            pl.BlockSpec((d_model, LANES), fixed),
            pl.BlockSpec((1, LANES), fixed),
        ],
        out_specs=[
            pl.BlockSpec((tm * ROW_SUB, LANES), row),
            pl.BlockSpec((8, tm), colb),
            pl.BlockSpec((tm, LANES), row),
            pl.BlockSpec((TOP_K * N_EXPERTS, LANES), fixed),
        ],
        out_shape=[
            jax.ShapeDtypeStruct((n_tok * ROW_SUB, LANES), F32),
            jax.ShapeDtypeStruct((8, n_tok), jnp.int32),
            jax.ShapeDtypeStruct((n_tok, LANES), F32),
            jax.ShapeDtypeStruct((TOP_K * N_EXPERTS, LANES), F32),
        ],
        compiler_params=_cparams(("arbitrary",)),
        name="outproj",
    )(attn, sa, gp, x, w_out, g, rw, rb)


def _rank_body(ti_ref, base_ref, tri_ref, dest_ref, carry_ref):
    i = pl.program_id(0)

    @pl.when(i == 0)
    def _():
        carry_ref[...] = jnp.zeros_like(carry_ref)

    tt = ti_ref.shape[1]
    e_iota = lax.broadcasted_iota(jnp.int32, (N_EXPERTS, tt), 0)
    hot = jnp.concatenate([(ti_ref[k:k + 1, :] == e_iota).astype(F32) for k in range(TOP_K)], axis=0)
    incl = jnp.dot(hot.astype(BF16), tri_ref[...], preferred_element_type=F32)
    slot = base_ref[:, 0:1] + carry_ref[:, 0:1] + incl - 1.0
    picked = hot * slot
    rows = [jnp.sum(picked[k * N_EXPERTS:(k + 1) * N_EXPERTS, :], axis=0, keepdims=True)
            for k in range(TOP_K)]
    rows.append(jnp.zeros((8 - TOP_K, tt), F32))
    dest_ref[...] = jnp.concatenate(rows, axis=0).astype(jnp.int32)
    carry_ref[...] += jnp.broadcast_to(jnp.sum(hot, axis=1, keepdims=True), carry_ref.shape)


def _rank(topi, base, tri):
    n_tok = topi.shape[1]
    tt = tri.shape[0]
    return pl.pallas_call(
        _rank_body,
        grid=(n_tok // tt,),
        in_specs=[
            pl.BlockSpec((8, tt), lambda i: (0, i)),
            pl.BlockSpec(base.shape, lambda i: (0, 0)),
            pl.BlockSpec((tt, tt), lambda i: (0, 0)),
        ],
        out_specs=pl.BlockSpec((8, tt), lambda i: (0, i)),
        out_shape=jax.ShapeDtypeStruct((8, n_tok), jnp.int32),
        scratch_shapes=[pltpu.VMEM((TOP_K * N_EXPERTS, LANES), F32)],
        compiler_params=_cparams(("arbitrary",)),
        name="rank",
    )(topi, base, tri)


DISPATCH_BUFS = 3


def _dispatch_body(zs_ref, zl_ref, tz_ref, dest_hbm, hp_hbm, hs_hbm, xs_hbm,
                   idx0, idx1, idx2, hb0, hb1, hb2, lsem, dsem, zsem,
                   *, tt, n_prompt_tiles, n_out_tiles):
    i = pl.program_id(0)
    n = pl.num_programs(0)
    idx = (idx0, idx1, idx2)
    hbuf = (hb0, hb1, hb2)
    nbuf = DISPATCH_BUFS
    tile_sub = tt * ROW_SUB

    def load_wait(s):
        pltpu.make_async_copy(dest_hbm.at[0], idx[s], lsem.at[s]).wait()
        pltpu.make_async_copy(hp_hbm.at[pl.ds(0, tile_sub)], hbuf[s], lsem.at[s]).wait()

    def load_start(tile, s):
        pltpu.make_async_copy(dest_hbm.at[tile], idx[s], lsem.at[s]).start()

        @pl.when(tile < n_prompt_tiles)
        def _():
            r0 = pl.multiple_of(tile * tile_sub, tile_sub)
            pltpu.make_async_copy(hp_hbm.at[pl.ds(r0, tile_sub)], hbuf[s], lsem.at[s]).start()

        @pl.when(tile >= n_prompt_tiles)
        def _():
            r0 = pl.multiple_of((tile - n_prompt_tiles) * tile_sub, tile_sub)
            pltpu.make_async_copy(hs_hbm.at[pl.ds(r0, tile_sub)], hbuf[s], lsem.at[s]).start()

    def scatter_start(s):
        for k in range(TOP_K):
            for r in range(tt):
                dst0 = pl.multiple_of(idx[s][k * tt + r], ROW_SUB)
                pltpu.make_async_copy(hbuf[s].at[pl.ds(r * ROW_SUB, ROW_SUB)], xs_hbm.at[pl.ds(dst0, ROW_SUB)],
                                      dsem.at[s]).start(priority=r % 2)

    def scatter_wait(s):
        for _ in range(TOP_K):
            pltpu.make_async_copy(hbuf[s], xs_hbm.at[pl.ds(0, tile_sub)], dsem.at[s]).wait()

    @pl.when(i == 0)
    def _():
        zero = hbuf[nbuf - 1]
        zero[...] = jnp.zeros_like(zero)

        def pad_pieces(e, act):
            length = zl_ref[e]
            for shift in range(tt.bit_length() - 1, -1, -1):
                piece = 1 << shift
                first = zs_ref[e] + jnp.bitwise_and(length, ~(2 * piece - 1))

                @pl.when(jnp.bitwise_and(length, piece) != 0)
                def _(piece=piece, first=first):
                    dst0 = pl.multiple_of(first * ROW_SUB, ROW_SUB)
                    act(pltpu.make_async_copy(zero.at[pl.ds(0, piece * ROW_SUB)],
                                              xs_hbm.at[pl.ds(dst0, piece * ROW_SUB)], zsem))

        def per_expert(e, c):
            pad_pieces(e, lambda cp: cp.start())
            pad_pieces(e, lambda cp: cp.wait())
            return c
        lax.fori_loop(0, N_EXPERTS, per_expert, 0)

        def tail_copy(j):
            r0 = pl.multiple_of(j * tile_sub, tile_sub)
            return pltpu.make_async_copy(zero, xs_hbm.at[pl.ds(r0, tile_sub)], zsem)

        def tail_one(j, c):
            tail_copy(j).start()
            tail_copy(j).wait()
            return c
        lax.fori_loop(tz_ref[0], n_out_tiles, tail_one, 0)

        load_start(0, 0)

        @pl.when(n > 1)
        def _():
            load_start(1, 1)

    for s in range(nbuf):
        @pl.when(i % nbuf == s)
        def _(s=s):
            prev = (s + nbuf - 1) % nbuf
            load_wait(s)
            scatter_start(s)

            @pl.when(i > 0)
            def _():
                scatter_wait(prev)

            @pl.when(i + 2 < n)
            def _():
                load_start(i + 2, prev)

            @pl.when(i == n - 1)
            def _():
                scatter_wait(s)


def _dispatch(zero_start, zero_len, tail_tile, dest_tiles, hp, hs, n_rows):
    n_tiles, width = dest_tiles.shape
    tt = width // TOP_K
    n_prompt_tiles = hp.shape[0] // (tt * ROW_SUB)
    any_spec = pl.BlockSpec(memory_space=pl.ANY)
    grid_spec = pltpu.PrefetchScalarGridSpec(
        num_scalar_prefetch=3,
        grid=(n_tiles,),
        in_specs=[any_spec, any_spec, any_spec],
        out_specs=any_spec,
        scratch_shapes=[pltpu.SMEM((width,), jnp.int32)] * DISPATCH_BUFS
        + [pltpu.VMEM((tt * ROW_SUB, LANES), F32)] * DISPATCH_BUFS
        + [
            pltpu.SemaphoreType.DMA((DISPATCH_BUFS,)),
            pltpu.SemaphoreType.DMA((DISPATCH_BUFS,)),
            pltpu.SemaphoreType.DMA,
        ],
    )
    return pl.pallas_call(
        functools.partial(_dispatch_body, tt=tt, n_prompt_tiles=n_prompt_tiles, n_out_tiles=n_rows // tt),
        grid_spec=grid_spec,
        out_shape=jax.ShapeDtypeStruct((n_rows * ROW_SUB, LANES), F32),
        compiler_params=_cparams(("arbitrary",)),
        name="dispatch",
    )(zero_start, zero_len, tail_tile, dest_tiles, hp, hs)


def _moe_body(te_ref, nu_ref, nx_ref, x_ref, g_ref, wg_hbm, wu_hbm, wd_hbm, b_ref,
              y_ref, wgs, wus, wds, wgb, wub, wdb, wsem, *, tm):
    i = pl.program_id(0)
    n_used = nu_ref[0]

    def weight_copies(e):
        return (pltpu.make_async_copy(wg_hbm.at[e], wgs, wsem),
                pltpu.make_async_copy(wu_hbm.at[e], wus, wsem),
                pltpu.make_async_copy(wd_hbm.at[e], wds, wsem))

    @pl.when(i >= n_used)
    def _():
        y_ref[...] = jnp.zeros_like(y_ref)

    @pl.when(i < n_used)
    def _():
        @pl.when(i == 0)
        def _():
            for cp in weight_copies(te_ref[0]):
                cp.start()

        prev = te_ref[jnp.maximum(i - 1, 0)]
        changed = jnp.logical_or(i == 0, te_ref[i] != prev)

        @pl.when(changed)
        def _():
            for cp in weight_copies(te_ref[i]):
                cp.wait()
            wgb[...] = wgs[...].astype(BF16)
            wub[...] = wus[...].astype(BF16)
            wdb[...] = wds[...].astype(BF16)

            @pl.when(nx_ref[i] >= 0)
            def _():
                for cp in weight_copies(nx_ref[i]):
                    cp.start()

        half = tm // MOE_SUBTILES
        d_model = ROW_SUB * LANES
        bias = b_ref[te_ref[i]]

        def normed_half(r0):
            cols = [_load_slab_cols(x_ref, half, c, r0) for c in range(ROW_SUB)]
            ssq = cols[0] * cols[0]
            for xc in cols[1:]:
                ssq = ssq + xc * xc
            r = lax.rsqrt(jnp.sum(ssq, axis=-1, keepdims=True) * (1.0 / d_model) + RMS_EPS)
            return jnp.concatenate(
                [(xc * r * g_ref[:, c * LANES:(c + 1) * LANES]).astype(BF16) for c, xc in enumerate(cols)], axis=1)

        def gate_up(xb):
            gt = jnp.dot(xb, wgb[...], preferred_element_type=F32) + bias[0:1, :]
            up = jnp.dot(xb, wub[...], preferred_element_type=F32) + bias[1:2, :]
            return gt, up

        def down(gt, up, r0):
            gt = jnp.minimum(gt, SWIGLU_LIMIT)
            up = jnp.clip(up, -SWIGLU_LIMIT, SWIGLU_LIMIT)
            hdn = (up + 1.0) * (gt * _sigmoid(SWIGLU_ALPHA * gt))
            y = jnp.dot(hdn.astype(BF16), wdb[...], preferred_element_type=F32) + bias[2:3, :]
            _store_slabs(y_ref, y, r0)

        pending = gate_up(normed_half(0))
        for j in range(MOE_SUBTILES):
            cur = pending
            if j + 1 < MOE_SUBTILES:
                pending = gate_up(normed_half((j + 1) * half))
            down(cur[0], cur[1], j * half)


def _moe(tile_expert, n_used, next_expert, x_rows, g, w_gate, b_gate, w_up, b_up, w_down, b_down, *, tm):
    n_tiles = x_rows.shape[0] // (tm * ROW_SUB)
    d_model = ROW_SUB * LANES
    d_ff = w_gate.shape[2]
    assert w_gate.shape[1] == d_model and d_ff == d_model
    biases = jnp.stack([b_gate, b_up, b_down], axis=1).astype(F32)
    any_spec = pl.BlockSpec(memory_space=pl.ANY)
    grid_spec = pltpu.PrefetchScalarGridSpec(
        num_scalar_prefetch=3,
        grid=(n_tiles,),
        in_specs=[
            pl.BlockSpec((tm * ROW_SUB, LANES), lambda i, te, nu, nx: (jnp.minimum(i, nu[0] - 1), 0)),
            pl.BlockSpec((1, d_model), lambda i, te, nu, nx: (0, 0)),
            any_spec, any_spec, any_spec,
            pl.BlockSpec(biases.shape, lambda i, te, nu, nx: (0, 0, 0)),
        ],
        out_specs=pl.BlockSpec((tm * ROW_SUB, LANES), lambda i, te, nu, nx: (i, 0)),
        scratch_shapes=[
            pltpu.VMEM((d_model, d_ff), F32),
            pltpu.VMEM((d_model, d_ff), F32),
            pltpu.VMEM((d_ff, d_model), F32),
            pltpu.VMEM((d_model, d_ff), BF16),
            pltpu.VMEM((d_model, d_ff), BF16),
            pltpu.VMEM((d_ff, d_model), BF16),
            pltpu.SemaphoreType.DMA,
        ],
    )
    return pl.pallas_call(
        functools.partial(_moe_body, tm=tm),
        grid_spec=grid_spec,
        out_shape=jax.ShapeDtypeStruct((n_tiles * tm * ROW_SUB, LANES), F32),
        compiler_params=_cparams(("arbitrary",)),
        name="moe",
    )(tile_expert, n_used, next_expert, x_rows, g, w_gate, w_up, w_down, biases)


def _combine_body(dest_hbm, y_hbm, h_ref, gt_ref, o_ref, idx0, idx1, yb0, yb1, isem, gsem, *, tt):
    i = pl.program_id(0)
    n = pl.num_programs(0)
    idx = (idx0, idx1)
    ybuf = (yb0, yb1)

    def idx_copy(tile, s):
        return pltpu.make_async_copy(dest_hbm.at[tile], idx[s], isem.at[s])

    def issue_gather(s):
        for k in range(TOP_K):
            for r in range(tt):
                src = pl.multiple_of(idx[s][k * tt + r], ROW_SUB)
                pltpu.make_async_copy(y_hbm.at[pl.ds(src, ROW_SUB)],
                                      ybuf[s].at[k, pl.ds(r * ROW_SUB, ROW_SUB)], gsem.at[s]).start(priority=r % 2)

    def wait_gather(s):
        for k in range(TOP_K):
            pltpu.make_async_copy(y_hbm.at[pl.ds(0, tt * ROW_SUB)], ybuf[s].at[k], gsem.at[s]).wait()

    @pl.when(i == 0)
    def _():
        idx_copy(0, 0).start()
        idx_copy(0, 0).wait()
        issue_gather(0)

        @pl.when(n > 1)
        def _():
            idx_copy(1, 1).start()

    for s in range(2):
        @pl.when(i % 2 == s)
        def _(s=s):
            o = 1 - s

            def weighted_sum():
                gts = gt_ref[...]
                for c in range(ROW_SUB):
                    cs = slice(c * LANES, (c + 1) * LANES)
                    acc = _load_slab_cols(h_ref, tt, c)
                    for k in range(TOP_K):
                        gate = gts[:, k * GATE_LANES:k * GATE_LANES + 1]
                        acc = acc + gate * ybuf[s][k, pl.ds(c, tt, stride=ROW_SUB), :]
                    o_ref[:, cs] = acc

            wait_gather(s)

            @pl.when(i + 1 < n)
            def _():
                idx_copy(i + 1, o).wait()
                issue_gather(o)
                weighted_sum()

            @pl.when(i + 1 >= n)
            def _():
                weighted_sum()

            @pl.when(i + 2 < n)
            def _():
                idx_copy(i + 2, s).start()


def _combine(dest_tiles, y_rows, h, gates_tok):
    n_tiles, width = dest_tiles.shape
    tt = width // TOP_K
    n_tok, d_model = h.shape[0] // ROW_SUB, ROW_SUB * LANES
    any_spec = pl.BlockSpec(memory_space=pl.ANY)
    return pl.pallas_call(
        functools.partial(_combine_body, tt=tt),
        grid=(n_tiles,),
        in_specs=[any_spec, any_spec,
                  pl.BlockSpec((tt * ROW_SUB, LANES), lambda i: (i, 0)),
                  pl.BlockSpec((tt, LANES), lambda i: (i, 0))],
        out_specs=pl.BlockSpec((tt, d_model), lambda i: (i, 0)),
        out_shape=jax.ShapeDtypeStruct((n_tok, d_model), F32),
        scratch_shapes=[
            pltpu.SMEM((TOP_K * tt,), jnp.int32),
            pltpu.SMEM((TOP_K * tt,), jnp.int32),
            pltpu.VMEM((TOP_K, tt * ROW_SUB, LANES), F32),
            pltpu.VMEM((TOP_K, tt * ROW_SUB, LANES), F32),
            pltpu.SemaphoreType.DMA((2,)),
            pltpu.SemaphoreType.DMA((2,)),
        ],
        compiler_params=_cparams(("arbitrary",)),
        name="combine",
    )(dest_tiles, y_rows, h, gates_tok)


def _block_diag_ones():
    r = lax.broadcasted_iota(jnp.int32, (NORM_BLOCK, NORM_BLOCK), 0) // HEAD_DIM
    c = lax.broadcasted_iota(jnp.int32, (NORM_BLOCK, NORM_BLOCK), 1) // HEAD_DIM
    return (r == c).astype(BF16)


def _dest_tiles(dest, tt):
    n_tok = dest.shape[1]
    tiles = dest[:TOP_K].reshape(TOP_K, n_tok // tt, tt).transpose(1, 0, 2).reshape(n_tok // tt, TOP_K * tt)
    return tiles * ROW_SUB


def _layer(xp, xs, ck, cv, sp, lam, lam_init, ng, w_in, qng, kng, slg, pw, ps, w_out, fg,
           rw, rb, w_gate, b_gate, w_up, b_up, w_down, b_down):
    bp, t_len, d_model = xp.shape
    bs, ts, _ = xs.shape
    past = ck.shape[1]
    pool_w = sp.shape[-1]
    n_p, n_s = bp * t_len, bs * ts

    w_in_b = w_in.astype(BF16)
    w_out_b = w_out.astype(BF16)
    pw_b = pw.astype(BF16)
    reps = NORM_BLOCK // HEAD_DIM
    qg = (jnp.tile(qng.astype(F32), reps) * (HEAD_DIM ** -0.5)).reshape(1, NORM_BLOCK)
    kg = jnp.tile(kng.astype(F32), reps).reshape(1, NORM_BLOCK)
    bd = _block_diag_ones()
    ng2 = ng.astype(F32).reshape(1, d_model)
    ps2 = ps.astype(F32).reshape(1, d_model)
    sg = (slg.astype(F32) * (1.0 - lam_init)).reshape(1, V_DIM)
    slopes = jnp.exp2(-(8.0 / N_HEADS) * jnp.arange(1, N_HEADS + 1, dtype=F32))
    lam1 = lam.reshape(1).astype(F32)
    fg2 = fg.astype(F32).reshape(1, d_model)
    rw_b = jnp.zeros((d_model, LANES), BF16).at[:, :N_EXPERTS].set(rw.astype(BF16))
    rb2 = jnp.zeros((1, LANES), F32).at[0, :N_EXPERTS].set(rb.astype(F32))

    tm = min(ROW_TILE, t_len)
    zero_pre = jnp.zeros((bp, POOL_HALO, pool_w), F32)
    qp, kp, kpb, vp, vpb, sap, gpp, utp = _inproj(
        xp, zero_pre, ng2, w_in_b, qg, kg, bd, pw_b, ps2, nseg=1, seg_len=tm, start_pos=0, carry=True)
    atp = _attn_prompt(qp, kpb, vpb, slopes, lam1, sg)
    hp, tip, tgp, cntp = _outproj(atp.reshape(n_p, d_model), sap.reshape(n_p, d_model),
                                  gpp.reshape(n_p, d_model), xp.reshape(n_p, d_model),
                                  w_out_b, fg2, rw_b, rb2)

    pre_s = jnp.concatenate([jnp.zeros((bs, POOL_HALO - sp.shape[1], pool_w), F32), sp.astype(F32)], axis=1)
    qs, ks, ksb, vs, vsb, sas, gps, uts = _inproj(
        xs.reshape(1, n_s, d_model), pre_s, ng2, w_in_b, qg, kg, bd, pw_b, ps2,
        nseg=bs, seg_len=ts, start_pos=past, carry=False)
    ats = _attn_sample(qs.reshape(bs, ts, d_model), ksb.reshape(bs, ts, d_model), vsb.reshape(bs, ts, d_model),
                       ck.reshape(bs, past * N_HEADS, V_DIM), cv.reshape(bs, past * N_HEADS, V_DIM),
                       slopes, lam1, sg)
    hs, tis, tgs, cnts = _outproj(ats.reshape(n_s, d_model), sas.reshape(n_s, d_model),
                                  gps.reshape(n_s, d_model), xs.reshape(n_s, d_model),
                                  w_out_b, fg2, rw_b, rb2)

    n_tok = n_p + n_s
    topi = jnp.concatenate([tip, tis], axis=1)
    cnt = (cntp[:, 0] + cnts[:, 0]).astype(jnp.int32).reshape(TOP_K, N_EXPERTS)
    per_expert = jnp.sum(cnt, axis=0)
    padded = (per_expert + MOE_TILE - 1) // MOE_TILE * MOE_TILE
    pad_end = jnp.cumsum(padded)
    pad_start = pad_end - padded
    base = pad_start[None, :] + jnp.cumsum(cnt, axis=0) - cnt
    base_f = jnp.broadcast_to(base.reshape(-1, 1).astype(F32), (TOP_K * N_EXPERTS, LANES))
    tt = TOK_TILE
    rt = max(m * tt for m in range(1, RANK_TILE_MAX // tt + 1) if n_tok % (m * tt) == 0)
    tri = (lax.broadcasted_iota(jnp.int32, (rt, rt), 0) <= lax.broadcasted_iota(jnp.int32, (rt, rt), 1)).astype(BF16)
    dest = _rank(topi, base_f, tri)

    n_tiles = -(-(n_tok * TOP_K) // MOE_TILE) + N_EXPERTS
    n_used = (pad_end[-1] // MOE_TILE).astype(jnp.int32)
    tile_start = jnp.arange(n_tiles, dtype=jnp.int32) * MOE_TILE
    last_start = jnp.maximum(pad_end[-1] - MOE_TILE, 0)
    tile_expert = jnp.minimum(
        jnp.sum((jnp.minimum(tile_start, last_start)[:, None] >= pad_end[None, :]).astype(jnp.int32), axis=1),
        N_EXPERTS - 1)
    dest_p = _dest_tiles(dest[:, :n_p], tt)
    dest_s = _dest_tiles(dest[:, n_p:], tt)
    dest_all = jnp.concatenate([dest_p, dest_s], axis=0)
    assert MOE_TILE <= 2 * tt
    x_rows = _dispatch(pad_start + per_expert, padded - per_expert, (pad_end[-1:] // tt).astype(jnp.int32),
                       dest_all, hp, hs, n_tiles * MOE_TILE)
    e_ids = jnp.arange(N_EXPERTS, dtype=jnp.int32)
    later_present = jnp.logical_and(e_ids[None, :] > e_ids[:, None], (padded > 0)[None, :])
    next_of = jnp.min(jnp.where(later_present, e_ids[None, :], N_EXPERTS), axis=1)
    next_of = jnp.where(next_of == N_EXPERTS, -1, next_of).astype(jnp.int32)
    next_expert = jnp.sum(jnp.where(tile_expert[:, None] == e_ids[None, :], next_of[None, :], 0), axis=1)
    y_rows = _moe(tile_expert, n_used.reshape(1), next_expert.astype(jnp.int32), x_rows, fg2,
                  w_gate, b_gate, w_up, b_up, w_down, b_down, tm=MOE_TILE)

    yp = _combine(dest_p, y_rows, hp, tgp)
    ys = _combine(dest_s, y_rows, hs, tgs)

    heads = (N_HEADS, V_DIM)
    return (yp.reshape(bp, t_len, d_model), ys.reshape(bs, ts, d_model),
            kp.reshape(bp, t_len, *heads), vp.reshape(bp, t_len, *heads), utp[:, 1:],
            ks.reshape(bs, ts, *heads), vs.reshape(bs, ts, *heads), uts[:, 1:])


def kernel(x_prompt, x_sample, cache_k, cache_v, state_pool, norm_mix_g, w_in, q_norm_g, k_norm_g,
           lambda_q1, lambda_k1, lambda_q2, lambda_k2, subln_g, pool_w, pool_scale, w_out, norm_ffn_g,
           router_w, router_b, w_gate, b_gate, w_up, b_up, w_down, b_down):
    depth = w_in.shape[0]
    hp, hs = x_prompt, x_sample
    outs = [[] for _ in range(6)]
    for layer in range(depth):
        lam_init = 0.8 - 0.6 * math.exp(-0.3 * layer)
        lam = (jnp.exp(jnp.sum(lambda_q1[layer].astype(F32) * lambda_k1[layer].astype(F32)))
               - jnp.exp(jnp.sum(lambda_q2[layer].astype(F32) * lambda_k2[layer].astype(F32)))
               + lam_init)
        hp, hs, kp, vp, up, ks, vs, us = _layer(
            hp, hs, cache_k[layer], cache_v[layer], state_pool[layer], lam, lam_init,
            norm_mix_g[layer], w_in[layer], q_norm_g[layer], k_norm_g[layer], subln_g[layer],
            pool_w[layer], pool_scale[layer], w_out[layer], norm_ffn_g[layer],
            router_w[layer], router_b[layer], w_gate[layer], b_gate[layer], w_up[layer], b_up[layer],
            w_down[layer], b_down[layer])
        for lst, val in zip(outs, (kp, vp, up, ks, vs, us)):
            lst.append(val)
    return (hp, hs) + tuple(jnp.stack(o) for o in outs)
```
